```python
import jax, jax.numpy as jnp
from jax import lax
import numpy as np

D_MODEL = 1024
BATCH = 8
SEQ = 2048
DEPTH = 2
DEC_BATCH = 128
DEC_SEQ = 8
PAST_LEN = 16384
PAGE_SIZE = 128

N_MIXERS = 2
N_LRU_LAYERS = (DEPTH + 1) // 2
N_CCM_LAYERS = DEPTH // 2
D_RNN = D_MODEL * 5 // 4
LRU_HEADS = 10
LRU_BLOCK = D_RNN // LRU_HEADS
LRU_CONV_W = 4
LRU_C = 8.0
D_CONV = D_MODEL
CCM_CONV_W = 31
N_EXPERTS = 64
TOP_K = 8
N_GROUPS = 8
TOPK_GROUPS = 4
D_EXPERT = D_MODEL // 4
D_SHARED = D_MODEL // 4
ROUTED_SCALE = 2.5
LN_EPS = 1e-5
DN_ALPHA = (2 * DEPTH) ** 0.25
DN_BETA = (8 * DEPTH) ** -0.25

kernel_name = 'hybrid_rglru_conformer_moe_step'


def layer_norm(x, g, b):
    xf = x.astype(jnp.float32)
    mu = jnp.mean(xf, axis=-1, keepdims=True)
    var = jnp.mean(jnp.square(xf - mu), axis=-1, keepdims=True)
    y = (xf - mu) * lax.rsqrt(var + LN_EPS) * g.astype(jnp.float32) + b.astype(jnp.float32)
    return y.astype(x.dtype)


def causal_dwconv(xp, w):
    c = w.shape[-1]
    return lax.conv_general_dilated(xp, w[:, None, :].astype(xp.dtype), window_strides=(1,), padding='VALID',
                                    dimension_numbers=('NWC', 'WIO', 'NWC'), feature_group_count=c)


def rg_lru_mixer(x, conv_buf, h0, seq_start, w_in, b_in, conv_w, conv_b, w_a, b_a, w_x, b_x, lam, w_out, b_out):
    n, s, _ = x.shape
    proj = jnp.einsum('bsd,de->bse', x, w_in) + b_in
    gate = jax.nn.gelu(proj[..., :D_RNN])
    u = proj[..., D_RNN:]
    up = jnp.concatenate([conv_buf.astype(u.dtype), u], axis=1)
    new_buf = up[:, -(LRU_CONV_W - 1):]
    xc = causal_dwconv(up, conv_w) + conv_b
    xh = xc.reshape(n, s, LRU_HEADS, LRU_BLOCK)
    r = jax.nn.sigmoid(jnp.einsum('bshi,hij->bshj', xh, w_a).reshape(n, s, D_RNN) + b_a)
    i = jax.nn.sigmoid(jnp.einsum('bshi,hij->bshj', xh, w_x).reshape(n, s, D_RNN) + b_x)
    log_a = -LRU_C * r.astype(jnp.float32) * jax.nn.softplus(-lam.astype(jnp.float32))
    a = jnp.exp(log_a)
    mult = jnp.sqrt(-jnp.expm1(2.0 * log_a))
    if seq_start:
        mult = mult.at[:, 0].set(1.0)
    bterm = xc.astype(jnp.float32) * i.astype(jnp.float32) * mult
    bterm = bterm.at[:, 0].add(a[:, 0] * h0.astype(jnp.float32))

    def combine(left, right):
        a1, b1 = left
        a2, b2 = right
        return a1 * a2, a2 * b1 + b2

    _, h = lax.associative_scan(combine, (a, bterm), axis=1)
    y = jnp.einsum('bse,ed->bsd', h.astype(x.dtype) * gate, w_out) + b_out
    return y, new_buf, h[:, -1].astype(x.dtype)


def conformer_conv_mixer(x, conv_buf, w_in, b_in, dw_w, dw_b, ln_g, ln_b, w_out, b_out):
    p = jnp.einsum('bsd,de->bse', x, w_in) + b_in
    g = p[..., :D_CONV] * jax.nn.sigmoid(p[..., D_CONV:])
    gp = jnp.concatenate([conv_buf.astype(g.dtype), g], axis=1)
    new_buf = gp[:, -(CCM_CONV_W - 1):]
    c = causal_dwconv(gp, dw_w) + dw_b
    hdn = jax.nn.silu(layer_norm(c, ln_g, ln_b))
    y = jnp.einsum('bse,ed->bsd', hdn, w_out) + b_out
    return y, new_buf


def moe_route(x, router_w, router_bias):
    t = x.shape[0]
    scores = jax.nn.sigmoid(jnp.einsum('td,de->te', x.astype(jnp.float32), router_w.astype(jnp.float32)))
    biased = scores + router_bias.astype(jnp.float32)
    grp = biased.reshape(t, N_GROUPS, N_EXPERTS // N_GROUPS)
    group_score = jnp.sum(lax.top_k(grp, 2)[0], axis=-1)
    gidx = lax.top_k(group_score, TOPK_GROUPS)[1]
    gmask = jnp.sum(jax.nn.one_hot(gidx, N_GROUPS, dtype=jnp.float32), axis=-2)
    emask = jnp.repeat(gmask, N_EXPERTS // N_GROUPS, axis=-1)
    masked = jnp.where(emask > 0, biased, -jnp.inf)
    idx = lax.top_k(masked, TOP_K)[1]
    w = jnp.take_along_axis(scores, idx, axis=-1)
    w = w / jnp.sum(w, axis=-1, keepdims=True) * ROUTED_SCALE
    return jnp.sum(jax.nn.one_hot(idx, N_EXPERTS, dtype=jnp.float32) * w[..., None], axis=-2)


def moe_tokens(x, router_w, router_bias, w_gate, w_up, w_down, sh_gate, sh_up, sh_down):
    comb = moe_route(x, router_w, router_bias).astype(x.dtype)
    g = jnp.einsum('td,edf->tef', x, w_gate)
    u = jnp.einsum('td,edf->tef', x, w_up)
    hdn = jax.nn.silu(g) * u * comb[..., None]
    y = jnp.einsum('tef,efd->td', hdn, w_down)
    ys = jnp.einsum('tf,fd->td', jax.nn.silu(x @ sh_gate) * (x @ sh_up), sh_down)
    return y + ys


def moe(x, router_w, router_bias, w_gate, w_up, w_down, sh_gate, sh_up, sh_down):
    return lax.map(lambda xr: moe_tokens(xr, router_w, router_bias, w_gate, w_up, w_down, sh_gate, sh_up, sh_down), x)


def setup_inputs(seed: int = 0) -> dict:
    key = jax.random.key(seed)
    nk = iter(jax.random.split(key, 64))
    f32 = jnp.float32

    def nrm(shape, scale):
        return jax.random.normal(next(nk), shape, f32) * scale

    NA, NB = N_LRU_LAYERS, N_CCM_LAYERS
    x_prompt = nrm((BATCH, SEQ, D_MODEL), 1.0)
    x_sample = nrm((DEC_BATCH, DEC_SEQ, D_MODEL), 1.0)
    state_lru_conv = nrm((NA, DEC_BATCH, LRU_CONV_W - 1, D_RNN), 1.0)
    state_lru_h = nrm((NA, DEC_BATCH, D_RNN), 0.5)
    state_ccm_conv = nrm((NB, DEC_BATCH, CCM_CONV_W - 1, D_CONV), 1.0)
    a8 = jax.random.uniform(next(nk), (NA, D_RNN), f32, 0.9, 0.999)
    sa = a8 ** (1.0 / LRU_C)
    lru_lambda = jnp.log(sa) - jnp.log1p(-sa)
    return {
        'x_prompt': x_prompt,
        'x_sample': x_sample,
        'state_lru_conv': state_lru_conv,
        'state_lru_h': state_lru_h,
        'state_ccm_conv': state_ccm_conv,
        'lru_w_in': nrm((NA, D_MODEL, 2 * D_RNN), D_MODEL ** -0.5),
        'lru_b_in': nrm((NA, 2 * D_RNN), 0.01),
        'lru_conv_w': nrm((NA, LRU_CONV_W, D_RNN), LRU_CONV_W ** -0.5),
        'lru_conv_b': nrm((NA, D_RNN), 0.01),
        'lru_w_a': nrm((NA, LRU_HEADS, LRU_BLOCK, LRU_BLOCK), LRU_BLOCK ** -0.5),
        'lru_b_a': nrm((NA, D_RNN), 0.01),
        'lru_w_x': nrm((NA, LRU_HEADS, LRU_BLOCK, LRU_BLOCK), LRU_BLOCK ** -0.5),
        'lru_b_x': nrm((NA, D_RNN), 0.01),
        'lru_lambda': lru_lambda,
        'lru_w_out': nrm((NA, D_RNN, D_MODEL), D_RNN ** -0.5 * DN_BETA),
        'lru_b_out': nrm((NA, D_MODEL), 0.01),
        'ccm_w_in': nrm((NB, D_MODEL, 2 * D_CONV), D_MODEL ** -0.5),
        'ccm_b_in': nrm((NB, 2 * D_CONV), 0.01),
        'ccm_dw_w': nrm((NB, CCM_CONV_W, D_CONV), CCM_CONV_W ** -0.5),
        'ccm_dw_b': nrm((NB, D_CONV), 0.01),
        'ccm_ln_g': 1.0 + nrm((NB, D_CONV), 0.02),
        'ccm_ln_b': nrm((NB, D_CONV), 0.01),
        'ccm_w_out': nrm((NB, D_CONV, D_MODEL), D_CONV ** -0.5 * DN_BETA),
        'ccm_b_out': nrm((NB, D_MODEL), 0.01),
        'ln1_g': 1.0 + nrm((DEPTH, D_MODEL), 0.02),
        'ln1_b': nrm((DEPTH, D_MODEL), 0.01),
        'ln2_g': 1.0 + nrm((DEPTH, D_MODEL), 0.02),
        'ln2_b': nrm((DEPTH, D_MODEL), 0.01),
        'router_w': nrm((DEPTH, D_MODEL, N_EXPERTS), D_MODEL ** -0.5),
        'router_bias': nrm((DEPTH, N_EXPERTS), 0.01),
        'exp_w_gate': nrm((DEPTH, N_EXPERTS, D_MODEL, D_EXPERT), D_MODEL ** -0.5),
        'exp_w_up': nrm((DEPTH, N_EXPERTS, D_MODEL, D_EXPERT), D_MODEL ** -0.5),
        'exp_w_down': nrm((DEPTH, N_EXPERTS, D_EXPERT, D_MODEL), D_EXPERT ** -0.5 * DN_BETA),
        'sh_w_gate': nrm((DEPTH, D_MODEL, D_SHARED), D_MODEL ** -0.5),
        'sh_w_up': nrm((DEPTH, D_MODEL, D_SHARED), D_MODEL ** -0.5),
        'sh_w_down': nrm((DEPTH, D_SHARED, D_MODEL), D_SHARED ** -0.5 * DN_BETA),
    }


def reference(x_prompt, x_sample, state_lru_conv, state_lru_h, state_ccm_conv,
              lru_w_in, lru_b_in, lru_conv_w, lru_conv_b, lru_w_a, lru_b_a, lru_w_x, lru_b_x, lru_lambda,
              lru_w_out, lru_b_out,
              ccm_w_in, ccm_b_in, ccm_dw_w, ccm_dw_b, ccm_ln_g, ccm_ln_b, ccm_w_out, ccm_b_out,
              ln1_g, ln1_b, ln2_g, ln2_b,
              router_w, router_bias, exp_w_gate, exp_w_up, exp_w_down, sh_w_gate, sh_w_up, sh_w_down):

    def trunk(x, lru_conv, lru_h, ccm_conv, seq_start):
        new_lc, new_lh, new_cc = [], [], []
        for layer in range(DEPTH):
            j = layer // N_MIXERS
            if layer % N_MIXERS == 0:
                m, cb, hl = rg_lru_mixer(x, lru_conv[j], lru_h[j], seq_start, lru_w_in[j], lru_b_in[j],
                                         lru_conv_w[j], lru_conv_b[j], lru_w_a[j], lru_b_a[j], lru_w_x[j],
                                         lru_b_x[j], lru_lambda[j], lru_w_out[j], lru_b_out[j])
                new_lc.append(cb)
                new_lh.append(hl)
            else:
                m, cb = conformer_conv_mixer(x, ccm_conv[j], ccm_w_in[j], ccm_b_in[j], ccm_dw_w[j], ccm_dw_b[j],
                                             ccm_ln_g[j], ccm_ln_b[j], ccm_w_out[j], ccm_b_out[j])
                new_cc.append(cb)
            x = layer_norm(DN_ALPHA * x + m, ln1_g[layer], ln1_b[layer])
            f = moe(x, router_w[layer], router_bias[layer], exp_w_gate[layer], exp_w_up[layer], exp_w_down[layer],
                    sh_w_gate[layer], sh_w_up[layer], sh_w_down[layer])
            x = layer_norm(DN_ALPHA * x + f, ln2_g[layer], ln2_b[layer])
        return x, jnp.stack(new_lc), jnp.stack(new_lh), jnp.stack(new_cc)

    dt = x_prompt.dtype
    p_lru_conv = jnp.zeros((N_LRU_LAYERS, BATCH, LRU_CONV_W - 1, D_RNN), dt)
    p_lru_h = jnp.zeros((N_LRU_LAYERS, BATCH, D_RNN), dt)
    p_ccm_conv = jnp.zeros((N_CCM_LAYERS, BATCH, CCM_CONV_W - 1, D_CONV), dt)
    y_prompt, lru_conv_prompt, lru_h_prompt, ccm_conv_prompt = trunk(x_prompt, p_lru_conv, p_lru_h, p_ccm_conv, True)
    y_sample, lru_conv_sample, lru_h_sample, ccm_conv_sample = trunk(x_sample, state_lru_conv, state_lru_h,
                                                                     state_ccm_conv, False)
    return (y_prompt, y_sample, lru_conv_prompt, lru_h_prompt, ccm_conv_prompt,
            lru_conv_sample, lru_h_sample, ccm_conv_sample)
```

```python
import functools

import jax
import jax.numpy as jnp
from jax import lax
from jax.experimental import pallas as pl
from jax.experimental.pallas import tpu as pltpu

LANES = 128
SUBLANES = 8
LANE_BITS = LANES.bit_length() - 1
VMEM_HEADROOM_BYTES = 8 << 20
VMEM_LIMIT_CAP_BYTES = 56 << 20
CONV_STATE_BLOCK_BYTES = 4 << 20

LN_EPS = 1e-5
LRU_C = 8.0
N_GROUPS = 8
TOPK_GROUPS = 4
TOP_K = 8
ROUTED_SCALE = 2.5

MIXER_ROWS = 512
ROUTE_ROWS = 512
MOE_TILE = 256
MOE_CHUNKS = 4
SCATTER_UNROLL = 4
ACC_VREGS = 32
SCAN_VREGS = 16

F32 = jnp.float32
BF16 = jnp.bfloat16


def _layer_norm(v, g, b):
    mu = jnp.mean(v, axis=-1, keepdims=True)
    d = v - mu
    var = jnp.mean(d * d, axis=-1, keepdims=True)
    return d * lax.rsqrt(var + LN_EPS) * g + b


def _round_up(n, m):
    return (n + m - 1) // m * m


def _lru_body(x_ref, cs_ref, h0_ref, win_ref, bin_ref, cw_ref, cb_ref, wa_ref, ba_ref, wx_ref, bx_ref,
              lam_ref, wout_ref, bout_ref, g1_ref, b1_ref,
              x1_ref, ncs_ref, hl_ref,
              ubuf, a_s, b_s, h_s, *, seq_start, alpha, scan_panels):
    nb, tc, d_model = x_ref.shape
    d_rnn = h0_ref.shape[1]
    heads = wa_ref.shape[0]
    hb = d_rnn // heads
    kw = cw_ref.shape[0]
    hist0 = SUBLANES - (kw - 1)
    rows = nb * tc
    ti = pl.program_id(1)

    @pl.when(ti == 0)
    def _init_state():
        ubuf[:, hist0:SUBLANES, :] = cs_ref[...]
        h_s[...] = h0_ref[...]

    x = x_ref[...].reshape(rows, d_model)
    proj = jnp.dot(x.astype(BF16), win_ref[...], preferred_element_type=F32) + bin_ref[...]
    gate = jax.nn.gelu(proj[:, :d_rnn])
    u = proj[:, d_rnn:]
    ubuf[:, SUBLANES:SUBLANES + tc, :] = u.reshape(nb, tc, d_rnn)
    xc3 = jnp.broadcast_to(cb_ref[...].reshape(1, 1, d_rnn), (nb, tc, d_rnn))
    for j in range(kw):
        xc3 = xc3 + cw_ref[j:j + 1, :].reshape(1, 1, d_rnn) * ubuf[:, hist0 + j:hist0 + j + tc, :]
    tail = ubuf[:, tc + hist0:tc + SUBLANES, :]
    ncs_ref[...] = tail
    ubuf[:, hist0:SUBLANES, :] = tail
    xc = xc3.reshape(rows, d_rnn)

    xcb = xc.astype(BF16)
    ra = jnp.concatenate([jnp.dot(xcb[:, h * hb:(h + 1) * hb], wa_ref[h], preferred_element_type=F32)
                          for h in range(heads)], axis=1)
    ia = jnp.concatenate([jnp.dot(xcb[:, h * hb:(h + 1) * hb], wx_ref[h], preferred_element_type=F32)
                          for h in range(heads)], axis=1)
    r = jax.nn.sigmoid(ra + ba_ref[...])
    gi = jax.nn.sigmoid(ia + bx_ref[...])
    lam = lam_ref[...]
    softplus_neg_lam = jnp.maximum(-lam, 0.0) + jnp.log1p(jnp.exp(-jnp.abs(lam)))
    log_a = (-LRU_C * softplus_neg_lam) * r
    a = jnp.exp(log_a)
    mult = jnp.sqrt(-jnp.tanh(log_a) * (a * a + 1.0))
    if seq_start:
        t_in_seq = lax.broadcasted_iota(jnp.int32, (rows, 1), 0) % tc
        mult = jnp.where((t_in_seq == 0) & (ti == 0), 1.0, mult)
    bterm = xc * gi * mult
    npan = d_rnn // LANES
    for p in range(npan):
        a_s[p] = a[:, p * LANES:(p + 1) * LANES]
        b_s[p] = bterm[:, p * LANES:(p + 1) * LANES]

    for p0 in range(0, npan, scan_panels):
        group = range(p0, min(p0 + scan_panels, npan))
        hs = [h_s[:, p * LANES:(p + 1) * LANES] for p in group]
        for t in range(tc):
            rs = pl.ds(t, nb, stride=tc)
            for k, p in enumerate(group):
                hs[k] = a_s[p, rs, :] * hs[k] + b_s[p, rs, :]
                b_s[p, rs, :] = hs[k]
        for k, p in enumerate(group):
            h_s[:, p * LANES:(p + 1) * LANES] = hs[k]
    hl_ref[...] = h_s[...]

    h_all = jnp.concatenate([b_s[p] for p in range(npan)], axis=1)
    y = jnp.dot((h_all * gate).astype(BF16), wout_ref[...], preferred_element_type=F32) + bout_ref[...]
    x1 = _layer_norm(alpha * x + y, g1_ref[...], b1_ref[...])
    x1_ref[...] = x1.reshape(nb, tc, d_model)


def _lru_mixer(x, conv_state, h0, win, b_in, conv_w, conv_b, wa, ba, wx, bx, lam, wout, bout, g1, b1, *,
               seq_start, alpha):
    n, s, d_model = x.shape
    d_rnn = h0.shape[1]
    kw = conv_w.shape[0]
    tc = min(s, MIXER_ROWS // SUBLANES)
    nb = min(n, MIXER_ROWS // tc)
    rows = nb * tc
    scan_panels = max(1, SCAN_VREGS * SUBLANES // nb)
    row = lambda v: v.reshape(1, -1)
    const2 = lambda i, t: (0, 0)
    const3 = lambda i, t: (0, 0, 0)
    vmem = (2 * 2 * rows * d_model * 4 + 2 * (win.size + wout.size + wa.size + wx.size) * 2
            + (nb * (tc + SUBLANES) + 2 * rows + nb) * d_rnn * 4 + 4 * rows * 2 * d_rnn * 4 + VMEM_HEADROOM_BYTES)
    return pl.pallas_call(
        functools.partial(_lru_body, seq_start=seq_start, alpha=alpha, scan_panels=scan_panels),
        grid=(n // nb, s // tc),
        in_specs=[
            pl.BlockSpec((nb, tc, d_model), lambda i, t: (i, t, 0)),
            pl.BlockSpec((nb, kw - 1, d_rnn), lambda i, t: (i, 0, 0)),
            pl.BlockSpec((nb, d_rnn), lambda i, t: (i, 0)),
            pl.BlockSpec(win.shape, const2), pl.BlockSpec((1, 2 * d_rnn), const2),
            pl.BlockSpec(conv_w.shape, const2), pl.BlockSpec((1, d_rnn), const2),
            pl.BlockSpec(wa.shape, const3), pl.BlockSpec((1, d_rnn), const2),
            pl.BlockSpec(wx.shape, const3), pl.BlockSpec((1, d_rnn), const2),
            pl.BlockSpec((1, d_rnn), const2),
            pl.BlockSpec(wout.shape, const2), pl.BlockSpec((1, d_model), const2),
            pl.BlockSpec((1, d_model), const2), pl.BlockSpec((1, d_model), const2),
        ],
        out_specs=[
            pl.BlockSpec((nb, tc, d_model), lambda i, t: (i, t, 0)),
            pl.BlockSpec((nb, kw - 1, d_rnn), lambda i, t: (i, 0, 0)),
            pl.BlockSpec((nb, d_rnn), lambda i, t: (i, 0)),
        ],
        out_shape=[jax.ShapeDtypeStruct(x.shape, F32),
                   jax.ShapeDtypeStruct((n, kw - 1, d_rnn), F32),
                   jax.ShapeDtypeStruct((n, d_rnn), F32)],
        scratch_shapes=[pltpu.VMEM((nb, tc + SUBLANES, d_rnn), F32),
                        pltpu.VMEM((d_rnn // LANES, rows, LANES), F32),
                        pltpu.VMEM((d_rnn // LANES, rows, LANES), F32),
                        pltpu.VMEM((nb, d_rnn), F32)],
        compiler_params=pltpu.CompilerParams(dimension_semantics=("arbitrary", "arbitrary"),
                                             vmem_limit_bytes=min(vmem, VMEM_LIMIT_CAP_BYTES)),
        name="lru_mixer",
    )(x, conv_state, h0, win, row(b_in), conv_w, row(conv_b), wa, row(ba), wx, row(bx), row(lam),
      wout, row(bout), row(g1), row(b1))


def _ccm_body(x_ref, cs_ref, win_ref, bin_ref, dw_ref, dwb_ref, lng_ref, lnb_ref, wout_ref, bout_ref,
              g1_ref, b1_ref, x1_ref, ncs_ref, gbuf, cbuf, *, alpha, conv_cols):
    nb, tc, d_model = x_ref.shape
    kw, d_conv = dw_ref.shape
    hist_rows = _round_up(kw - 1, SUBLANES)
    hist0 = hist_rows - (kw - 1)
    rows = nb * tc
    ti = pl.program_id(1)

    @pl.when(ti == 0)
    def _init_state():
        gbuf[:, hist0:hist_rows, :] = cs_ref[...]

    x = x_ref[...].reshape(rows, d_model)
    p = jnp.dot(x.astype(BF16), win_ref[...], preferred_element_type=F32) + bin_ref[...]
    glu = p[:, :d_conv] * jax.nn.sigmoid(p[:, d_conv:])
    gbuf[:, hist_rows:hist_rows + tc, :] = glu.reshape(nb, tc, d_conv)

    def conv_seq(n, carry):
        for c0 in range(0, d_conv, conv_cols):
            cols = slice(c0, c0 + conv_cols)
            acc = jnp.broadcast_to(dwb_ref[:, cols], (tc, conv_cols))
            for j in range(kw):
                acc = acc + dw_ref[j:j + 1, cols] * gbuf[n, hist0 + j:hist0 + j + tc, cols]
            cbuf[n, :, cols] = acc
        return carry
    lax.fori_loop(0, nb, conv_seq, 0)

    tail = gbuf[:, tc + hist0:tc + hist_rows, :]
    ncs_ref[...] = tail
    gbuf[:, hist0:hist_rows, :] = tail

    c = cbuf[...].reshape(rows, d_conv)
    hn = _layer_norm(c, lng_ref[...], lnb_ref[...])
    hdn = hn * jax.nn.sigmoid(hn)
    y = jnp.dot(hdn.astype(BF16), wout_ref[...], preferred_element_type=F32) + bout_ref[...]
    x1 = _layer_norm(alpha * x + y, g1_ref[...], b1_ref[...])
    x1_ref[...] = x1.reshape(nb, tc, d_model)


def _ccm_mixer(x, conv_state, win, b_in, dw_w, dw_b, ln_g, ln_b, wout, bout, g1, b1, *, alpha):
    n, s, d_model = x.shape
    kw, d_conv = dw_w.shape
    tc = min(s, MIXER_ROWS // SUBLANES)
    nb = min(n, MIXER_ROWS // tc)
    while nb > SUBLANES and nb * (kw - 1) * d_conv * 4 > CONV_STATE_BLOCK_BYTES:
        nb //= 2
    rows = nb * tc
    hist_rows = _round_up(kw - 1, SUBLANES)
    conv_cols = min(d_conv, max(LANES, (ACC_VREGS * SUBLANES * LANES // tc) // LANES * LANES))
    while d_conv % conv_cols:
        conv_cols -= LANES
    row = lambda v: v.reshape(1, -1)
    const2 = lambda i, t: (0, 0)
    vmem = (2 * 2 * rows * d_model * 4 + 2 * (win.size + wout.size) * 2
            + (nb * (tc + hist_rows) + rows) * d_conv * 4 + 4 * nb * (kw - 1) * d_conv * 4
            + 4 * rows * 2 * d_conv * 4 + VMEM_HEADROOM_BYTES)
    return pl.pallas_call(
        functools.partial(_ccm_body, alpha=alpha, conv_cols=conv_cols),
        grid=(n // nb, s // tc),
        in_specs=[
            pl.BlockSpec((nb, tc, d_model), lambda i, t: (i, t, 0)),
            pl.BlockSpec((nb, kw - 1, d_conv), lambda i, t: (i, 0, 0)),
            pl.BlockSpec(win.shape, const2), pl.BlockSpec((1, 2 * d_conv), const2),
            pl.BlockSpec(dw_w.shape, const2), pl.BlockSpec((1, d_conv), const2),
            pl.BlockSpec((1, d_conv), const2), pl.BlockSpec((1, d_conv), const2),
            pl.BlockSpec(wout.shape, const2), pl.BlockSpec((1, d_model), const2),
            pl.BlockSpec((1, d_model), const2), pl.BlockSpec((1, d_model), const2),
        ],
        out_specs=[
            pl.BlockSpec((nb, tc, d_model), lambda i, t: (i, t, 0)),
            pl.BlockSpec((nb, kw - 1, d_conv), lambda i, t: (i, 0, 0)),
        ],
        out_shape=[jax.ShapeDtypeStruct(x.shape, F32),
                   jax.ShapeDtypeStruct((n, kw - 1, d_conv), F32)],
        scratch_shapes=[pltpu.VMEM((nb, tc + hist_rows, d_conv), F32),
                        pltpu.VMEM((nb, tc, d_conv), F32)],
        compiler_params=pltpu.CompilerParams(dimension_semantics=("arbitrary", "arbitrary"),
                                             vmem_limit_bytes=min(vmem, VMEM_LIMIT_CAP_BYTES)),
        name="ccm_mixer",
    )(x, conv_state, win, row(b_in), dw_w, row(dw_b), row(ln_g), row(ln_b), wout, row(bout), row(g1), row(b1))


def _first_argmax(v, axis):
    m = jnp.max(v, axis=axis, keepdims=True)
    iota = lax.broadcasted_iota(jnp.int32, v.shape, axis)
    first = jnp.min(jnp.where(v == m, iota, v.shape[axis]), axis=axis, keepdims=True)
    return m, iota == first, first


def _route_body(x_ref, rwt_ref, rb_ref, sg_ref, su_ref, sd_ref, resid_ref, idx_ref, w_ref, *, alpha):
    x = x_ref[...]
    n_exp = rwt_ref.shape[0]
    tr = x.shape[0]
    logits = lax.dot_general(rwt_ref[...], x, (((1,), (1,)), ((), ())),
                             precision=lax.Precision.HIGHEST, preferred_element_type=F32)
    scores = jax.nn.sigmoid(logits)
    biased = scores + rb_ref[...]
    per_group = n_exp // N_GROUPS
    grp = biased.reshape(N_GROUPS, per_group, tr)
    m1, hit1, _ = _first_argmax(grp, 1)
    m2 = jnp.max(jnp.where(hit1, -jnp.inf, grp), axis=1, keepdims=True)
    gscore = (m1 + m2).reshape(N_GROUPS, tr)
    gsel = jnp.zeros((N_GROUPS, tr), F32)
    for _ in range(TOPK_GROUPS):
        _, hit, _ = _first_argmax(gscore, 0)
        gsel = jnp.where(hit, 1.0, gsel)
        gscore = jnp.where(hit, -jnp.inf, gscore)
    emask = jnp.broadcast_to(gsel.reshape(N_GROUPS, 1, tr), (N_GROUPS, per_group, tr)).reshape(n_exp, tr)
    masked = jnp.where(emask > 0.0, biased, -jnp.inf)
    idx_rows, w_rows = [], []
    for _ in range(TOP_K):
        _, hit, first = _first_argmax(masked, 0)
        idx_rows.append(first)
        w_rows.append(jnp.sum(jnp.where(hit, scores, 0.0), axis=0, keepdims=True))
        masked = jnp.where(hit, -jnp.inf, masked)
    w = jnp.concatenate(w_rows, axis=0)
    idx_ref[...] = jnp.concatenate(idx_rows, axis=0)
    w_ref[...] = w / jnp.sum(w, axis=0, keepdims=True) * ROUTED_SCALE

    xb = x.astype(BF16)
    sgate = jnp.dot(xb, sg_ref[...], preferred_element_type=F32)
    sup = jnp.dot(xb, su_ref[...], preferred_element_type=F32)
    hs = (sgate * jax.nn.sigmoid(sgate) * sup).astype(BF16)
    resid_ref[...] = alpha * x + jnp.dot(hs, sd_ref[...], preferred_element_type=F32)


def _route_shared(x, router_wt, router_bias, sg, su, sd, *, alpha):
    t, d_model = x.shape
    n_exp = router_wt.shape[0]
    tr = min(t, ROUTE_ROWS)
    const2 = lambda i: (0, 0)
    vmem = 2 * 2 * tr * d_model * 4 + 2 * (router_wt.size * 4 + 3 * sg.size * 2) + VMEM_HEADROOM_BYTES
    return pl.pallas_call(
        functools.partial(_route_body, alpha=alpha),
        grid=(t // tr,),
        in_specs=[pl.BlockSpec((tr, d_model), lambda i: (i, 0)),
                  pl.BlockSpec(router_wt.shape, const2), pl.BlockSpec((n_exp, 1), const2),
                  pl.BlockSpec(sg.shape, const2), pl.BlockSpec(su.shape, const2), pl.BlockSpec(sd.shape, const2)],
        out_specs=[pl.BlockSpec((tr, d_model), lambda i: (i, 0)),
                   pl.BlockSpec((TOP_K, tr), lambda i: (0, i)),
                   pl.BlockSpec((TOP_K, tr), lambda i: (0, i))],
        out_shape=[jax.ShapeDtypeStruct((t, d_model), F32),
                   jax.ShapeDtypeStruct((TOP_K, t), jnp.int32),
                   jax.ShapeDtypeStruct((TOP_K, t), F32)],
        compiler_params=pltpu.CompilerParams(dimension_semantics=("arbitrary",), vmem_limit_bytes=min(vmem, VMEM_LIMIT_CAP_BYTES)),
        name="route_shared",
    )(x, router_wt, router_bias.reshape(n_exp, 1), sg, su, sd)


def _moe_body(tile_e_ref, tile_c_ref, tile_f_ref,
              tok_ref, wts_ref, wg_ref, wu_ref, wd_ref, g2_ref, b2_ref,
              xp_hbm, xs_hbm, rp_hbm, rs_hbm,
              op_hbm, os_hbm,
              x_s, y_s, xt_s, ot_s, wgu_s, wd_s, tok_sm, wts_sm, sem_in, sem_out, sem_sm,
              *, tm, cp_rows, cs_rows, npan):
    chunk = cp_rows + cs_rows
    chunk_stride = chunk + SUBLANES
    tile_stride = tm + SUBLANES
    d_ff = wg_ref.shape[2]
    j = pl.program_id(0)
    c = tile_c_ref[j]
    flags = tile_f_ref[j]
    is_active = (flags & 1) != 0
    seg_first = (flags & 2) != 0
    chunk_first = (flags & 4) != 0
    chunk_last = (flags & 8) != 0

    def chunk_copies(hbm_p, hbm_s, vmem, sem, to_vmem):
        copies = []
        for p in range(npan):
            for hbm, nrows, row0 in ((hbm_p, cp_rows, 0), (hbm_s, cs_rows, cp_rows)):
                h = hbm.at[pl.ds(c * nrows, nrows), pl.ds(p * LANES, LANES)]
                v = vmem.at[pl.ds(p * chunk_stride + row0, nrows), :]
                copies.append(pltpu.make_async_copy(h, v, sem.at[0]) if to_vmem
                              else pltpu.make_async_copy(v, h, sem.at[0]))
        return copies

    @pl.when(chunk_first)
    def _load_chunk():
        copies = (chunk_copies(xp_hbm, xs_hbm, x_s, sem_in, True)
                  + chunk_copies(rp_hbm, rs_hbm, y_s, sem_in, True))
        for cp in copies:
            cp.start()
        zeros = jnp.zeros((SUBLANES, LANES), F32)
        for p in range(npan):
            x_s[pl.ds(p * chunk_stride + chunk, SUBLANES), :] = zeros
            y_s[pl.ds(p * chunk_stride + chunk, SUBLANES), :] = zeros
        for cp in copies:
            cp.wait()

    @pl.when(seg_first)
    def _cast_weights():
        wgu_s[:, :d_ff] = wg_ref[0].astype(BF16)
        wgu_s[:, d_ff:] = wu_ref[0].astype(BF16)
        wd_s[...] = wd_ref[0].astype(BF16)

    @pl.when(is_active)
    def _tile():
        cp_tok = pltpu.make_async_copy(tok_ref.at[0], tok_sm, sem_sm.at[0])
        cp_wts = pltpu.make_async_copy(wts_ref.at[0], wts_sm, sem_sm.at[1])
        cp_tok.start()
        cp_wts.start()
        cp_tok.wait()
        cp_wts.wait()

        def gather(i, carry):
            t = tok_sm[i >> LANE_BITS, i & (LANES - 1)]
            xt_s[pl.ds(i, npan, stride=tile_stride), :] = x_s[pl.ds(t, npan, stride=chunk_stride), :]
            return carry
        lax.fori_loop(0, tm, gather, 0)

        lhs = jnp.concatenate([xt_s[pl.ds(p * tile_stride, tm), :] for p in range(npan)], axis=1).astype(BF16)
        gu = jnp.dot(lhs, wgu_s[...], preferred_element_type=F32)
        g = gu[:, :d_ff]
        hidden = (g * jax.nn.sigmoid(g) * gu[:, d_ff:]).astype(BF16)
        o = jnp.dot(hidden, wd_s[...], preferred_element_type=F32)
        for p in range(npan):
            ot_s[pl.ds(p * tile_stride, tm), :] = o[:, p * LANES:(p + 1) * LANES]

        def combine(ib, carry):
            toks, rows_new = [], []
            for k in range(SCATTER_UNROLL):
                i = ib * SCATTER_UNROLL + k
                t = tok_sm[i >> LANE_BITS, i & (LANES - 1)]
                w = wts_sm[i >> LANE_BITS, i & (LANES - 1)]
                row = ot_s[pl.ds(i, npan, stride=tile_stride), :]
                toks.append(t)
                rows_new.append(y_s[pl.ds(t, npan, stride=chunk_stride), :] + w * row)
            for t, new in zip(toks, rows_new):
                y_s[pl.ds(t, npan, stride=chunk_stride), :] = new
            return carry
        lax.fori_loop(0, tm // SCATTER_UNROLL, combine, 0)

    @pl.when(chunk_last)
    def _finish_chunk():
        blk = MOE_TILE
        g2 = g2_ref[...]
        b2 = b2_ref[...]

        def ln_block(rb, carry):
            r0 = pl.multiple_of(rb * blk, blk)
            v = jnp.concatenate([y_s[pl.ds(p * chunk_stride + r0, blk), :] for p in range(npan)], axis=1)
            out = _layer_norm(v, g2, b2)
            for p in range(npan):
                y_s[pl.ds(p * chunk_stride + r0, blk), :] = out[:, p * LANES:(p + 1) * LANES]
            return carry
        lax.fori_loop(0, chunk // blk, ln_block, 0)
        copies = chunk_copies(op_hbm, os_hbm, y_s, sem_out, False)
        for cp in copies:
            cp.start()
        for cp in copies:
            cp.wait()


def _moe_routed(tile_e, tile_c, tile_f, tok, wts, wg, wu, wd, g2, b2, xp, xs, rp, rs):
    n_tiles = tile_e.shape[0]
    n_exp, d_model, d_ff = wg.shape
    tm = tok.shape[1] * LANES
    npan = d_model // LANES
    assert npan == SUBLANES, "one token row must fill exactly one (8, 128) register"
    cp_rows, cs_rows = xp.shape[0] // MOE_CHUNKS, xs.shape[0] // MOE_CHUNKS
    chunk = cp_rows + cs_rows
    assert chunk % MOE_TILE == 0 and cs_rows % SUBLANES == 0
    chunk_stride = chunk + SUBLANES
    tile_stride = tm + SUBLANES
    any_spec = pl.BlockSpec(memory_space=pl.ANY)
    tile_map = lambda j, te, tc, tf: (j, 0, 0)
    expert_map = lambda j, te, tc, tf: (te[j], 0, 0)
    const2 = lambda j, te, tc, tf: (0, 0)
    vmem = (2 * npan * chunk_stride * LANES * 4 + 2 * npan * tile_stride * LANES * 4 + 3 * d_model * d_ff * 2
            + 2 * 3 * d_model * d_ff * 4 + VMEM_HEADROOM_BYTES)
    return pl.pallas_call(
        functools.partial(_moe_body, tm=tm, cp_rows=cp_rows, cs_rows=cs_rows, npan=npan),
        grid_spec=pltpu.PrefetchScalarGridSpec(
            num_scalar_prefetch=3,
            grid=(n_tiles,),
            in_specs=[
                pl.BlockSpec((1, tm // LANES, LANES), tile_map),
                pl.BlockSpec((1, tm // LANES, LANES), tile_map),
                pl.BlockSpec((1, d_model, d_ff), expert_map),
                pl.BlockSpec((1, d_model, d_ff), expert_map),
                pl.BlockSpec((1, d_ff, d_model), expert_map),
                pl.BlockSpec((1, d_model), const2),
                pl.BlockSpec((1, d_model), const2),
                any_spec, any_spec, any_spec, any_spec,
            ],
            out_specs=[any_spec, any_spec],
            scratch_shapes=[
                pltpu.VMEM((npan * chunk_stride, LANES), F32),
                pltpu.VMEM((npan * chunk_stride, LANES), F32),
                pltpu.VMEM((npan * tile_stride, LANES), F32),
                pltpu.VMEM((npan * tile_stride, LANES), F32),
                pltpu.VMEM((d_model, 2 * d_ff), BF16),
                pltpu.VMEM((d_ff, d_model), BF16),
                pltpu.SMEM((tm // LANES, LANES), jnp.int32),
                pltpu.SMEM((tm // LANES, LANES), F32),
                pltpu.SemaphoreType.DMA((1,)),
                pltpu.SemaphoreType.DMA((1,)),
                pltpu.SemaphoreType.DMA((2,)),
            ]),
        out_shape=[jax.ShapeDtypeStruct(xp.shape, F32), jax.ShapeDtypeStruct(xs.shape, F32)],
        compiler_params=pltpu.CompilerParams(dimension_semantics=("arbitrary",), vmem_limit_bytes=min(vmem, VMEM_LIMIT_CAP_BYTES)),
        name="moe_routed",
    )(tile_e, tile_c, tile_f, tok, wts, wg, wu, wd, g2.reshape(1, d_model), b2.reshape(1, d_model),
      xp, xs, rp, rs)


def _build_tiles(idx_p, w_p, idx_s, w_s, n_exp):
    k, tp = idx_p.shape
    ts = idx_s.shape[1]
    tm = MOE_TILE
    cp, cs = tp // MOE_CHUNKS, ts // MOE_CHUNKS
    chunk = cp + cs
    nseg = MOE_CHUNKS * n_exp
    split = lambda a_p, a_s: jnp.concatenate([a_p.T.reshape(MOE_CHUNKS, cp, k), a_s.T.reshape(MOE_CHUNKS, cs, k)],
                                             axis=1)
    idx = split(idx_p, idx_s)
    w = split(w_p, w_s)
    onehot = (idx[..., None] == jnp.arange(n_exp, dtype=jnp.int32)).any(axis=2).astype(jnp.int32)
    csum = jnp.cumsum(onehot, axis=1)
    counts = csum[:, -1, :].reshape(nseg)
    tiles_per_seg = (counts + tm - 1) // tm
    tile_end = jnp.cumsum(tiles_per_seg)
    tile_base = tile_end - tiles_per_seg
    n_active = tile_end[-1]
    n_tiles = (MOE_CHUNKS * chunk * k) // tm + nseg
    rank = jnp.take_along_axis(csum - onehot, idx, axis=2)
    seg = jnp.arange(MOE_CHUNKS, dtype=jnp.int32)[:, None, None] * n_exp + idx
    dest = (tile_base[seg] * tm + rank).reshape(-1)
    tok_local = jnp.broadcast_to(jnp.arange(chunk, dtype=jnp.int32)[None, :, None], idx.shape).reshape(-1)
    tok = jnp.full((n_tiles * tm,), chunk, jnp.int32).at[dest].set(tok_local, unique_indices=True)
    wts = jnp.zeros((n_tiles * tm,), F32).at[dest].set(w.reshape(-1), unique_indices=True)

    jj = jnp.arange(n_tiles, dtype=jnp.int32)
    active = jj < n_active
    seg_of_tile = jnp.searchsorted(tile_end, jnp.minimum(jj, n_active - 1), side="right").astype(jnp.int32)
    tile_e = seg_of_tile % n_exp
    tile_c = seg_of_tile // n_exp
    seg_first = active & (jj == tile_base[seg_of_tile])
    prev_c = jnp.concatenate([jnp.full((1,), -1, jnp.int32), tile_c[:-1]])
    next_c = jnp.concatenate([tile_c[1:], jnp.full((1,), -1, jnp.int32)])
    chunk_first = active & (tile_c != prev_c)
    chunk_last = active & ((tile_c != next_c) | (jj == n_active - 1))
    flags = (active.astype(jnp.int32) + 2 * seg_first.astype(jnp.int32) + 4 * chunk_first.astype(jnp.int32)
             + 8 * chunk_last.astype(jnp.int32))
    shape = (n_tiles, tm // LANES, LANES)
    return tile_e, tile_c, flags, tok.reshape(shape), wts.reshape(shape)


def kernel(x_prompt, x_sample, state_lru_conv, state_lru_h, state_ccm_conv, lru_w_in, lru_b_in, lru_conv_w, lru_conv_b, lru_w_a, lru_b_a, lru_w_x, lru_b_x, lru_lambda, lru_w_out, lru_b_out, ccm_w_in, ccm_b_in, ccm_dw_w, ccm_dw_b, ccm_ln_g, ccm_ln_b, ccm_w_out, ccm_b_out, ln1_g, ln1_b, ln2_g, ln2_b, router_w, router_bias, exp_w_gate, exp_w_up, exp_w_down, sh_w_gate, sh_w_up, sh_w_down):
    depth = ln1_g.shape[0]
    alpha = (2 * depth) ** 0.25
    n_exp = router_w.shape[2]
    bp, sp, d_model = x_prompt.shape
    bs, ss, _ = x_sample.shape
    kw_lru = lru_conv_w.shape[1]
    kw_ccm = ccm_dw_w.shape[1]
    d_rnn = lru_conv_w.shape[2]
    d_conv = ccm_dw_w.shape[2]
    bf = lambda a: a.astype(BF16)

    zero_lru_conv = jnp.zeros((bp, kw_lru - 1, d_rnn), F32)
    zero_lru_h = jnp.zeros((bp, d_rnn), F32)
    zero_ccm_conv = jnp.zeros((bp, kw_ccm - 1, d_conv), F32)

    xp, xs = x_prompt, x_sample
    lru_conv_p, lru_h_p, ccm_conv_p = [], [], []
    lru_conv_s, lru_h_s, ccm_conv_s = [], [], []
    for layer in range(depth):
        j = layer // 2
        if layer % 2 == 0:
            weights = (bf(lru_w_in[j]), lru_b_in[j], lru_conv_w[j], lru_conv_b[j], bf(lru_w_a[j]), lru_b_a[j],
                       bf(lru_w_x[j]), lru_b_x[j], lru_lambda[j], bf(lru_w_out[j]), lru_b_out[j],
                       ln1_g[layer], ln1_b[layer])
            xp, cb, hl = _lru_mixer(xp, zero_lru_conv, zero_lru_h, *weights, seq_start=True, alpha=alpha)
            lru_conv_p.append(cb)
            lru_h_p.append(hl)
            xs, cb, hl = _lru_mixer(xs, state_lru_conv[j], state_lru_h[j], *weights, seq_start=False, alpha=alpha)
            lru_conv_s.append(cb)
            lru_h_s.append(hl)
        else:
            weights = (bf(ccm_w_in[j]), ccm_b_in[j], ccm_dw_w[j], ccm_dw_b[j], ccm_ln_g[j], ccm_ln_b[j],
                       bf(ccm_w_out[j]), ccm_b_out[j], ln1_g[layer], ln1_b[layer])
            xp, cb = _ccm_mixer(xp, zero_ccm_conv, *weights, alpha=alpha)
            ccm_conv_p.append(cb)
            xs, cb = _ccm_mixer(xs, state_ccm_conv[j], *weights, alpha=alpha)
            ccm_conv_s.append(cb)

        shared = (router_w[layer].T, router_bias[layer], bf(sh_w_gate[layer]), bf(sh_w_up[layer]),
                  bf(sh_w_down[layer]))
        xp2 = xp.reshape(bp * sp, d_model)
        xs2 = xs.reshape(bs * ss, d_model)
        resid_p, idx_p, w_p = _route_shared(xp2, *shared, alpha=alpha)
        resid_s, idx_s, w_s = _route_shared(xs2, *shared, alpha=alpha)
        tiles = _build_tiles(idx_p, w_p, idx_s, w_s, n_exp)
        xp2, xs2 = _moe_routed(*tiles, exp_w_gate[layer], exp_w_up[layer], exp_w_down[layer],
                               ln2_g[layer], ln2_b[layer], xp2, xs2, resid_p, resid_s)
        xp = xp2.reshape(bp, sp, d_model)
        xs = xs2.reshape(bs, ss, d_model)

    return (xp, xs, jnp.stack(lru_conv_p), jnp.stack(lru_h_p), jnp.stack(ccm_conv_p),
            jnp.stack(lru_conv_s), jnp.stack(lru_h_s), jnp.stack(ccm_conv_s))
```

```python
import functools

import jax
import jax.numpy as jnp
from jax import lax
from jax.experimental import pallas as pl
from jax.experimental.pallas import tpu as pltpu

LANES = 128
SUBLANES = 8
LANE_BITS = LANES.bit_length() - 1
VMEM_HEADROOM_BYTES = 8 << 20
VMEM_LIMIT_CAP_BYTES = 56 << 20
CONV_STATE_BLOCK_BYTES = 4 << 20

LN_EPS = 1e-5
LRU_C = 8.0
N_GROUPS = 8
TOPK_GROUPS = 4
TOP_K = 8
ROUTED_SCALE = 2.5

MIXER_ROWS = 512
ROUTE_ROWS = 512
MOE_TILE = 256
MOE_CHUNKS = 4
SCATTER_UNROLL = 16
GATHER_UNROLL = 16
ACC_VREGS = 32
SCAN_VREGS = 16

F32 = jnp.float32
BF16 = jnp.bfloat16


def _layer_norm(v, g, b):
    mu = jnp.mean(v, axis=-1, keepdims=True)
    d = v - mu
    var = jnp.mean(d * d, axis=-1, keepdims=True)
    return d * lax.rsqrt(var + LN_EPS) * g + b


def _round_up(n, m):
    return (n + m - 1) // m * m


def _lru_body(x_ref, cs_ref, h0_ref, win_ref, bin_ref, cw_ref, cb_ref, wa_ref, ba_ref, wx_ref, bx_ref,
              lam_ref, wout_ref, bout_ref, g1_ref, b1_ref,
              x1_ref, ncs_ref, hl_ref,
              ubuf, a_s, b_s, h_s, *, seq_start, alpha, scan_panels):
    nb, tc, d_model = x_ref.shape
    d_rnn = h0_ref.shape[1]
    heads = wa_ref.shape[0]
    hb = d_rnn // heads
    kw = cw_ref.shape[0]
    hist0 = SUBLANES - (kw - 1)
    rows = nb * tc
    ti = pl.program_id(1)

    @pl.when(ti == 0)
    def _init_state():
        ubuf[:, hist0:SUBLANES, :] = cs_ref[...]
        h_s[...] = h0_ref[...]

    x = x_ref[...].reshape(rows, d_model)
    proj = jnp.dot(x.astype(BF16), win_ref[...], preferred_element_type=F32) + bin_ref[...]
    gate = jax.nn.gelu(proj[:, :d_rnn])
    u = proj[:, d_rnn:]
    ubuf[:, SUBLANES:SUBLANES + tc, :] = u.reshape(nb, tc, d_rnn)
    xc3 = jnp.broadcast_to(cb_ref[...].reshape(1, 1, d_rnn), (nb, tc, d_rnn))
    for j in range(kw):
        xc3 = xc3 + cw_ref[j:j + 1, :].reshape(1, 1, d_rnn) * ubuf[:, hist0 + j:hist0 + j + tc, :]
    tail = ubuf[:, tc + hist0:tc + SUBLANES, :]
    ncs_ref[...] = tail
    ubuf[:, hist0:SUBLANES, :] = tail
    xc = xc3.reshape(rows, d_rnn)

    xcb = xc.astype(BF16)
    ra = jnp.concatenate([jnp.dot(xcb[:, h * hb:(h + 1) * hb], wa_ref[h], preferred_element_type=F32)
                          for h in range(heads)], axis=1)
    ia = jnp.concatenate([jnp.dot(xcb[:, h * hb:(h + 1) * hb], wx_ref[h], preferred_element_type=F32)
                          for h in range(heads)], axis=1)
    r = jax.nn.sigmoid(ra + ba_ref[...])
    gi = jax.nn.sigmoid(ia + bx_ref[...])
    lam = lam_ref[...]
    softplus_neg_lam = jnp.maximum(-lam, 0.0) + jnp.log1p(jnp.exp(-jnp.abs(lam)))
    log_a = (-LRU_C * softplus_neg_lam) * r
    a = jnp.exp(log_a)
    mult = jnp.sqrt(-jnp.tanh(log_a) * (a * a + 1.0))
    if seq_start:
        t_in_seq = lax.broadcasted_iota(jnp.int32, (rows, 1), 0) % tc
        mult = jnp.where((t_in_seq == 0) & (ti == 0), 1.0, mult)
    bterm = xc * gi * mult
    npan = d_rnn // LANES
    for p in range(npan):
        a_s[p] = a[:, p * LANES:(p + 1) * LANES]
        b_s[p] = bterm[:, p * LANES:(p + 1) * LANES]

    for p0 in range(0, npan, scan_panels):
        group = range(p0, min(p0 + scan_panels, npan))
        hs = [h_s[:, p * LANES:(p + 1) * LANES] for p in group]
        for t in range(tc):
            rs = pl.ds(t, nb, stride=tc)
            for k, p in enumerate(group):
                hs[k] = a_s[p, rs, :] * hs[k] + b_s[p, rs, :]
                b_s[p, rs, :] = hs[k]
        for k, p in enumerate(group):
            h_s[:, p * LANES:(p + 1) * LANES] = hs[k]
    hl_ref[...] = h_s[...]

    h_all = jnp.concatenate([b_s[p] for p in range(npan)], axis=1)
    y = jnp.dot((h_all * gate).astype(BF16), wout_ref[...], preferred_element_type=F32) + bout_ref[...]
    x1 = _layer_norm(alpha * x + y, g1_ref[...], b1_ref[...])
    x1_ref[...] = x1.reshape(nb, tc, d_model)


def _lru_mixer(x, conv_state, h0, win, b_in, conv_w, conv_b, wa, ba, wx, bx, lam, wout, bout, g1, b1, *,
               seq_start, alpha):
    n, s, d_model = x.shape
    d_rnn = h0.shape[1]
    kw = conv_w.shape[0]
    tc = min(s, MIXER_ROWS // SUBLANES)
    nb = min(n, MIXER_ROWS // tc)
    rows = nb * tc
    scan_panels = max(1, SCAN_VREGS * SUBLANES // nb)
    row = lambda v: v.reshape(1, -1)
    const2 = lambda i, t: (0, 0)
    const3 = lambda i, t: (0, 0, 0)
    vmem = (2 * 2 * rows * d_model * 4 + 2 * (win.size + wout.size + wa.size + wx.size) * 2
            + (nb * (tc + SUBLANES) + 2 * rows + nb) * d_rnn * 4 + 4 * rows * 2 * d_rnn * 4 + VMEM_HEADROOM_BYTES)
    return pl.pallas_call(
        functools.partial(_lru_body, seq_start=seq_start, alpha=alpha, scan_panels=scan_panels),
        grid=(n // nb, s // tc),
        in_specs=[
            pl.BlockSpec((nb, tc, d_model), lambda i, t: (i, t, 0)),
            pl.BlockSpec((nb, kw - 1, d_rnn), lambda i, t: (i, 0, 0)),
            pl.BlockSpec((nb, d_rnn), lambda i, t: (i, 0)),
            pl.BlockSpec(win.shape, const2), pl.BlockSpec((1, 2 * d_rnn), const2),
            pl.BlockSpec(conv_w.shape, const2), pl.BlockSpec((1, d_rnn), const2),
            pl.BlockSpec(wa.shape, const3), pl.BlockSpec((1, d_rnn), const2),
            pl.BlockSpec(wx.shape, const3), pl.BlockSpec((1, d_rnn), const2),
            pl.BlockSpec((1, d_rnn), const2),
            pl.BlockSpec(wout.shape, const2), pl.BlockSpec((1, d_model), const2),
            pl.BlockSpec((1, d_model), const2), pl.BlockSpec((1, d_model), const2),
        ],
        out_specs=[
            pl.BlockSpec((nb, tc, d_model), lambda i, t: (i, t, 0)),
            pl.BlockSpec((nb, kw - 1, d_rnn), lambda i, t: (i, 0, 0)),
            pl.BlockSpec((nb, d_rnn), lambda i, t: (i, 0)),
        ],
        out_shape=[jax.ShapeDtypeStruct(x.shape, F32),
                   jax.ShapeDtypeStruct((n, kw - 1, d_rnn), F32),
                   jax.ShapeDtypeStruct((n, d_rnn), F32)],
        scratch_shapes=[pltpu.VMEM((nb, tc + SUBLANES, d_rnn), F32),
                        pltpu.VMEM((d_rnn // LANES, rows, LANES), F32),
                        pltpu.VMEM((d_rnn // LANES, rows, LANES), F32),
                        pltpu.VMEM((nb, d_rnn), F32)],
        compiler_params=pltpu.CompilerParams(dimension_semantics=("arbitrary", "arbitrary"),
                                             vmem_limit_bytes=min(vmem, VMEM_LIMIT_CAP_BYTES)),
        name="lru_mixer",
    )(x, conv_state, h0, win, row(b_in), conv_w, row(conv_b), wa, row(ba), wx, row(bx), row(lam),
      wout, row(bout), row(g1), row(b1))


def _ccm_body(x_ref, cs_ref, win_ref, bin_ref, dw_ref, dwb_ref, lng_ref, lnb_ref, wout_ref, bout_ref,
              g1_ref, b1_ref, x1_ref, ncs_ref, gbuf, cbuf, *, alpha, conv_cols):
    nb, tc, d_model = x_ref.shape
    kw, d_conv = dw_ref.shape
    hist_rows = _round_up(kw - 1, SUBLANES)
    hist0 = hist_rows - (kw - 1)
    rows = nb * tc
    ti = pl.program_id(1)

    @pl.when(ti == 0)
    def _init_state():
        gbuf[:, hist0:hist_rows, :] = cs_ref[...]

    x = x_ref[...].reshape(rows, d_model)
    p = jnp.dot(x.astype(BF16), win_ref[...], preferred_element_type=F32) + bin_ref[...]
    glu = p[:, :d_conv] * jax.nn.sigmoid(p[:, d_conv:])
    gbuf[:, hist_rows:hist_rows + tc, :] = glu.reshape(nb, tc, d_conv)

    def conv_seq(n, carry):
        for c0 in range(0, d_conv, conv_cols):
            cols = slice(c0, c0 + conv_cols)
            acc = jnp.broadcast_to(dwb_ref[:, cols], (tc, conv_cols))
            for j in range(kw):
                acc = acc + dw_ref[j:j + 1, cols] * gbuf[n, hist0 + j:hist0 + j + tc, cols]
            cbuf[n, :, cols] = acc
        return carry
    lax.fori_loop(0, nb, conv_seq, 0)

    tail = gbuf[:, tc + hist0:tc + hist_rows, :]
    ncs_ref[...] = tail
    gbuf[:, hist0:hist_rows, :] = tail

    c = cbuf[...].reshape(rows, d_conv)
    hn = _layer_norm(c, lng_ref[...], lnb_ref[...])
    hdn = hn * jax.nn.sigmoid(hn)
    y = jnp.dot(hdn.astype(BF16), wout_ref[...], preferred_element_type=F32) + bout_ref[...]
    x1 = _layer_norm(alpha * x + y, g1_ref[...], b1_ref[...])
    x1_ref[...] = x1.reshape(nb, tc, d_model)


def _ccm_mixer(x, conv_state, win, b_in, dw_w, dw_b, ln_g, ln_b, wout, bout, g1, b1, *, alpha):
    n, s, d_model = x.shape
    kw, d_conv = dw_w.shape
    tc = min(s, MIXER_ROWS // SUBLANES)
    nb = min(n, MIXER_ROWS // tc)
    while nb > SUBLANES and nb * (kw - 1) * d_conv * 4 > CONV_STATE_BLOCK_BYTES:
        nb //= 2
    rows = nb * tc
    hist_rows = _round_up(kw - 1, SUBLANES)
    conv_cols = min(d_conv, max(LANES, (ACC_VREGS * SUBLANES * LANES // tc) // LANES * LANES))
    while d_conv % conv_cols:
        conv_cols -= LANES
    row = lambda v: v.reshape(1, -1)
    const2 = lambda i, t: (0, 0)
    vmem = (2 * 2 * rows * d_model * 4 + 2 * (win.size + wout.size) * 2
            + (nb * (tc + hist_rows) + rows) * d_conv * 4 + 4 * nb * (kw - 1) * d_conv * 4
            + 4 * rows * 2 * d_conv * 4 + VMEM_HEADROOM_BYTES)
    return pl.pallas_call(
        functools.partial(_ccm_body, alpha=alpha, conv_cols=conv_cols),
        grid=(n // nb, s // tc),
        in_specs=[
            pl.BlockSpec((nb, tc, d_model), lambda i, t: (i, t, 0)),
            pl.BlockSpec((nb, kw - 1, d_conv), lambda i, t: (i, 0, 0)),
            pl.BlockSpec(win.shape, const2), pl.BlockSpec((1, 2 * d_conv), const2),
            pl.BlockSpec(dw_w.shape, const2), pl.BlockSpec((1, d_conv), const2),
            pl.BlockSpec((1, d_conv), const2), pl.BlockSpec((1, d_conv), const2),
            pl.BlockSpec(wout.shape, const2), pl.BlockSpec((1, d_model), const2),
            pl.BlockSpec((1, d_model), const2), pl.BlockSpec((1, d_model), const2),
        ],
        out_specs=[
            pl.BlockSpec((nb, tc, d_model), lambda i, t: (i, t, 0)),
            pl.BlockSpec((nb, kw - 1, d_conv), lambda i, t: (i, 0, 0)),
        ],
        out_shape=[jax.ShapeDtypeStruct(x.shape, F32),
                   jax.ShapeDtypeStruct((n, kw - 1, d_conv), F32)],
        scratch_shapes=[pltpu.VMEM((nb, tc + hist_rows, d_conv), F32),
                        pltpu.VMEM((nb, tc, d_conv), F32)],
        compiler_params=pltpu.CompilerParams(dimension_semantics=("arbitrary", "arbitrary"),
                                             vmem_limit_bytes=min(vmem, VMEM_LIMIT_CAP_BYTES)),
        name="ccm_mixer",
    )(x, conv_state, win, row(b_in), dw_w, row(dw_b), row(ln_g), row(ln_b), wout, row(bout), row(g1), row(b1))


def _first_argmax(v, axis):
    m = jnp.max(v, axis=axis, keepdims=True)
    iota = lax.broadcasted_iota(jnp.int32, v.shape, axis)
    first = jnp.min(jnp.where(v == m, iota, v.shape[axis]), axis=axis, keepdims=True)
    return m, iota == first, first


def _route_body(x_ref, rwt_ref, rb_ref, sg_ref, su_ref, sd_ref, cnt_in_ref,
                resid_ref, idx_ref, w_ref, rank_ref, cnt_ref, tri_s, run_s, *, alpha, steps_per_chunk):
    x = x_ref[...]
    n_exp = rwt_ref.shape[0]
    tr = x.shape[0]
    step = pl.program_id(0)

    @pl.when(step == 0)
    def _build_prefix_matrix():
        r = lax.broadcasted_iota(jnp.int32, (tr, tr), 0)
        c = lax.broadcasted_iota(jnp.int32, (tr, tr), 1)
        tri_s[...] = jnp.where(r <= c, 1.0, 0.0).astype(BF16)

    @pl.when(step % steps_per_chunk == 0)
    def _start_chunk():
        run_s[...] = cnt_in_ref[0]

    logits = lax.dot_general(rwt_ref[...], x, (((1,), (1,)), ((), ())),
                             precision=lax.Precision.HIGHEST, preferred_element_type=F32)
    scores = jax.nn.sigmoid(logits)
    biased = scores + rb_ref[...]
    per_group = n_exp // N_GROUPS
    grp = biased.reshape(N_GROUPS, per_group, tr)
    m1, hit1, _ = _first_argmax(grp, 1)
    m2 = jnp.max(jnp.where(hit1, -jnp.inf, grp), axis=1, keepdims=True)
    gscore = (m1 + m2).reshape(N_GROUPS, tr)
    gsel = jnp.zeros((N_GROUPS, tr), F32)
    for _ in range(TOPK_GROUPS):
        _, hit, _ = _first_argmax(gscore, 0)
        gsel = jnp.where(hit, 1.0, gsel)
        gscore = jnp.where(hit, -jnp.inf, gscore)
    emask = jnp.broadcast_to(gsel.reshape(N_GROUPS, 1, tr), (N_GROUPS, per_group, tr)).reshape(n_exp, tr)
    masked = jnp.where(emask > 0.0, biased, -jnp.inf)
    idx_rows, w_rows, hits = [], [], []
    for _ in range(TOP_K):
        _, hit, first = _first_argmax(masked, 0)
        idx_rows.append(first)
        hits.append(hit)
        w_rows.append(jnp.sum(jnp.where(hit, scores, 0.0), axis=0, keepdims=True))
        masked = jnp.where(hit, -jnp.inf, masked)
    w = jnp.concatenate(w_rows, axis=0)
    idx_ref[...] = jnp.concatenate(idx_rows, axis=0)
    w_ref[...] = w / jnp.sum(w, axis=0, keepdims=True) * ROUTED_SCALE

    sel = jnp.zeros((n_exp, tr), F32)
    for hit in hits:
        sel = jnp.where(hit, 1.0, sel)
    before = jnp.dot(sel.astype(BF16), tri_s[...], preferred_element_type=F32) - sel + run_s[:, 0:1]
    rank_ref[...] = jnp.concatenate(
        [jnp.sum(jnp.where(hit, before, 0.0), axis=0, keepdims=True) for hit in hits], axis=0).astype(jnp.int32)
    run_s[...] = run_s[...] + jnp.sum(sel, axis=1, keepdims=True)
    cnt_ref[0] = run_s[...]

    xb = x.astype(BF16)
    sgate = jnp.dot(xb, sg_ref[...], preferred_element_type=F32)
    sup = jnp.dot(xb, su_ref[...], preferred_element_type=F32)
    hs = (sgate * jax.nn.sigmoid(sgate) * sup).astype(BF16)
    resid_ref[...] = alpha * x + jnp.dot(hs, sd_ref[...], preferred_element_type=F32)


def _route_shared(x, router_wt, router_bias, sg, su, sd, cnt_in, *, alpha):
    t, d_model = x.shape
    n_exp = router_wt.shape[0]
    rows_per_chunk = t // MOE_CHUNKS
    tr = min(rows_per_chunk, ROUTE_ROWS)
    steps_per_chunk = rows_per_chunk // tr
    const2 = lambda i: (0, 0)
    chunk_map = lambda i: (i // steps_per_chunk, 0, 0)
    vmem = 2 * 2 * tr * d_model * 4 + 2 * (router_wt.size * 4 + 3 * sg.size * 2) + VMEM_HEADROOM_BYTES
    return pl.pallas_call(
        functools.partial(_route_body, alpha=alpha, steps_per_chunk=steps_per_chunk),
        grid=(t // tr,),
        in_specs=[pl.BlockSpec((tr, d_model), lambda i: (i, 0)),
                  pl.BlockSpec(router_wt.shape, const2), pl.BlockSpec((n_exp, 1), const2),
                  pl.BlockSpec(sg.shape, const2), pl.BlockSpec(su.shape, const2), pl.BlockSpec(sd.shape, const2),
                  pl.BlockSpec((1, n_exp, LANES), chunk_map)],
        out_specs=[pl.BlockSpec((tr, d_model), lambda i: (i, 0)),
                   pl.BlockSpec((TOP_K, tr), lambda i: (0, i)),
                   pl.BlockSpec((TOP_K, tr), lambda i: (0, i)),
                   pl.BlockSpec((TOP_K, tr), lambda i: (0, i)),
                   pl.BlockSpec((1, n_exp, LANES), chunk_map)],
        out_shape=[jax.ShapeDtypeStruct((t, d_model), F32),
                   jax.ShapeDtypeStruct((TOP_K, t), jnp.int32),
                   jax.ShapeDtypeStruct((TOP_K, t), F32),
                   jax.ShapeDtypeStruct((TOP_K, t), jnp.int32),
                   jax.ShapeDtypeStruct((MOE_CHUNKS, n_exp, LANES), F32)],
        scratch_shapes=[pltpu.VMEM((tr, tr), BF16), pltpu.VMEM((n_exp, LANES), F32)],
        compiler_params=pltpu.CompilerParams(dimension_semantics=("arbitrary",),
                                             vmem_limit_bytes=min(vmem, VMEM_LIMIT_CAP_BYTES)),
        name="route_shared",
    )(x, router_wt, router_bias.reshape(n_exp, 1), sg, su, sd, cnt_in)


def _moe_body(tile_e_ref, tile_c_ref, tile_f_ref, seg_cnt_ref, seg_base_ref,
              dest_ref, wsel_ref, wg_ref, wu_ref, wd_ref, g2_ref, b2_ref,
              xp_hbm, xs_hbm, rp_hbm, rs_hbm,
              op_hbm, os_hbm,
              x_s, y_s, xt_s, ot_s, wgu_s, wd_s, tok_sm, wts_sm, dest_sm, wsel_sm, sem_in, sem_out, sem_sm,
              *, tm, cp_rows, cs_rows, npan, tiles_per_chunk):
    chunk = cp_rows + cs_rows
    chunk_stride = chunk + SUBLANES
    tile_stride = tm + SUBLANES
    d_ff = wg_ref.shape[2]
    n_exp = seg_cnt_ref.shape[0] // MOE_CHUNKS
    top_k = dest_ref.shape[1]
    j = pl.program_id(0)
    c = tile_c_ref[j]
    list0 = (j - c * tiles_per_chunk) * tm
    flags = tile_f_ref[j]
    is_active = (flags & 1) != 0
    seg_first = (flags & 2) != 0
    chunk_first = (flags & 4) != 0
    chunk_last = (flags & 8) != 0

    def chunk_copies(hbm_p, hbm_s, vmem, sem, to_vmem):
        copies = []
        for p in range(npan):
            for hbm, nrows, row0 in ((hbm_p, cp_rows, 0), (hbm_s, cs_rows, cp_rows)):
                h = hbm.at[pl.ds(c * nrows, nrows), pl.ds(p * LANES, LANES)]
                v = vmem.at[pl.ds(p * chunk_stride + row0, nrows), :]
                copies.append(pltpu.make_async_copy(h, v, sem.at[0]) if to_vmem
                              else pltpu.make_async_copy(v, h, sem.at[0]))
        return copies

    @pl.when(chunk_first)
    def _load_chunk():
        copies = (chunk_copies(xp_hbm, xs_hbm, x_s, sem_in, True)
                  + chunk_copies(rp_hbm, rs_hbm, y_s, sem_in, True))
        for cp in copies:
            cp.start()
        zeros = jnp.zeros((SUBLANES, LANES), F32)
        for p in range(npan):
            x_s[pl.ds(p * chunk_stride + chunk, SUBLANES), :] = zeros
            y_s[pl.ds(p * chunk_stride + chunk, SUBLANES), :] = zeros
        for k in range(top_k):
            cp_dest = pltpu.make_async_copy(dest_ref.at[0, k], dest_sm, sem_sm.at[0])
            cp_wsel = pltpu.make_async_copy(wsel_ref.at[0, k], wsel_sm, sem_sm.at[1])
            cp_dest.start()
            cp_wsel.start()
            cp_dest.wait()
            cp_wsel.wait()

            def fill(r, carry):
                for lane in range(LANES):
                    d = dest_sm[r, lane]
                    tok_sm[d] = r * LANES + lane
                    wts_sm[d] = wsel_sm[r, lane]
                return carry
            lax.fori_loop(0, chunk // LANES, fill, 0)

        def pad_segment(e, carry):
            n = seg_cnt_ref[c * n_exp + e]
            base = seg_base_ref[c * n_exp + e]

            def pad(i, carry2):
                tok_sm[i] = chunk
                wts_sm[i] = 0.0
                return carry2
            lax.fori_loop(base + n, base + (n + tm - 1) // tm * tm, pad, 0)
            return carry
        lax.fori_loop(0, n_exp, pad_segment, 0)

        for cp in copies:
            cp.wait()

    @pl.when(seg_first)
    def _cast_weights():
        wgu_s[:, :d_ff] = wg_ref[0].astype(BF16)
        wgu_s[:, d_ff:] = wu_ref[0].astype(BF16)
        wd_s[...] = wd_ref[0].astype(BF16)

    @pl.when(is_active)
    def _tile():
        def gather(ib, carry):
            for u in range(GATHER_UNROLL):
                i = ib * GATHER_UNROLL + u
                t = tok_sm[list0 + i]
                xt_s[pl.ds(i, npan, stride=tile_stride), :] = x_s[pl.ds(t, npan, stride=chunk_stride), :]
            return carry
        lax.fori_loop(0, tm // GATHER_UNROLL, gather, 0)

        lhs = jnp.concatenate([xt_s[pl.ds(p * tile_stride, tm), :] for p in range(npan)], axis=1).astype(BF16)
        gu = jnp.dot(lhs, wgu_s[...], preferred_element_type=F32)
        g = gu[:, :d_ff]
        hidden = (g * jax.nn.sigmoid(g) * gu[:, d_ff:]).astype(BF16)
        o = jnp.dot(hidden, wd_s[...], preferred_element_type=F32)
        for p in range(npan):
            ot_s[pl.ds(p * tile_stride, tm), :] = o[:, p * LANES:(p + 1) * LANES]

        def combine(ib, carry):
            toks, rows_new = [], []
            for k in range(SCATTER_UNROLL):
                i = ib * SCATTER_UNROLL + k
                t = tok_sm[list0 + i]
                w = wts_sm[list0 + i]
                row = ot_s[pl.ds(i, npan, stride=tile_stride), :]
                toks.append(t)
                rows_new.append(y_s[pl.ds(t, npan, stride=chunk_stride), :] + w * row)
            for t, new in zip(toks, rows_new):
                y_s[pl.ds(t, npan, stride=chunk_stride), :] = new
            return carry
        lax.fori_loop(0, tm // SCATTER_UNROLL, combine, 0)

    @pl.when(chunk_last)
    def _finish_chunk():
        blk = MOE_TILE
        g2 = g2_ref[...]
        b2 = b2_ref[...]

        def ln_block(rb, carry):
            r0 = pl.multiple_of(rb * blk, blk)
            v = jnp.concatenate([y_s[pl.ds(p * chunk_stride + r0, blk), :] for p in range(npan)], axis=1)
            out = _layer_norm(v, g2, b2)
            for p in range(npan):
                y_s[pl.ds(p * chunk_stride + r0, blk), :] = out[:, p * LANES:(p + 1) * LANES]
            return carry
        lax.fori_loop(0, chunk // blk, ln_block, 0)
        copies = chunk_copies(op_hbm, os_hbm, y_s, sem_out, False)
        for cp in copies:
            cp.start()
        for cp in copies:
            cp.wait()


def _moe_routed(tile_e, tile_c, tile_f, seg_cnt, seg_base, dest, wsel, wg, wu, wd, g2, b2, xp, xs, rp, rs):
    n_tiles = tile_e.shape[0]
    n_exp, d_model, d_ff = wg.shape
    tm = MOE_TILE
    top_k = dest.shape[1]
    tiles_per_chunk = n_tiles // MOE_CHUNKS
    npan = d_model // LANES
    assert npan == SUBLANES, "one token row must fill exactly one (8, 128) register"
    cp_rows, cs_rows = xp.shape[0] // MOE_CHUNKS, xs.shape[0] // MOE_CHUNKS
    chunk = cp_rows + cs_rows
    assert chunk % MOE_TILE == 0 and cs_rows % SUBLANES == 0
    chunk_stride = chunk + SUBLANES
    tile_stride = tm + SUBLANES
    any_spec = pl.BlockSpec(memory_space=pl.ANY)
    chunk_map = lambda j, te, tc, *_: (tc[j], 0, 0, 0)
    expert_map = lambda j, te, *_: (te[j], 0, 0)
    const2 = lambda j, *_: (0, 0)
    list_block = (1, top_k, chunk // LANES, LANES)
    vmem = (2 * npan * chunk_stride * LANES * 4 + 2 * npan * tile_stride * LANES * 4 + 3 * d_model * d_ff * 2
            + 2 * 3 * d_model * d_ff * 4 + 4 * top_k * chunk * 4 + VMEM_HEADROOM_BYTES)
    return pl.pallas_call(
        functools.partial(_moe_body, tm=tm, cp_rows=cp_rows, cs_rows=cs_rows, npan=npan,
                          tiles_per_chunk=tiles_per_chunk),
        grid_spec=pltpu.PrefetchScalarGridSpec(
            num_scalar_prefetch=5,
            grid=(n_tiles,),
            in_specs=[
                pl.BlockSpec(list_block, chunk_map),
                pl.BlockSpec(list_block, chunk_map),
                pl.BlockSpec((1, d_model, d_ff), expert_map),
                pl.BlockSpec((1, d_model, d_ff), expert_map),
                pl.BlockSpec((1, d_ff, d_model), expert_map),
                pl.BlockSpec((1, d_model), const2),
                pl.BlockSpec((1, d_model), const2),
                any_spec, any_spec, any_spec, any_spec,
            ],
            out_specs=[any_spec, any_spec],
            scratch_shapes=[
                pltpu.VMEM((npan * chunk_stride, LANES), F32),
                pltpu.VMEM((npan * chunk_stride, LANES), F32),
                pltpu.VMEM((npan * tile_stride, LANES), F32),
                pltpu.VMEM((npan * tile_stride, LANES), F32),
                pltpu.VMEM((d_model, 2 * d_ff), BF16),
                pltpu.VMEM((d_ff, d_model), BF16),
                pltpu.SMEM((tiles_per_chunk * tm,), jnp.int32),
                pltpu.SMEM((tiles_per_chunk * tm,), F32),
                pltpu.SMEM((chunk // LANES, LANES), jnp.int32),
                pltpu.SMEM((chunk // LANES, LANES), F32),
                pltpu.SemaphoreType.DMA((1,)),
                pltpu.SemaphoreType.DMA((1,)),
                pltpu.SemaphoreType.DMA((2,)),
            ]),
        out_shape=[jax.ShapeDtypeStruct(xp.shape, F32), jax.ShapeDtypeStruct(xs.shape, F32)],
        compiler_params=pltpu.CompilerParams(dimension_semantics=("arbitrary",),
                                             vmem_limit_bytes=min(vmem, VMEM_LIMIT_CAP_BYTES)),
        name="moe_routed",
    )(tile_e, tile_c, tile_f, seg_cnt, seg_base, dest, wsel, wg, wu, wd,
      g2.reshape(1, d_model), b2.reshape(1, d_model), xp, xs, rp, rs)


def _plan_tiles(cnt, idx_p, w_p, rank_p, idx_s, w_s, rank_s):
    k, tp = idx_p.shape
    ts = idx_s.shape[1]
    n_chunks, n_exp = cnt.shape
    tm = MOE_TILE
    cp, cs = tp // n_chunks, ts // n_chunks
    chunk = cp + cs
    tiles_per_chunk = (chunk * k) // tm + n_exp
    experts = jnp.arange(n_exp, dtype=jnp.int32)

    seg_tiles = (cnt + tm - 1) // tm
    seg_tile_end = jnp.sum(jnp.where(experts[None, :] <= experts[:, None], seg_tiles[:, None, :], 0), axis=-1)
    seg_tile_base = seg_tile_end - seg_tiles
    seg_base = seg_tile_base * tm
    n_active = seg_tile_end[:, -1]

    def rows_of(idx, rank, per_chunk):
        idx3 = idx.reshape(k, n_chunks, per_chunk)
        base = jnp.sum(jnp.where(idx3[..., None] == experts, seg_base[None, :, None, :], 0), axis=-1)
        return jnp.transpose(base + rank.reshape(k, n_chunks, per_chunk), (1, 0, 2))
    by_chunk = lambda a, per_chunk: jnp.transpose(a.reshape(k, n_chunks, per_chunk), (1, 0, 2))
    list_shape = (n_chunks, k, chunk // LANES, LANES)
    dest = jnp.concatenate([rows_of(idx_p, rank_p, cp), rows_of(idx_s, rank_s, cs)], axis=2).reshape(list_shape)
    wsel = jnp.concatenate([by_chunk(w_p, cp), by_chunk(w_s, cs)], axis=2).reshape(list_shape)

    jl = jnp.arange(tiles_per_chunk, dtype=jnp.int32)[None, :]
    active = jl < n_active[:, None]
    jc = jnp.minimum(jl, n_active[:, None] - 1)
    tile_e = jnp.sum((seg_tile_end[:, None, :] <= jc[:, :, None]).astype(jnp.int32), axis=-1)
    first_tile = jnp.sum(jnp.where(tile_e[..., None] == experts, seg_tile_base[:, None, :], 0), axis=-1)
    flags = (active.astype(jnp.int32) + 2 * (active & (jl == first_tile)).astype(jnp.int32)
             + 4 * (jl == 0).astype(jnp.int32) + 8 * (jl == n_active[:, None] - 1).astype(jnp.int32))
    tile_c = jnp.broadcast_to(jnp.arange(n_chunks, dtype=jnp.int32)[:, None], tile_e.shape)
    flat = lambda a: a.reshape(-1)
    return flat(tile_e), flat(tile_c), flat(flags), flat(cnt), flat(seg_base), dest, wsel


def kernel(x_prompt, x_sample, state_lru_conv, state_lru_h, state_ccm_conv, lru_w_in, lru_b_in, lru_conv_w, lru_conv_b, lru_w_a, lru_b_a, lru_w_x, lru_b_x, lru_lambda, lru_w_out, lru_b_out, ccm_w_in, ccm_b_in, ccm_dw_w, ccm_dw_b, ccm_ln_g, ccm_ln_b, ccm_w_out, ccm_b_out, ln1_g, ln1_b, ln2_g, ln2_b, router_w, router_bias, exp_w_gate, exp_w_up, exp_w_down, sh_w_gate, sh_w_up, sh_w_down):
    depth = ln1_g.shape[0]
    alpha = (2 * depth) ** 0.25
    n_exp = router_w.shape[2]
    bp, sp, d_model = x_prompt.shape
    bs, ss, _ = x_sample.shape
    kw_lru = lru_conv_w.shape[1]
    kw_ccm = ccm_dw_w.shape[1]
    d_rnn = lru_conv_w.shape[2]
    d_conv = ccm_dw_w.shape[2]
    bf = lambda a: a.astype(BF16)

    zero_lru_conv = jnp.zeros((bp, kw_lru - 1, d_rnn), F32)
    zero_lru_h = jnp.zeros((bp, d_rnn), F32)
    zero_ccm_conv = jnp.zeros((bp, kw_ccm - 1, d_conv), F32)

    xp, xs = x_prompt, x_sample
    lru_conv_p, lru_h_p, ccm_conv_p = [], [], []
    lru_conv_s, lru_h_s, ccm_conv_s = [], [], []
    for layer in range(depth):
        j = layer // 2
        if layer % 2 == 0:
            weights = (bf(lru_w_in[j]), lru_b_in[j], lru_conv_w[j], lru_conv_b[j], bf(lru_w_a[j]), lru_b_a[j],
                       bf(lru_w_x[j]), lru_b_x[j], lru_lambda[j], bf(lru_w_out[j]), lru_b_out[j],
                       ln1_g[layer], ln1_b[layer])
            xp, cb, hl = _lru_mixer(xp, zero_lru_conv, zero_lru_h, *weights, seq_start=True, alpha=alpha)
            lru_conv_p.append(cb)
            lru_h_p.append(hl)
            xs, cb, hl = _lru_mixer(xs, state_lru_conv[j], state_lru_h[j], *weights, seq_start=False, alpha=alpha)
            lru_conv_s.append(cb)
            lru_h_s.append(hl)
        else:
            weights = (bf(ccm_w_in[j]), ccm_b_in[j], ccm_dw_w[j], ccm_dw_b[j], ccm_ln_g[j], ccm_ln_b[j],
                       bf(ccm_w_out[j]), ccm_b_out[j], ln1_g[layer], ln1_b[layer])
            xp, cb = _ccm_mixer(xp, zero_ccm_conv, *weights, alpha=alpha)
            ccm_conv_p.append(cb)
            xs, cb = _ccm_mixer(xs, state_ccm_conv[j], *weights, alpha=alpha)
            ccm_conv_s.append(cb)

        shared = (router_w[layer].T, router_bias[layer], bf(sh_w_gate[layer]), bf(sh_w_up[layer]),
                  bf(sh_w_down[layer]))
        xp2 = xp.reshape(bp * sp, d_model)
        xs2 = xs.reshape(bs * ss, d_model)
        no_tokens = jnp.zeros((MOE_CHUNKS, n_exp, LANES), F32)
        resid_p, idx_p, w_p, rank_p, cnt_p = _route_shared(xp2, *shared, no_tokens, alpha=alpha)
        resid_s, idx_s, w_s, rank_s, cnt = _route_shared(xs2, *shared, cnt_p, alpha=alpha)
        tiles = _plan_tiles(cnt[:, :, 0].astype(jnp.int32), idx_p, w_p, rank_p, idx_s, w_s, rank_s)
        xp2, xs2 = _moe_routed(*tiles, exp_w_gate[layer], exp_w_up[layer], exp_w_down[layer],
                               ln2_g[layer], ln2_b[layer], xp2, xs2, resid_p, resid_s)
        xp = xp2.reshape(bp, sp, d_model)
        xs = xs2.reshape(bs, ss, d_model)

    return (xp, xs, jnp.stack(lru_conv_p), jnp.stack(lru_h_p), jnp.stack(ccm_conv_p),
            jnp.stack(lru_conv_s), jnp.stack(lru_h_s), jnp.stack(ccm_conv_s))
```

```python
import functools

import jax
import jax.numpy as jnp
from jax import lax
from jax.experimental import pallas as pl
from jax.experimental.pallas import tpu as pltpu

LANES = 128
SUBLANES = 8
LANE_BITS = LANES.bit_length() - 1
VMEM_HEADROOM_BYTES = 8 << 20
VMEM_LIMIT_CAP_BYTES = 56 << 20
CONV_STATE_BLOCK_BYTES = 4 << 20

LN_EPS = 1e-5
LRU_C = 8.0
N_GROUPS = 8
TOPK_GROUPS = 4
TOP_K = 8
ROUTED_SCALE = 2.5

MIXER_ROWS = 512
ROUTE_ROWS = 512
MOE_TILE = 256
MOE_CHUNKS = 4
SCATTER_UNROLL = 16
GATHER_UNROLL = 16
ACC_VREGS = 32
SCAN_VREGS = 16

F32 = jnp.float32
BF16 = jnp.bfloat16


def _layer_norm(v, g, b):
    mu = jnp.mean(v, axis=-1, keepdims=True)
    d = v - mu
    var = jnp.mean(d * d, axis=-1, keepdims=True)
    return d * lax.rsqrt(var + LN_EPS) * g + b


def _round_up(n, m):
    return (n + m - 1) // m * m


def _lru_body(x_ref, cs_ref, h0_ref, win_ref, bin_ref, cw_ref, cb_ref, wa_ref, ba_ref, wx_ref, bx_ref,
              lam_ref, wout_ref, bout_ref, g1_ref, b1_ref,
              x1_ref, ncs_ref, hl_ref,
              ubuf, a_s, b_s, h_s, *, seq_start, alpha, scan_panels):
    nb, tc, d_model = x_ref.shape
    d_rnn = h0_ref.shape[1]
    heads = wa_ref.shape[0]
    hb = d_rnn // heads
    kw = cw_ref.shape[0]
    hist0 = SUBLANES - (kw - 1)
    rows = nb * tc
    ti = pl.program_id(1)

    @pl.when(ti == 0)
    def _init_state():
        ubuf[:, hist0:SUBLANES, :] = cs_ref[...]
        h_s[...] = h0_ref[...]

    x = x_ref[...].reshape(rows, d_model)
    proj = jnp.dot(x.astype(BF16), win_ref[...], preferred_element_type=F32) + bin_ref[...]
    gate = jax.nn.gelu(proj[:, :d_rnn])
    u = proj[:, d_rnn:]
    ubuf[:, SUBLANES:SUBLANES + tc, :] = u.reshape(nb, tc, d_rnn)
    xc3 = jnp.broadcast_to(cb_ref[...].reshape(1, 1, d_rnn), (nb, tc, d_rnn))
    for j in range(kw):
        xc3 = xc3 + cw_ref[j:j + 1, :].reshape(1, 1, d_rnn) * ubuf[:, hist0 + j:hist0 + j + tc, :]
    tail = ubuf[:, tc + hist0:tc + SUBLANES, :]
    ncs_ref[...] = tail
    ubuf[:, hist0:SUBLANES, :] = tail
    xc = xc3.reshape(rows, d_rnn)

    xcb = xc.astype(BF16)
    ra = jnp.concatenate([jnp.dot(xcb[:, h * hb:(h + 1) * hb], wa_ref[h], preferred_element_type=F32)
                          for h in range(heads)], axis=1)
    ia = jnp.concatenate([jnp.dot(xcb[:, h * hb:(h + 1) * hb], wx_ref[h], preferred_element_type=F32)
                          for h in range(heads)], axis=1)
    r = jax.nn.sigmoid(ra + ba_ref[...])
    gi = jax.nn.sigmoid(ia + bx_ref[...])
    lam = lam_ref[...]
    softplus_neg_lam = jnp.maximum(-lam, 0.0) + jnp.log1p(jnp.exp(-jnp.abs(lam)))
    log_a = (-LRU_C * softplus_neg_lam) * r
    a = jnp.exp(log_a)
    mult = jnp.sqrt(-jnp.tanh(log_a) * (a * a + 1.0))
    if seq_start:
        t_in_seq = lax.broadcasted_iota(jnp.int32, (rows, 1), 0) % tc
        mult = jnp.where((t_in_seq == 0) & (ti == 0), 1.0, mult)
    bterm = xc * gi * mult
    npan = d_rnn // LANES
    for p in range(npan):
        a_s[p] = a[:, p * LANES:(p + 1) * LANES]
        b_s[p] = bterm[:, p * LANES:(p + 1) * LANES]

    for p0 in range(0, npan, scan_panels):
        group = range(p0, min(p0 + scan_panels, npan))
        hs = [h_s[:, p * LANES:(p + 1) * LANES] for p in group]
        for t in range(tc):
            rs = pl.ds(t, nb, stride=tc)
            for k, p in enumerate(group):
                hs[k] = a_s[p, rs, :] * hs[k] + b_s[p, rs, :]
                b_s[p, rs, :] = hs[k]
        for k, p in enumerate(group):
            h_s[:, p * LANES:(p + 1) * LANES] = hs[k]
    hl_ref[...] = h_s[...]

    h_all = jnp.concatenate([b_s[p] for p in range(npan)], axis=1)
    y = jnp.dot((h_all * gate).astype(BF16), wout_ref[...], preferred_element_type=F32) + bout_ref[...]
    x1 = _layer_norm(alpha * x + y, g1_ref[...], b1_ref[...])
    x1_ref[...] = x1.reshape(nb, tc, d_model)


def _lru_mixer(x, conv_state, h0, win, b_in, conv_w, conv_b, wa, ba, wx, bx, lam, wout, bout, g1, b1, *,
               seq_start, alpha):
    n, s, d_model = x.shape
    d_rnn = h0.shape[1]
    kw = conv_w.shape[0]
    tc = min(s, MIXER_ROWS // SUBLANES)
    nb = min(n, MIXER_ROWS // tc)
    rows = nb * tc
    scan_panels = max(1, SCAN_VREGS * SUBLANES // nb)
    row = lambda v: v.reshape(1, -1)
    const2 = lambda i, t: (0, 0)
    const3 = lambda i, t: (0, 0, 0)
    vmem = (2 * 2 * rows * d_model * 4 + 2 * (win.size + wout.size + wa.size + wx.size) * 2
            + (nb * (tc + SUBLANES) + 2 * rows + nb) * d_rnn * 4 + 4 * rows * 2 * d_rnn * 4 + VMEM_HEADROOM_BYTES)
    return pl.pallas_call(
        functools.partial(_lru_body, seq_start=seq_start, alpha=alpha, scan_panels=scan_panels),
        grid=(n // nb, s // tc),
        in_specs=[
            pl.BlockSpec((nb, tc, d_model), lambda i, t: (i, t, 0)),
            pl.BlockSpec((nb, kw - 1, d_rnn), lambda i, t: (i, 0, 0)),
            pl.BlockSpec((nb, d_rnn), lambda i, t: (i, 0)),
            pl.BlockSpec(win.shape, const2), pl.BlockSpec((1, 2 * d_rnn), const2),
            pl.BlockSpec(conv_w.shape, const2), pl.BlockSpec((1, d_rnn), const2),
            pl.BlockSpec(wa.shape, const3), pl.BlockSpec((1, d_rnn), const2),
            pl.BlockSpec(wx.shape, const3), pl.BlockSpec((1, d_rnn), const2),
            pl.BlockSpec((1, d_rnn), const2),
            pl.BlockSpec(wout.shape, const2), pl.BlockSpec((1, d_model), const2),
            pl.BlockSpec((1, d_model), const2), pl.BlockSpec((1, d_model), const2),
        ],
        out_specs=[
            pl.BlockSpec((nb, tc, d_model), lambda i, t: (i, t, 0)),
            pl.BlockSpec((nb, kw - 1, d_rnn), lambda i, t: (i, 0, 0)),
            pl.BlockSpec((nb, d_rnn), lambda i, t: (i, 0)),
        ],
        out_shape=[jax.ShapeDtypeStruct(x.shape, F32),
                   jax.ShapeDtypeStruct((n, kw - 1, d_rnn), F32),
                   jax.ShapeDtypeStruct((n, d_rnn), F32)],
        scratch_shapes=[pltpu.VMEM((nb, tc + SUBLANES, d_rnn), F32),
                        pltpu.VMEM((d_rnn // LANES, rows, LANES), F32),
                        pltpu.VMEM((d_rnn // LANES, rows, LANES), F32),
                        pltpu.VMEM((nb, d_rnn), F32)],
        compiler_params=pltpu.CompilerParams(dimension_semantics=("arbitrary", "arbitrary"),
                                             vmem_limit_bytes=min(vmem, VMEM_LIMIT_CAP_BYTES)),
        name="lru_mixer",
    )(x, conv_state, h0, win, row(b_in), conv_w, row(conv_b), wa, row(ba), wx, row(bx), row(lam),
      wout, row(bout), row(g1), row(b1))


def _ccm_body(x_ref, cs_ref, win_ref, bin_ref, dw_ref, dwb_ref, lng_ref, lnb_ref, wout_ref, bout_ref,
              g1_ref, b1_ref, x1_ref, ncs_ref, gbuf, cbuf, *, alpha, conv_cols):
    nb, tc, d_model = x_ref.shape
    kw, d_conv = dw_ref.shape
    hist_rows = _round_up(kw - 1, SUBLANES)
    hist0 = hist_rows - (kw - 1)
    rows = nb * tc
    ti = pl.program_id(1)

    @pl.when(ti == 0)
    def _init_state():
        gbuf[:, hist0:hist_rows, :] = cs_ref[...]

    x = x_ref[...].reshape(rows, d_model)
    p = jnp.dot(x.astype(BF16), win_ref[...], preferred_element_type=F32) + bin_ref[...]
    glu = p[:, :d_conv] * jax.nn.sigmoid(p[:, d_conv:])
    gbuf[:, hist_rows:hist_rows + tc, :] = glu.reshape(nb, tc, d_conv)

    def conv_seq(n, carry):
        for c0 in range(0, d_conv, conv_cols):
            cols = slice(c0, c0 + conv_cols)
            acc = jnp.broadcast_to(dwb_ref[:, cols], (tc, conv_cols))
            for j in range(kw):
                acc = acc + dw_ref[j:j + 1, cols] * gbuf[n, hist0 + j:hist0 + j + tc, cols]
            cbuf[n, :, cols] = acc
        return carry
    lax.fori_loop(0, nb, conv_seq, 0)

    tail = gbuf[:, tc + hist0:tc + hist_rows, :]
    ncs_ref[...] = tail
    gbuf[:, hist0:hist_rows, :] = tail

    c = cbuf[...].reshape(rows, d_conv)
    hn = _layer_norm(c, lng_ref[...], lnb_ref[...])
    hdn = hn * jax.nn.sigmoid(hn)
    y = jnp.dot(hdn.astype(BF16), wout_ref[...], preferred_element_type=F32) + bout_ref[...]
    x1 = _layer_norm(alpha * x + y, g1_ref[...], b1_ref[...])
    x1_ref[...] = x1.reshape(nb, tc, d_model)


def _ccm_mixer(x, conv_state, win, b_in, dw_w, dw_b, ln_g, ln_b, wout, bout, g1, b1, *, alpha):
    n, s, d_model = x.shape
    kw, d_conv = dw_w.shape
    tc = min(s, MIXER_ROWS // SUBLANES)
    nb = min(n, MIXER_ROWS // tc)
    while nb > SUBLANES and nb * (kw - 1) * d_conv * 4 > CONV_STATE_BLOCK_BYTES:
        nb //= 2
    rows = nb * tc
    hist_rows = _round_up(kw - 1, SUBLANES)
    conv_cols = min(d_conv, max(LANES, (ACC_VREGS * SUBLANES * LANES // tc) // LANES * LANES))
    while d_conv % conv_cols:
        conv_cols -= LANES
    row = lambda v: v.reshape(1, -1)
    const2 = lambda i, t: (0, 0)
    vmem = (2 * 2 * rows * d_model * 4 + 2 * (win.size + wout.size) * 2
            + (nb * (tc + hist_rows) + rows) * d_conv * 4 + 4 * nb * (kw - 1) * d_conv * 4
            + 4 * rows * 2 * d_conv * 4 + VMEM_HEADROOM_BYTES)
    return pl.pallas_call(
        functools.partial(_ccm_body, alpha=alpha, conv_cols=conv_cols),
        grid=(n // nb, s // tc),
        in_specs=[
            pl.BlockSpec((nb, tc, d_model), lambda i, t: (i, t, 0)),
            pl.BlockSpec((nb, kw - 1, d_conv), lambda i, t: (i, 0, 0)),
            pl.BlockSpec(win.shape, const2), pl.BlockSpec((1, 2 * d_conv), const2),
            pl.BlockSpec(dw_w.shape, const2), pl.BlockSpec((1, d_conv), const2),
            pl.BlockSpec((1, d_conv), const2), pl.BlockSpec((1, d_conv), const2),
            pl.BlockSpec(wout.shape, const2), pl.BlockSpec((1, d_model), const2),
            pl.BlockSpec((1, d_model), const2), pl.BlockSpec((1, d_model), const2),
        ],
        out_specs=[
            pl.BlockSpec((nb, tc, d_model), lambda i, t: (i, t, 0)),
            pl.BlockSpec((nb, kw - 1, d_conv), lambda i, t: (i, 0, 0)),
        ],
        out_shape=[jax.ShapeDtypeStruct(x.shape, F32),
                   jax.ShapeDtypeStruct((n, kw - 1, d_conv), F32)],
        scratch_shapes=[pltpu.VMEM((nb, tc + hist_rows, d_conv), F32),
                        pltpu.VMEM((nb, tc, d_conv), F32)],
        compiler_params=pltpu.CompilerParams(dimension_semantics=("arbitrary", "arbitrary"),
                                             vmem_limit_bytes=min(vmem, VMEM_LIMIT_CAP_BYTES)),
        name="ccm_mixer",
    )(x, conv_state, win, row(b_in), dw_w, row(dw_b), row(ln_g), row(ln_b), wout, row(bout), row(g1), row(b1))


def _first_argmax(v, axis):
    m = jnp.max(v, axis=axis, keepdims=True)
    iota = lax.broadcasted_iota(jnp.int32, v.shape, axis)
    first = jnp.min(jnp.where(v == m, iota, v.shape[axis]), axis=axis, keepdims=True)
    return m, iota == first, first


def _route_body(x_ref, rwt_ref, rb_ref, sg_ref, su_ref, sd_ref, cnt_in_ref,
                resid_ref, idx_ref, w_ref, rank_ref, cnt_ref, tri_s, run_s, *, alpha, steps_per_chunk):
    x = x_ref[...]
    n_exp = rwt_ref.shape[0]
    tr = x.shape[0]
    step = pl.program_id(0)

    @pl.when(step == 0)
    def _build_prefix_matrix():
        r = lax.broadcasted_iota(jnp.int32, (tr, tr), 0)
        c = lax.broadcasted_iota(jnp.int32, (tr, tr), 1)
        tri_s[...] = jnp.where(r <= c, 1.0, 0.0).astype(BF16)

    @pl.when(step % steps_per_chunk == 0)
    def _start_chunk():
        run_s[...] = cnt_in_ref[0]

    logits = lax.dot_general(rwt_ref[...], x, (((1,), (1,)), ((), ())),
                             precision=lax.Precision.HIGHEST, preferred_element_type=F32)
    scores = jax.nn.sigmoid(logits)
    biased = scores + rb_ref[...]
    per_group = n_exp // N_GROUPS
    grp = biased.reshape(N_GROUPS, per_group, tr)
    m1, hit1, _ = _first_argmax(grp, 1)
    m2 = jnp.max(jnp.where(hit1, -jnp.inf, grp), axis=1, keepdims=True)
    gscore = (m1 + m2).reshape(N_GROUPS, tr)
    gsel = jnp.zeros((N_GROUPS, tr), F32)
    for _ in range(TOPK_GROUPS):
        _, hit, _ = _first_argmax(gscore, 0)
        gsel = jnp.where(hit, 1.0, gsel)
        gscore = jnp.where(hit, -jnp.inf, gscore)
    emask = jnp.broadcast_to(gsel.reshape(N_GROUPS, 1, tr), (N_GROUPS, per_group, tr)).reshape(n_exp, tr)
    masked = jnp.where(emask > 0.0, biased, -jnp.inf)
    idx_rows, w_rows, hits = [], [], []
    for _ in range(TOP_K):
        _, hit, first = _first_argmax(masked, 0)
        idx_rows.append(first)
        hits.append(hit)
        w_rows.append(jnp.sum(jnp.where(hit, scores, 0.0), axis=0, keepdims=True))
        masked = jnp.where(hit, -jnp.inf, masked)
    w = jnp.concatenate(w_rows, axis=0)
    idx_ref[...] = jnp.concatenate(idx_rows, axis=0)
    w_ref[...] = w / jnp.sum(w, axis=0, keepdims=True) * ROUTED_SCALE

    sel = jnp.zeros((n_exp, tr), F32)
    for hit in hits:
        sel = jnp.where(hit, 1.0, sel)
    before = jnp.dot(sel.astype(BF16), tri_s[...], preferred_element_type=F32) - sel + run_s[:, 0:1]
    rank_ref[...] = jnp.concatenate(
        [jnp.sum(jnp.where(hit, before, 0.0), axis=0, keepdims=True) for hit in hits], axis=0).astype(jnp.int32)
    run_s[...] = run_s[...] + jnp.sum(sel, axis=1, keepdims=True)
    cnt_ref[0] = run_s[...]

    xb = x.astype(BF16)
    sgate = jnp.dot(xb, sg_ref[...], preferred_element_type=F32)
    sup = jnp.dot(xb, su_ref[...], preferred_element_type=F32)
    hs = (sgate * jax.nn.sigmoid(sgate) * sup).astype(BF16)
    resid_ref[...] = alpha * x + jnp.dot(hs, sd_ref[...], preferred_element_type=F32)


def _route_shared(x, router_wt, router_bias, sg, su, sd, cnt_in, *, alpha):
    t, d_model = x.shape
    n_exp = router_wt.shape[0]
    rows_per_chunk = t // MOE_CHUNKS
    tr = min(rows_per_chunk, ROUTE_ROWS)
    steps_per_chunk = rows_per_chunk // tr
    const2 = lambda i: (0, 0)
    chunk_map = lambda i: (i // steps_per_chunk, 0, 0)
    vmem = 2 * 2 * tr * d_model * 4 + 2 * (router_wt.size * 4 + 3 * sg.size * 2) + VMEM_HEADROOM_BYTES
    return pl.pallas_call(
        functools.partial(_route_body, alpha=alpha, steps_per_chunk=steps_per_chunk),
        grid=(t // tr,),
        in_specs=[pl.BlockSpec((tr, d_model), lambda i: (i, 0)),
                  pl.BlockSpec(router_wt.shape, const2), pl.BlockSpec((n_exp, 1), const2),
                  pl.BlockSpec(sg.shape, const2), pl.BlockSpec(su.shape, const2), pl.BlockSpec(sd.shape, const2),
                  pl.BlockSpec((1, n_exp, LANES), chunk_map)],
        out_specs=[pl.BlockSpec((tr, d_model), lambda i: (i, 0)),
                   pl.BlockSpec((TOP_K, tr), lambda i: (0, i)),
                   pl.BlockSpec((TOP_K, tr), lambda i: (0, i)),
                   pl.BlockSpec((TOP_K, tr), lambda i: (0, i)),
                   pl.BlockSpec((1, n_exp, LANES), chunk_map)],
        out_shape=[jax.ShapeDtypeStruct((t, d_model), F32),
                   jax.ShapeDtypeStruct((TOP_K, t), jnp.int32),
                   jax.ShapeDtypeStruct((TOP_K, t), F32),
                   jax.ShapeDtypeStruct((TOP_K, t), jnp.int32),
                   jax.ShapeDtypeStruct((MOE_CHUNKS, n_exp, LANES), F32)],
        scratch_shapes=[pltpu.VMEM((tr, tr), BF16), pltpu.VMEM((n_exp, LANES), F32)],
        compiler_params=pltpu.CompilerParams(dimension_semantics=("arbitrary",),
                                             vmem_limit_bytes=min(vmem, VMEM_LIMIT_CAP_BYTES)),
        name="route_shared",
    )(x, router_wt, router_bias.reshape(n_exp, 1), sg, su, sd, cnt_in)


def _moe_body(tile_e_ref, tile_c_ref, tile_f_ref, seg_cnt_ref, seg_base_ref,
              wg_ref, wu_ref, wd_ref, g2_ref, b2_ref,
              dest_hbm, wsel_hbm, xp_hbm, xs_hbm, rp_hbm, rs_hbm,
              op_hbm, os_hbm,
              x_s, y_s, xa_s, xb_s, oa_s, ob_s, wgu_s, wd_s, tok_sm, wts_sm,
              dest_a_sm, dest_b_sm, wsel_a_sm, wsel_b_sm, sem_in, sem_out, sem_sm,
              *, tm, cp_rows, cs_rows, npan, tiles_per_chunk, top_k):
    chunk = cp_rows + cs_rows
    chunk_stride = chunk + SUBLANES
    half = tm // 2
    half_stride = half + SUBLANES
    dest_sm = (dest_a_sm, dest_b_sm)
    wsel_sm = (wsel_a_sm, wsel_b_sm)
    d_ff = wg_ref.shape[3]
    n_exp = seg_cnt_ref.shape[0] // MOE_CHUNKS
    j = pl.program_id(0)
    c = tile_c_ref[j]
    list0 = (j - c * tiles_per_chunk) * tm
    flags = tile_f_ref[j]
    seg_first = (flags & 2) != 0
    chunk_first = (flags & 4) != 0
    chunk_last = (flags & 8) != 0
    both_halves = (flags & 17) == 1
    first_half_only = (flags & 17) == 17

    def chunk_copies(hbm_p, hbm_s, vmem, sem, to_vmem):
        copies = []
        for p in range(npan):
            for hbm, nrows, row0 in ((hbm_p, cp_rows, 0), (hbm_s, cs_rows, cp_rows)):
                h = hbm.at[pl.ds(c * nrows, nrows), pl.ds(p * LANES, LANES)]
                v = vmem.at[pl.ds(p * chunk_stride + row0, nrows), :]
                copies.append(pltpu.make_async_copy(h, v, sem.at[0]) if to_vmem
                              else pltpu.make_async_copy(v, h, sem.at[0]))
        return copies

    @pl.when(chunk_first)
    def _load_chunk():
        copies = (chunk_copies(xp_hbm, xs_hbm, x_s, sem_in, True)
                  + chunk_copies(rp_hbm, rs_hbm, y_s, sem_in, True))
        for cp in copies:
            cp.start()
        zeros = jnp.zeros((SUBLANES, LANES), F32)
        for p in range(npan):
            x_s[pl.ds(p * chunk_stride + chunk, SUBLANES), :] = zeros
            y_s[pl.ds(p * chunk_stride + chunk, SUBLANES), :] = zeros
        def stage(k):
            first = (c * top_k + k) * chunk
            return (pltpu.make_async_copy(dest_hbm.at[pl.ds(first, chunk)], dest_sm[k % 2], sem_sm.at[k % 2]),
                    pltpu.make_async_copy(wsel_hbm.at[pl.ds(first, chunk)], wsel_sm[k % 2], sem_sm.at[2 + k % 2]))
        staged = stage(0)
        for cp in staged:
            cp.start()
        for k in range(top_k):
            for cp in staged:
                cp.wait()
            if k + 1 < top_k:
                staged = stage(k + 1)
                for cp in staged:
                    cp.start()

            def fill(r, carry):
                for lane in range(LANES):
                    t = r * LANES + lane
                    d = dest_sm[k % 2][t]
                    tok_sm[d] = t
                    wts_sm[d] = wsel_sm[k % 2][t]
                return carry
            lax.fori_loop(0, chunk // LANES, fill, 0)

        def pad_segment(e, carry):
            n = seg_cnt_ref[c * n_exp + e]
            base = seg_base_ref[c * n_exp + e]

            def pad(i, carry2):
                tok_sm[i] = chunk
                wts_sm[i] = 0.0
                return carry2
            lax.fori_loop(base + n, base + (n + tm - 1) // tm * tm, pad, 0)
            return carry
        lax.fori_loop(0, n_exp, pad_segment, 0)

        for cp in copies:
            cp.wait()

    @pl.when(seg_first)
    def _cast_weights():
        wgu_s[:, :d_ff] = wg_ref[0, 0].astype(BF16)
        wgu_s[:, d_ff:] = wu_ref[0, 0].astype(BF16)
        wd_s[...] = wd_ref[0, 0].astype(BF16)

    def gather_half(first, buf):
        for i in range(half):
            t = tok_sm[list0 + first + i]
            buf[pl.ds(i, npan, stride=half_stride), :] = x_s[pl.ds(t, npan, stride=chunk_stride), :]

    def ffn_half(src, dst):
        lhs = jnp.concatenate([src[pl.ds(p * half_stride, half), :] for p in range(npan)], axis=1).astype(BF16)
        gu = jnp.dot(lhs, wgu_s[...], preferred_element_type=F32)
        g = gu[:, :d_ff]
        hidden = (g * jax.nn.sigmoid(g) * gu[:, d_ff:]).astype(BF16)
        o = jnp.dot(hidden, wd_s[...], preferred_element_type=F32)
        for p in range(npan):
            dst[pl.ds(p * half_stride, half), :] = o[:, p * LANES:(p + 1) * LANES]

    def combine_half(first, buf):
        for i0 in range(0, half, SCATTER_UNROLL):
            toks, rows_new = [], []
            for i in range(i0, i0 + SCATTER_UNROLL):
                t = tok_sm[list0 + first + i]
                w = wts_sm[list0 + first + i]
                row = buf[pl.ds(i, npan, stride=half_stride), :]
                toks.append(t)
                rows_new.append(y_s[pl.ds(t, npan, stride=chunk_stride), :] + w * row)
            for t, new in zip(toks, rows_new):
                y_s[pl.ds(t, npan, stride=chunk_stride), :] = new

    @pl.when(both_halves)
    def _tile():
        gather_half(0, xa_s)
        gather_half(half, xb_s)
        ffn_half(xa_s, oa_s)
        ffn_half(xb_s, ob_s)
        combine_half(0, oa_s)
        combine_half(half, ob_s)

    @pl.when(first_half_only)
    def _half_tile():
        gather_half(0, xa_s)
        ffn_half(xa_s, oa_s)
        combine_half(0, oa_s)

    @pl.when(chunk_last)
    def _finish_chunk():
        blk = MOE_TILE
        g2 = g2_ref[...]
        b2 = b2_ref[...]

        def ln_block(rb, carry):
            r0 = pl.multiple_of(rb * blk, blk)
            v = jnp.concatenate([y_s[pl.ds(p * chunk_stride + r0, blk), :] for p in range(npan)], axis=1)
            out = _layer_norm(v, g2, b2)
            for p in range(npan):
                y_s[pl.ds(p * chunk_stride + r0, blk), :] = out[:, p * LANES:(p + 1) * LANES]
            return carry
        lax.fori_loop(0, chunk // blk, ln_block, 0)
        copies = chunk_copies(op_hbm, os_hbm, y_s, sem_out, False)
        for cp in copies:
            cp.start()
        for cp in copies:
            cp.wait()


def _moe_routed(tile_e, tile_c, tile_f, seg_cnt, seg_base, dest, wsel, wg, wu, wd, g2, b2, xp, xs, rp, rs, *,
                layer, top_k):
    n_tiles = tile_e.shape[0]
    _, n_exp, d_model, d_ff = wg.shape
    tm = MOE_TILE
    tiles_per_chunk = n_tiles // MOE_CHUNKS
    npan = d_model // LANES
    assert npan == SUBLANES, "one token row must fill exactly one (8, 128) register"
    cp_rows, cs_rows = xp.shape[0] // MOE_CHUNKS, xs.shape[0] // MOE_CHUNKS
    chunk = cp_rows + cs_rows
    assert chunk % MOE_TILE == 0 and cs_rows % SUBLANES == 0
    chunk_stride = chunk + SUBLANES
    half_stride = tm // 2 + SUBLANES
    any_spec = pl.BlockSpec(memory_space=pl.ANY)
    expert_map = lambda j, te, *_: (layer, te[j], 0, 0)
    const2 = lambda j, *_: (0, 0)
    vmem = (2 * npan * chunk_stride * LANES * 4 + 4 * npan * half_stride * LANES * 4 + 3 * d_model * d_ff * 2
            + 2 * 3 * d_model * d_ff * 4 + VMEM_HEADROOM_BYTES)
    return pl.pallas_call(
        functools.partial(_moe_body, tm=tm, cp_rows=cp_rows, cs_rows=cs_rows, npan=npan,
                          tiles_per_chunk=tiles_per_chunk, top_k=top_k),
        grid_spec=pltpu.PrefetchScalarGridSpec(
            num_scalar_prefetch=5,
            grid=(n_tiles,),
            in_specs=[
                pl.BlockSpec((1, 1, d_model, d_ff), expert_map),
                pl.BlockSpec((1, 1, d_model, d_ff), expert_map),
                pl.BlockSpec((1, 1, d_ff, d_model), expert_map),
                pl.BlockSpec((1, d_model), const2),
                pl.BlockSpec((1, d_model), const2),
                any_spec, any_spec, any_spec, any_spec, any_spec, any_spec,
            ],
            out_specs=[any_spec, any_spec],
            scratch_shapes=[
                pltpu.VMEM((npan * chunk_stride, LANES), F32),
                pltpu.VMEM((npan * chunk_stride, LANES), F32),
                pltpu.VMEM((npan * half_stride, LANES), F32),
                pltpu.VMEM((npan * half_stride, LANES), F32),
                pltpu.VMEM((npan * half_stride, LANES), F32),
                pltpu.VMEM((npan * half_stride, LANES), F32),
                pltpu.VMEM((d_model, 2 * d_ff), BF16),
                pltpu.VMEM((d_ff, d_model), BF16),
                pltpu.SMEM((tiles_per_chunk * tm,), jnp.int32),
                pltpu.SMEM((tiles_per_chunk * tm,), F32),
                pltpu.SMEM((chunk,), jnp.int32),
                pltpu.SMEM((chunk,), jnp.int32),
                pltpu.SMEM((chunk,), F32),
                pltpu.SMEM((chunk,), F32),
                pltpu.SemaphoreType.DMA((1,)),
                pltpu.SemaphoreType.DMA((1,)),
                pltpu.SemaphoreType.DMA((4,)),
            ]),
        out_shape=[jax.ShapeDtypeStruct(xp.shape, F32), jax.ShapeDtypeStruct(xs.shape, F32)],
        compiler_params=pltpu.CompilerParams(dimension_semantics=("arbitrary",),
                                             vmem_limit_bytes=min(vmem, VMEM_LIMIT_CAP_BYTES)),
        name="moe_routed",
    )(tile_e, tile_c, tile_f, seg_cnt, seg_base, wg, wu, wd,
      g2.reshape(1, d_model), b2.reshape(1, d_model), dest, wsel, xp, xs, rp, rs)


def _plan_tiles(cnt, idx_p, w_p, rank_p, idx_s, w_s, rank_s):
    k, tp = idx_p.shape
    ts = idx_s.shape[1]
    n_chunks, n_exp = cnt.shape
    tm = MOE_TILE
    cp, cs = tp // n_chunks, ts // n_chunks
    chunk = cp + cs
    tiles_per_chunk = (chunk * k) // tm + n_exp
    experts = jnp.arange(n_exp, dtype=jnp.int32)

    seg_tiles = (cnt + tm - 1) // tm
    seg_tile_end = jnp.sum(jnp.where(experts[None, :] <= experts[:, None], seg_tiles[:, None, :], 0), axis=-1)
    seg_tile_base = seg_tile_end - seg_tiles
    seg_base = seg_tile_base * tm
    n_active = seg_tile_end[:, -1]

    def rows_of(idx, rank, per_chunk):
        idx3 = idx.reshape(k, n_chunks, per_chunk)
        base = jnp.sum(jnp.where(idx3[..., None] == experts, seg_base[None, :, None, :], 0), axis=-1)
        return jnp.transpose(base + rank.reshape(k, n_chunks, per_chunk), (1, 0, 2))
    by_chunk = lambda a, per_chunk: jnp.transpose(a.reshape(k, n_chunks, per_chunk), (1, 0, 2))
    dest = jnp.concatenate([rows_of(idx_p, rank_p, cp), rows_of(idx_s, rank_s, cs)], axis=2).reshape(-1)
    wsel = jnp.concatenate([by_chunk(w_p, cp), by_chunk(w_s, cs)], axis=2).reshape(-1)

    jl = jnp.arange(tiles_per_chunk, dtype=jnp.int32)[None, :]
    active = jl < n_active[:, None]
    jc = jnp.minimum(jl, n_active[:, None] - 1)
    tile_e = jnp.sum((seg_tile_end[:, None, :] <= jc[:, :, None]).astype(jnp.int32), axis=-1)
    of_tile = lambda per_seg: jnp.sum(jnp.where(tile_e[..., None] == experts, per_seg[:, None, :], 0), axis=-1)
    first_tile = of_tile(seg_tile_base)
    rows_left = of_tile(cnt) - (jl - first_tile) * tm
    flags = (active.astype(jnp.int32) + 2 * (active & (jl == first_tile)).astype(jnp.int32)
             + 4 * (jl == 0).astype(jnp.int32) + 8 * (jl == n_active[:, None] - 1).astype(jnp.int32)
             + 16 * (rows_left <= tm // 2).astype(jnp.int32))
    tile_c = jnp.broadcast_to(jnp.arange(n_chunks, dtype=jnp.int32)[:, None], tile_e.shape)
    flat = lambda a: a.reshape(-1)
    return flat(tile_e), flat(tile_c), flat(flags), flat(cnt), flat(seg_base), dest, wsel


def kernel(x_prompt, x_sample, state_lru_conv, state_lru_h, state_ccm_conv, lru_w_in, lru_b_in, lru_conv_w, lru_conv_b, lru_w_a, lru_b_a, lru_w_x, lru_b_x, lru_lambda, lru_w_out, lru_b_out, ccm_w_in, ccm_b_in, ccm_dw_w, ccm_dw_b, ccm_ln_g, ccm_ln_b, ccm_w_out, ccm_b_out, ln1_g, ln1_b, ln2_g, ln2_b, router_w, router_bias, exp_w_gate, exp_w_up, exp_w_down, sh_w_gate, sh_w_up, sh_w_down):
    depth = ln1_g.shape[0]
    alpha = (2 * depth) ** 0.25
    n_exp = router_w.shape[2]
    bp, sp, d_model = x_prompt.shape
    bs, ss, _ = x_sample.shape
    kw_lru = lru_conv_w.shape[1]
    kw_ccm = ccm_dw_w.shape[1]
    d_rnn = lru_conv_w.shape[2]
    d_conv = ccm_dw_w.shape[2]
    bf = lambda a: a.astype(BF16)

    zero_lru_conv = jnp.zeros((bp, kw_lru - 1, d_rnn), F32)
    zero_lru_h = jnp.zeros((bp, d_rnn), F32)
    zero_ccm_conv = jnp.zeros((bp, kw_ccm - 1, d_conv), F32)

    xp, xs = x_prompt, x_sample
    lru_conv_p, lru_h_p, ccm_conv_p = [], [], []
    lru_conv_s, lru_h_s, ccm_conv_s = [], [], []
    for layer in range(depth):
        j = layer // 2
        if layer % 2 == 0:
            weights = (bf(lru_w_in[j]), lru_b_in[j], lru_conv_w[j], lru_conv_b[j], bf(lru_w_a[j]), lru_b_a[j],
                       bf(lru_w_x[j]), lru_b_x[j], lru_lambda[j], bf(lru_w_out[j]), lru_b_out[j],
                       ln1_g[layer], ln1_b[layer])
            xp, cb, hl = _lru_mixer(xp, zero_lru_conv, zero_lru_h, *weights, seq_start=True, alpha=alpha)
            lru_conv_p.append(cb)
            lru_h_p.append(hl)
            xs, cb, hl = _lru_mixer(xs, state_lru_conv[j], state_lru_h[j], *weights, seq_start=False, alpha=alpha)
            lru_conv_s.append(cb)
            lru_h_s.append(hl)
        else:
            weights = (bf(ccm_w_in[j]), ccm_b_in[j], ccm_dw_w[j], ccm_dw_b[j], ccm_ln_g[j], ccm_ln_b[j],
                       bf(ccm_w_out[j]), ccm_b_out[j], ln1_g[layer], ln1_b[layer])
            xp, cb = _ccm_mixer(xp, zero_ccm_conv, *weights, alpha=alpha)
            ccm_conv_p.append(cb)
            xs, cb = _ccm_mixer(xs, state_ccm_conv[j], *weights, alpha=alpha)
            ccm_conv_s.append(cb)

        shared = (router_w[layer].T, router_bias[layer], bf(sh_w_gate[layer]), bf(sh_w_up[layer]),
                  bf(sh_w_down[layer]))
        xp2 = xp.reshape(bp * sp, d_model)
        xs2 = xs.reshape(bs * ss, d_model)
        no_tokens = jnp.zeros((MOE_CHUNKS, n_exp, LANES), F32)
        resid_p, idx_p, w_p, rank_p, cnt_p = _route_shared(xp2, *shared, no_tokens, alpha=alpha)
        resid_s, idx_s, w_s, rank_s, cnt = _route_shared(xs2, *shared, cnt_p, alpha=alpha)
        tiles = _plan_tiles(cnt[:, :, 0].astype(jnp.int32), idx_p, w_p, rank_p, idx_s, w_s, rank_s)
        xp2, xs2 = _moe_routed(*tiles, exp_w_gate, exp_w_up, exp_w_down, ln2_g[layer], ln2_b[layer],
                               xp2, xs2, resid_p, resid_s, layer=layer, top_k=idx_p.shape[0])
        xp = xp2.reshape(bp, sp, d_model)
        xs = xs2.reshape(bs, ss, d_model)

    return (xp, xs, jnp.stack(lru_conv_p), jnp.stack(lru_h_p), jnp.stack(ccm_conv_p),
            jnp.stack(lru_conv_s), jnp.stack(lru_h_s), jnp.stack(ccm_conv_s))
```

```python
import functools

import jax
import jax.numpy as jnp
from jax import lax
from jax.experimental import pallas as pl
from jax.experimental.pallas import tpu as pltpu

LANES = 128
SUBLANES = 8
LANE_BITS = LANES.bit_length() - 1
VMEM_HEADROOM_BYTES = 8 << 20
VMEM_LIMIT_CAP_BYTES = 56 << 20
CONV_STATE_BLOCK_BYTES = 4 << 20

LN_EPS = 1e-5
LRU_C = 8.0
N_GROUPS = 8
TOPK_GROUPS = 4
TOP_K = 8
ROUTED_SCALE = 2.5

MIXER_ROWS = 512
ROUTE_ROWS = 512
MOE_TILE = 256
MOE_CHUNKS = 4
SCATTER_UNROLL = 16
WEIGHT_BUFFERS = 2
ACC_VREGS = 32
SCAN_VREGS = 16

F32 = jnp.float32
BF16 = jnp.bfloat16


def _layer_norm(v, g, b):
    mu = jnp.mean(v, axis=-1, keepdims=True)
    d = v - mu
    var = jnp.mean(d * d, axis=-1, keepdims=True)
    return d * lax.rsqrt(var + LN_EPS) * g + b


def _round_up(n, m):
    return (n + m - 1) // m * m


def _lru_body(x_ref, cs_ref, h0_ref, win_ref, bin_ref, cw_ref, cb_ref, wa_ref, ba_ref, wx_ref, bx_ref,
              lam_ref, wout_ref, bout_ref, g1_ref, b1_ref,
              x1_ref, ncs_ref, hl_ref,
              ubuf, a_s, b_s, h_s, *, seq_start, alpha, scan_panels):
    nb, tc, d_model = x_ref.shape
    d_rnn = h0_ref.shape[1]
    heads = wa_ref.shape[0]
    hb = d_rnn // heads
    kw = cw_ref.shape[0]
    hist0 = SUBLANES - (kw - 1)
    rows = nb * tc
    ti = pl.program_id(1)

    @pl.when(ti == 0)
    def _init_state():
        ubuf[:, hist0:SUBLANES, :] = cs_ref[...]
        h_s[...] = h0_ref[...]

    x = x_ref[...].reshape(rows, d_model)
    proj = jnp.dot(x.astype(BF16), win_ref[...], preferred_element_type=F32) + bin_ref[...]
    gate = jax.nn.gelu(proj[:, :d_rnn])
    u = proj[:, d_rnn:]
    ubuf[:, SUBLANES:SUBLANES + tc, :] = u.reshape(nb, tc, d_rnn)
    xc3 = jnp.broadcast_to(cb_ref[...].reshape(1, 1, d_rnn), (nb, tc, d_rnn))
    for j in range(kw):
        xc3 = xc3 + cw_ref[j:j + 1, :].reshape(1, 1, d_rnn) * ubuf[:, hist0 + j:hist0 + j + tc, :]
    tail = ubuf[:, tc + hist0:tc + SUBLANES, :]
    ncs_ref[...] = tail
    ubuf[:, hist0:SUBLANES, :] = tail
    xc = xc3.reshape(rows, d_rnn)

    xcb = xc.astype(BF16)
    ra = jnp.concatenate([jnp.dot(xcb[:, h * hb:(h + 1) * hb], wa_ref[h], preferred_element_type=F32)
                          for h in range(heads)], axis=1)
    ia = jnp.concatenate([jnp.dot(xcb[:, h * hb:(h + 1) * hb], wx_ref[h], preferred_element_type=F32)
                          for h in range(heads)], axis=1)
    r = jax.nn.sigmoid(ra + ba_ref[...])
    gi = jax.nn.sigmoid(ia + bx_ref[...])
    lam = lam_ref[...]
    softplus_neg_lam = jnp.maximum(-lam, 0.0) + jnp.log1p(jnp.exp(-jnp.abs(lam)))
    log_a = (-LRU_C * softplus_neg_lam) * r
    a = jnp.exp(log_a)
    mult = jnp.sqrt(-jnp.tanh(log_a) * (a * a + 1.0))
    if seq_start:
        t_in_seq = lax.broadcasted_iota(jnp.int32, (rows, 1), 0) % tc
        mult = jnp.where((t_in_seq == 0) & (ti == 0), 1.0, mult)
    bterm = xc * gi * mult
    npan = d_rnn // LANES
    for p in range(npan):
        a_s[p] = a[:, p * LANES:(p + 1) * LANES]
        b_s[p] = bterm[:, p * LANES:(p + 1) * LANES]

    for p0 in range(0, npan, scan_panels):
        group = range(p0, min(p0 + scan_panels, npan))
        hs = [h_s[:, p * LANES:(p + 1) * LANES] for p in group]
        for t in range(tc):
            rs = pl.ds(t, nb, stride=tc)
            for k, p in enumerate(group):
                hs[k] = a_s[p, rs, :] * hs[k] + b_s[p, rs, :]
                b_s[p, rs, :] = hs[k]
        for k, p in enumerate(group):
            h_s[:, p * LANES:(p + 1) * LANES] = hs[k]
    hl_ref[...] = h_s[...]

    h_all = jnp.concatenate([b_s[p] for p in range(npan)], axis=1)
    y = jnp.dot((h_all * gate).astype(BF16), wout_ref[...], preferred_element_type=F32) + bout_ref[...]
    x1 = _layer_norm(alpha * x + y, g1_ref[...], b1_ref[...])
    x1_ref[...] = x1.reshape(nb, tc, d_model)


def _lru_mixer(x, conv_state, h0, win, b_in, conv_w, conv_b, wa, ba, wx, bx, lam, wout, bout, g1, b1, *,
               seq_start, alpha):
    n, s, d_model = x.shape
    d_rnn = h0.shape[1]
    kw = conv_w.shape[0]
    tc = min(s, MIXER_ROWS // SUBLANES)
    nb = min(n, MIXER_ROWS // tc)
    rows = nb * tc
    scan_panels = max(1, SCAN_VREGS * SUBLANES // nb)
    row = lambda v: v.reshape(1, -1)
    const2 = lambda i, t: (0, 0)
    const3 = lambda i, t: (0, 0, 0)
    vmem = (2 * 2 * rows * d_model * 4 + 2 * (win.size + wout.size + wa.size + wx.size) * 2
            + (nb * (tc + SUBLANES) + 2 * rows + nb) * d_rnn * 4 + 4 * rows * 2 * d_rnn * 4 + VMEM_HEADROOM_BYTES)
    return pl.pallas_call(
        functools.partial(_lru_body, seq_start=seq_start, alpha=alpha, scan_panels=scan_panels),
        grid=(n // nb, s // tc),
        in_specs=[
            pl.BlockSpec((nb, tc, d_model), lambda i, t: (i, t, 0)),
            pl.BlockSpec((nb, kw - 1, d_rnn), lambda i, t: (i, 0, 0)),
            pl.BlockSpec((nb, d_rnn), lambda i, t: (i, 0)),
            pl.BlockSpec(win.shape, const2), pl.BlockSpec((1, 2 * d_rnn), const2),
            pl.BlockSpec(conv_w.shape, const2), pl.BlockSpec((1, d_rnn), const2),
            pl.BlockSpec(wa.shape, const3), pl.BlockSpec((1, d_rnn), const2),
            pl.BlockSpec(wx.shape, const3), pl.BlockSpec((1, d_rnn), const2),
            pl.BlockSpec((1, d_rnn), const2),
            pl.BlockSpec(wout.shape, const2), pl.BlockSpec((1, d_model), const2),
            pl.BlockSpec((1, d_model), const2), pl.BlockSpec((1, d_model), const2),
        ],
        out_specs=[
            pl.BlockSpec((nb, tc, d_model), lambda i, t: (i, t, 0)),
            pl.BlockSpec((nb, kw - 1, d_rnn), lambda i, t: (i, 0, 0)),
            pl.BlockSpec((nb, d_rnn), lambda i, t: (i, 0)),
        ],
        out_shape=[jax.ShapeDtypeStruct(x.shape, F32),
                   jax.ShapeDtypeStruct((n, kw - 1, d_rnn), F32),
                   jax.ShapeDtypeStruct((n, d_rnn), F32)],
        scratch_shapes=[pltpu.VMEM((nb, tc + SUBLANES, d_rnn), F32),
                        pltpu.VMEM((d_rnn // LANES, rows, LANES), F32),
                        pltpu.VMEM((d_rnn // LANES, rows, LANES), F32),
                        pltpu.VMEM((nb, d_rnn), F32)],
        compiler_params=pltpu.CompilerParams(dimension_semantics=("arbitrary", "arbitrary"),
                                             vmem_limit_bytes=min(vmem, VMEM_LIMIT_CAP_BYTES)),
        name="lru_mixer",
    )(x, conv_state, h0, win, row(b_in), conv_w, row(conv_b), wa, row(ba), wx, row(bx), row(lam),
      wout, row(bout), row(g1), row(b1))


def _ccm_body(x_ref, cs_ref, win_ref, bin_ref, dw_ref, dwb_ref, lng_ref, lnb_ref, wout_ref, bout_ref,
              g1_ref, b1_ref, x1_ref, ncs_ref, gbuf, cbuf, *, alpha, conv_cols):
    nb, tc, d_model = x_ref.shape
    kw, d_conv = dw_ref.shape
    hist_rows = _round_up(kw - 1, SUBLANES)
    hist0 = hist_rows - (kw - 1)
    rows = nb * tc
    ti = pl.program_id(1)

    @pl.when(ti == 0)
    def _init_state():
        gbuf[:, hist0:hist_rows, :] = cs_ref[...]

    x = x_ref[...].reshape(rows, d_model)
    p = jnp.dot(x.astype(BF16), win_ref[...], preferred_element_type=F32) + bin_ref[...]
    glu = p[:, :d_conv] * jax.nn.sigmoid(p[:, d_conv:])
    gbuf[:, hist_rows:hist_rows + tc, :] = glu.reshape(nb, tc, d_conv)

    def conv_seq(n, carry):
        for c0 in range(0, d_conv, conv_cols):
            cols = slice(c0, c0 + conv_cols)
            acc = jnp.broadcast_to(dwb_ref[:, cols], (tc, conv_cols))
            for j in range(kw):
                acc = acc + dw_ref[j:j + 1, cols] * gbuf[n, hist0 + j:hist0 + j + tc, cols]
            cbuf[n, :, cols] = acc
        return carry
    lax.fori_loop(0, nb, conv_seq, 0)

    tail = gbuf[:, tc + hist0:tc + hist_rows, :]
    ncs_ref[...] = tail
    gbuf[:, hist0:hist_rows, :] = tail

    c = cbuf[...].reshape(rows, d_conv)
    hn = _layer_norm(c, lng_ref[...], lnb_ref[...])
    hdn = hn * jax.nn.sigmoid(hn)
    y = jnp.dot(hdn.astype(BF16), wout_ref[...], preferred_element_type=F32) + bout_ref[...]
    x1 = _layer_norm(alpha * x + y, g1_ref[...], b1_ref[...])
    x1_ref[...] = x1.reshape(nb, tc, d_model)


def _ccm_mixer(x, conv_state, win, b_in, dw_w, dw_b, ln_g, ln_b, wout, bout, g1, b1, *, alpha):
    n, s, d_model = x.shape
    kw, d_conv = dw_w.shape
    tc = min(s, MIXER_ROWS // SUBLANES)
    nb = min(n, MIXER_ROWS // tc)
    while nb > SUBLANES and nb * (kw - 1) * d_conv * 4 > CONV_STATE_BLOCK_BYTES:
        nb //= 2
    rows = nb * tc
    hist_rows = _round_up(kw - 1, SUBLANES)
    conv_cols = min(d_conv, max(LANES, (ACC_VREGS * SUBLANES * LANES // tc) // LANES * LANES))
    while d_conv % conv_cols:
        conv_cols -= LANES
    row = lambda v: v.reshape(1, -1)
    const2 = lambda i, t: (0, 0)
    vmem = (2 * 2 * rows * d_model * 4 + 2 * (win.size + wout.size) * 2
            + (nb * (tc + hist_rows) + rows) * d_conv * 4 + 4 * nb * (kw - 1) * d_conv * 4
            + 4 * rows * 2 * d_conv * 4 + VMEM_HEADROOM_BYTES)
    return pl.pallas_call(
        functools.partial(_ccm_body, alpha=alpha, conv_cols=conv_cols),
        grid=(n // nb, s // tc),
        in_specs=[
            pl.BlockSpec((nb, tc, d_model), lambda i, t: (i, t, 0)),
            pl.BlockSpec((nb, kw - 1, d_conv), lambda i, t: (i, 0, 0)),
            pl.BlockSpec(win.shape, const2), pl.BlockSpec((1, 2 * d_conv), const2),
            pl.BlockSpec(dw_w.shape, const2), pl.BlockSpec((1, d_conv), const2),
            pl.BlockSpec((1, d_conv), const2), pl.BlockSpec((1, d_conv), const2),
            pl.BlockSpec(wout.shape, const2), pl.BlockSpec((1, d_model), const2),
            pl.BlockSpec((1, d_model), const2), pl.BlockSpec((1, d_model), const2),
        ],
        out_specs=[
            pl.BlockSpec((nb, tc, d_model), lambda i, t: (i, t, 0)),
            pl.BlockSpec((nb, kw - 1, d_conv), lambda i, t: (i, 0, 0)),
        ],
        out_shape=[jax.ShapeDtypeStruct(x.shape, F32),
                   jax.ShapeDtypeStruct((n, kw - 1, d_conv), F32)],
        scratch_shapes=[pltpu.VMEM((nb, tc + hist_rows, d_conv), F32),
                        pltpu.VMEM((nb, tc, d_conv), F32)],
        compiler_params=pltpu.CompilerParams(dimension_semantics=("arbitrary", "arbitrary"),
                                             vmem_limit_bytes=min(vmem, VMEM_LIMIT_CAP_BYTES)),
        name="ccm_mixer",
    )(x, conv_state, win, row(b_in), dw_w, row(dw_b), row(ln_g), row(ln_b), wout, row(bout), row(g1), row(b1))


def _first_argmax(v, axis):
    m = jnp.max(v, axis=axis, keepdims=True)
    iota = lax.broadcasted_iota(jnp.int32, v.shape, axis)
    first = jnp.min(jnp.where(v == m, iota, v.shape[axis]), axis=axis, keepdims=True)
    return m, iota == first, first


def _route_body(x_ref, rwt_ref, rb_ref, sg_ref, su_ref, sd_ref, cnt_in_ref,
                resid_ref, idx_ref, w_ref, rank_ref, cnt_ref, tri_s, run_s, *, alpha, steps_per_chunk):
    x = x_ref[...]
    n_exp = rwt_ref.shape[0]
    tr = x.shape[0]
    step = pl.program_id(0)

    @pl.when(step == 0)
    def _build_prefix_matrix():
        r = lax.broadcasted_iota(jnp.int32, (tr, tr), 0)
        c = lax.broadcasted_iota(jnp.int32, (tr, tr), 1)
        tri_s[...] = jnp.where(r <= c, 1.0, 0.0).astype(BF16)

    @pl.when(step % steps_per_chunk == 0)
    def _start_chunk():
        run_s[...] = cnt_in_ref[0]

    logits = lax.dot_general(rwt_ref[...], x, (((1,), (1,)), ((), ())),
                             precision=lax.Precision.HIGHEST, preferred_element_type=F32)
    scores = jax.nn.sigmoid(logits)
    biased = scores + rb_ref[...]
    per_group = n_exp // N_GROUPS
    grp = biased.reshape(N_GROUPS, per_group, tr)
    m1, hit1, _ = _first_argmax(grp, 1)
    m2 = jnp.max(jnp.where(hit1, -jnp.inf, grp), axis=1, keepdims=True)
    gscore = (m1 + m2).reshape(N_GROUPS, tr)
    gsel = jnp.zeros((N_GROUPS, tr), F32)
    for _ in range(TOPK_GROUPS):
        _, hit, _ = _first_argmax(gscore, 0)
        gsel = jnp.where(hit, 1.0, gsel)
        gscore = jnp.where(hit, -jnp.inf, gscore)
    emask = jnp.broadcast_to(gsel.reshape(N_GROUPS, 1, tr), (N_GROUPS, per_group, tr)).reshape(n_exp, tr)
    masked = jnp.where(emask > 0.0, biased, -jnp.inf)
    idx_rows, w_rows, hits = [], [], []
    for _ in range(TOP_K):
        _, hit, first = _first_argmax(masked, 0)
        idx_rows.append(first)
        hits.append(hit)
        w_rows.append(jnp.sum(jnp.where(hit, scores, 0.0), axis=0, keepdims=True))
        masked = jnp.where(hit, -jnp.inf, masked)
    w = jnp.concatenate(w_rows, axis=0)
    idx_ref[...] = jnp.concatenate(idx_rows, axis=0)
    w_ref[...] = w / jnp.sum(w, axis=0, keepdims=True) * ROUTED_SCALE

    sel = jnp.zeros((n_exp, tr), F32)
    for hit in hits:
        sel = jnp.where(hit, 1.0, sel)
    before = jnp.dot(sel.astype(BF16), tri_s[...], preferred_element_type=F32) - sel + run_s[:, 0:1]
    rank_ref[...] = jnp.concatenate(
        [jnp.sum(jnp.where(hit, before, 0.0), axis=0, keepdims=True) for hit in hits], axis=0).astype(jnp.int32)
    run_s[...] = run_s[...] + jnp.sum(sel, axis=1, keepdims=True)
    cnt_ref[0] = run_s[...]

    xb = x.astype(BF16)
    sgate = jnp.dot(xb, sg_ref[...], preferred_element_type=F32)
    sup = jnp.dot(xb, su_ref[...], preferred_element_type=F32)
    hs = (sgate * jax.nn.sigmoid(sgate) * sup).astype(BF16)
    resid_ref[...] = alpha * x + jnp.dot(hs, sd_ref[...], preferred_element_type=F32)


def _route_shared(x, router_wt, router_bias, sg, su, sd, cnt_in, *, alpha):
    t, d_model = x.shape
    n_exp = router_wt.shape[0]
    rows_per_chunk = t // MOE_CHUNKS
    tr = min(rows_per_chunk, ROUTE_ROWS)
    steps_per_chunk = rows_per_chunk // tr
    const2 = lambda i: (0, 0)
    chunk_map = lambda i: (i // steps_per_chunk, 0, 0)
    vmem = 2 * 2 * tr * d_model * 4 + 2 * (router_wt.size * 4 + 3 * sg.size * 2) + VMEM_HEADROOM_BYTES
    return pl.pallas_call(
        functools.partial(_route_body, alpha=alpha, steps_per_chunk=steps_per_chunk),
        grid=(t // tr,),
        in_specs=[pl.BlockSpec((tr, d_model), lambda i: (i, 0)),
                  pl.BlockSpec(router_wt.shape, const2), pl.BlockSpec((n_exp, 1), const2),
                  pl.BlockSpec(sg.shape, const2), pl.BlockSpec(su.shape, const2), pl.BlockSpec(sd.shape, const2),
                  pl.BlockSpec((1, n_exp, LANES), chunk_map)],
        out_specs=[pl.BlockSpec((tr, d_model), lambda i: (i, 0)),
                   pl.BlockSpec((TOP_K, tr), lambda i: (0, i)),
                   pl.BlockSpec((TOP_K, tr), lambda i: (0, i)),
                   pl.BlockSpec((TOP_K, tr), lambda i: (0, i)),
                   pl.BlockSpec((1, n_exp, LANES), chunk_map)],
        out_shape=[jax.ShapeDtypeStruct((t, d_model), F32),
                   jax.ShapeDtypeStruct((TOP_K, t), jnp.int32),
                   jax.ShapeDtypeStruct((TOP_K, t), F32),
                   jax.ShapeDtypeStruct((TOP_K, t), jnp.int32),
                   jax.ShapeDtypeStruct((MOE_CHUNKS, n_exp, LANES), F32)],
        scratch_shapes=[pltpu.VMEM((tr, tr), BF16), pltpu.VMEM((n_exp, LANES), F32)],
        compiler_params=pltpu.CompilerParams(dimension_semantics=("arbitrary",),
                                             vmem_limit_bytes=min(vmem, VMEM_LIMIT_CAP_BYTES)),
        name="route_shared",
    )(x, router_wt, router_bias.reshape(n_exp, 1), sg, su, sd, cnt_in)


def _moe_body(tile_e_ref, tile_c_ref, tile_f_ref, seg_cnt_ref, seg_base_ref,
              wg_ref, wu_ref, wd_ref, g2_ref, b2_ref,
              dest_hbm, wsel_hbm, xp_hbm, xs_hbm, rp_hbm, rs_hbm,
              op_hbm, os_hbm,
              x_s, y_s, xa_s, xb_s, oa_s, ob_s, tok_sm, wts_sm,
              dest_a_sm, dest_b_sm, wsel_a_sm, wsel_b_sm, sem_in, sem_out, sem_sm,
              *, tm, cp_rows, cs_rows, npan, tiles_per_chunk, top_k):
    chunk = cp_rows + cs_rows
    chunk_stride = chunk + SUBLANES
    half = tm // 2
    half_stride = half + SUBLANES
    dest_sm = (dest_a_sm, dest_b_sm)
    wsel_sm = (wsel_a_sm, wsel_b_sm)
    d_ff = wg_ref.shape[3]
    n_exp = seg_cnt_ref.shape[0] // MOE_CHUNKS
    j = pl.program_id(0)
    c = tile_c_ref[j]
    list0 = (j - c * tiles_per_chunk) * tm
    flags = tile_f_ref[j]
    chunk_first = (flags & 4) != 0
    chunk_last = (flags & 8) != 0
    both_halves = (flags & 17) == 1
    first_half_only = (flags & 17) == 17

    def chunk_copies(hbm_p, hbm_s, vmem, sem, to_vmem):
        copies = []
        for p in range(npan):
            for hbm, nrows, row0 in ((hbm_p, cp_rows, 0), (hbm_s, cs_rows, cp_rows)):
                h = hbm.at[pl.ds(c * nrows, nrows), pl.ds(p * LANES, LANES)]
                v = vmem.at[pl.ds(p * chunk_stride + row0, nrows), :]
                copies.append(pltpu.make_async_copy(h, v, sem.at[0]) if to_vmem
                              else pltpu.make_async_copy(v, h, sem.at[0]))
        return copies

    @pl.when(chunk_first)
    def _load_chunk():
        copies = (chunk_copies(xp_hbm, xs_hbm, x_s, sem_in, True)
                  + chunk_copies(rp_hbm, rs_hbm, y_s, sem_in, True))
        for cp in copies:
            cp.start()
        zeros = jnp.zeros((SUBLANES, LANES), F32)
        for p in range(npan):
            x_s[pl.ds(p * chunk_stride + chunk, SUBLANES), :] = zeros
            y_s[pl.ds(p * chunk_stride + chunk, SUBLANES), :] = zeros
        def stage(k):
            first = (c * top_k + k) * chunk
            return (pltpu.make_async_copy(dest_hbm.at[pl.ds(first, chunk)], dest_sm[k % 2], sem_sm.at[k % 2]),
                    pltpu.make_async_copy(wsel_hbm.at[pl.ds(first, chunk)], wsel_sm[k % 2], sem_sm.at[2 + k % 2]))
        staged = stage(0)
        for cp in staged:
            cp.start()
        for k in range(top_k):
            for cp in staged:
                cp.wait()
            if k + 1 < top_k:
                staged = stage(k + 1)
                for cp in staged:
                    cp.start()

            def fill(r, carry):
                for lane in range(LANES):
                    t = r * LANES + lane
                    d = dest_sm[k % 2][t]
                    tok_sm[d] = t
                    wts_sm[d] = wsel_sm[k % 2][t]
                return carry
            lax.fori_loop(0, chunk // LANES, fill, 0)

        def pad_segment(e, carry):
            n = seg_cnt_ref[c * n_exp + e]
            base = seg_base_ref[c * n_exp + e]

            def pad(i, carry2):
                tok_sm[i] = chunk
                wts_sm[i] = 0.0
                return carry2
            lax.fori_loop(base + n, base + (n + tm - 1) // tm * tm, pad, 0)
            return carry
        lax.fori_loop(0, n_exp, pad_segment, 0)

        for cp in copies:
            cp.wait()

    def gather_half(first, buf):
        for i in range(half):
            t = tok_sm[list0 + first + i]
            buf[pl.ds(i, npan, stride=half_stride), :] = x_s[pl.ds(t, npan, stride=chunk_stride), :]

    def ffn_half(src, dst):
        lhs = jnp.concatenate([src[pl.ds(p * half_stride, half), :] for p in range(npan)], axis=1).astype(BF16)
        g = jnp.dot(lhs, wg_ref[0, 0], preferred_element_type=F32)
        u = jnp.dot(lhs, wu_ref[0, 0], preferred_element_type=F32)
        hidden = (g * jax.nn.sigmoid(g) * u).astype(BF16)
        o = jnp.dot(hidden, wd_ref[0, 0], preferred_element_type=F32)
        for p in range(npan):
            dst[pl.ds(p * half_stride, half), :] = o[:, p * LANES:(p + 1) * LANES]

    def combine_half(first, buf):
        for i0 in range(0, half, SCATTER_UNROLL):
            toks, rows_new = [], []
            for i in range(i0, i0 + SCATTER_UNROLL):
                t = tok_sm[list0 + first + i]
                w = wts_sm[list0 + first + i]
                row = buf[pl.ds(i, npan, stride=half_stride), :]
                toks.append(t)
                rows_new.append(y_s[pl.ds(t, npan, stride=chunk_stride), :] + w * row)
            for t, new in zip(toks, rows_new):
                y_s[pl.ds(t, npan, stride=chunk_stride), :] = new

    @pl.when(both_halves)
    def _tile():
        gather_half(0, xa_s)
        gather_half(half, xb_s)
        ffn_half(xa_s, oa_s)
        ffn_half(xb_s, ob_s)
        combine_half(0, oa_s)
        combine_half(half, ob_s)

    @pl.when(first_half_only)
    def _half_tile():
        gather_half(0, xa_s)
        ffn_half(xa_s, oa_s)
        combine_half(0, oa_s)

    @pl.when(chunk_last)
    def _finish_chunk():
        blk = MOE_TILE
        g2 = g2_ref[...]
        b2 = b2_ref[...]

        def ln_block(rb, carry):
            r0 = pl.multiple_of(rb * blk, blk)
            v = jnp.concatenate([y_s[pl.ds(p * chunk_stride + r0, blk), :] for p in range(npan)], axis=1)
            out = _layer_norm(v, g2, b2)
            for p in range(npan):
                y_s[pl.ds(p * chunk_stride + r0, blk), :] = out[:, p * LANES:(p + 1) * LANES]
            return carry
        lax.fori_loop(0, chunk // blk, ln_block, 0)
        copies = chunk_copies(op_hbm, os_hbm, y_s, sem_out, False)
        for cp in copies:
            cp.start()
        for cp in copies:
            cp.wait()


def _moe_routed(tile_e, tile_c, tile_f, seg_cnt, seg_base, dest, wsel, wg, wu, wd, g2, b2, xp, xs, rp, rs, *,
                layer, top_k):
    n_tiles = tile_e.shape[0]
    _, n_exp, d_model, d_ff = wg.shape
    tm = MOE_TILE
    tiles_per_chunk = n_tiles // MOE_CHUNKS
    npan = d_model // LANES
    assert npan == SUBLANES, "one token row must fill exactly one (8, 128) register"
    cp_rows, cs_rows = xp.shape[0] // MOE_CHUNKS, xs.shape[0] // MOE_CHUNKS
    chunk = cp_rows + cs_rows
    assert chunk % MOE_TILE == 0 and cs_rows % SUBLANES == 0
    chunk_stride = chunk + SUBLANES
    half_stride = tm // 2 + SUBLANES
    any_spec = pl.BlockSpec(memory_space=pl.ANY)
    expert_map = lambda j, te, *_: (layer, te[j], 0, 0)
    const2 = lambda j, *_: (0, 0)
    expert_spec = lambda shape: pl.BlockSpec((1, 1) + shape, expert_map)
    vmem = (2 * npan * chunk_stride * LANES * 4 + 4 * npan * half_stride * LANES * 4
            + WEIGHT_BUFFERS * 3 * d_model * d_ff * wg.dtype.itemsize + VMEM_HEADROOM_BYTES)
    return pl.pallas_call(
        functools.partial(_moe_body, tm=tm, cp_rows=cp_rows, cs_rows=cs_rows, npan=npan,
                          tiles_per_chunk=tiles_per_chunk, top_k=top_k),
        grid_spec=pltpu.PrefetchScalarGridSpec(
            num_scalar_prefetch=5,
            grid=(n_tiles,),
            in_specs=[
                expert_spec((d_model, d_ff)),
                expert_spec((d_model, d_ff)),
                expert_spec((d_ff, d_model)),
                pl.BlockSpec((1, d_model), const2),
                pl.BlockSpec((1, d_model), const2),
                any_spec, any_spec, any_spec, any_spec, any_spec, any_spec,
            ],
            out_specs=[any_spec, any_spec],
            scratch_shapes=[
                pltpu.VMEM((npan * chunk_stride, LANES), F32),
                pltpu.VMEM((npan * chunk_stride, LANES), F32),
                pltpu.VMEM((npan * half_stride, LANES), F32),
                pltpu.VMEM((npan * half_stride, LANES), F32),
                pltpu.VMEM((npan * half_stride, LANES), F32),
                pltpu.VMEM((npan * half_stride, LANES), F32),
                pltpu.SMEM((tiles_per_chunk * tm,), jnp.int32),
                pltpu.SMEM((tiles_per_chunk * tm,), F32),
                pltpu.SMEM((chunk,), jnp.int32),
                pltpu.SMEM((chunk,), jnp.int32),
                pltpu.SMEM((chunk,), F32),
                pltpu.SMEM((chunk,), F32),
                pltpu.SemaphoreType.DMA((1,)),
                pltpu.SemaphoreType.DMA((1,)),
                pltpu.SemaphoreType.DMA((4,)),
            ]),
        out_shape=[jax.ShapeDtypeStruct(xp.shape, F32), jax.ShapeDtypeStruct(xs.shape, F32)],
        compiler_params=pltpu.CompilerParams(dimension_semantics=("arbitrary",),
                                             vmem_limit_bytes=min(vmem, VMEM_LIMIT_CAP_BYTES)),
        name="moe_routed",
    )(tile_e, tile_c, tile_f, seg_cnt, seg_base, wg, wu, wd,
      g2.reshape(1, d_model), b2.reshape(1, d_model), dest, wsel, xp, xs, rp, rs)


def _plan_tiles(cnt, idx_p, w_p, rank_p, idx_s, w_s, rank_s):
    k, tp = idx_p.shape
    ts = idx_s.shape[1]
    n_chunks, n_exp = cnt.shape
    tm = MOE_TILE
    cp, cs = tp // n_chunks, ts // n_chunks
    chunk = cp + cs
    tiles_per_chunk = (chunk * k) // tm + n_exp
    experts = jnp.arange(n_exp, dtype=jnp.int32)

    seg_tiles = (cnt + tm - 1) // tm
    seg_tile_end = jnp.sum(jnp.where(experts[None, :] <= experts[:, None], seg_tiles[:, None, :], 0), axis=-1)
    seg_tile_base = seg_tile_end - seg_tiles
    seg_base = seg_tile_base * tm
    n_active = seg_tile_end[:, -1]

    def rows_of(idx, rank, per_chunk):
        idx3 = idx.reshape(k, n_chunks, per_chunk)
        base = jnp.sum(jnp.where(idx3[..., None] == experts, seg_base[None, :, None, :], 0), axis=-1)
        return jnp.transpose(base + rank.reshape(k, n_chunks, per_chunk), (1, 0, 2))
    by_chunk = lambda a, per_chunk: jnp.transpose(a.reshape(k, n_chunks, per_chunk), (1, 0, 2))
    dest = jnp.concatenate([rows_of(idx_p, rank_p, cp), rows_of(idx_s, rank_s, cs)], axis=2).reshape(-1)
    wsel = jnp.concatenate([by_chunk(w_p, cp), by_chunk(w_s, cs)], axis=2).reshape(-1)

    jl = jnp.arange(tiles_per_chunk, dtype=jnp.int32)[None, :]
    active = jl < n_active[:, None]
    jc = jnp.minimum(jl, n_active[:, None] - 1)
    tile_e = jnp.sum((seg_tile_end[:, None, :] <= jc[:, :, None]).astype(jnp.int32), axis=-1)
    of_tile = lambda per_seg: jnp.sum(jnp.where(tile_e[..., None] == experts, per_seg[:, None, :], 0), axis=-1)
    first_tile = of_tile(seg_tile_base)
    rows_left = of_tile(cnt) - (jl - first_tile) * tm
    flags = (active.astype(jnp.int32) + 2 * (active & (jl == first_tile)).astype(jnp.int32)
             + 4 * (jl == 0).astype(jnp.int32) + 8 * (jl == n_active[:, None] - 1).astype(jnp.int32)
             + 16 * (rows_left <= tm // 2).astype(jnp.int32))
    tile_c = jnp.broadcast_to(jnp.arange(n_chunks, dtype=jnp.int32)[:, None], tile_e.shape)
    flat = lambda a: a.reshape(-1)
    return flat(tile_e), flat(tile_c), flat(flags), flat(cnt), flat(seg_base), dest, wsel


def kernel(x_prompt, x_sample, state_lru_conv, state_lru_h, state_ccm_conv, lru_w_in, lru_b_in, lru_conv_w, lru_conv_b, lru_w_a, lru_b_a, lru_w_x, lru_b_x, lru_lambda, lru_w_out, lru_b_out, ccm_w_in, ccm_b_in, ccm_dw_w, ccm_dw_b, ccm_ln_g, ccm_ln_b, ccm_w_out, ccm_b_out, ln1_g, ln1_b, ln2_g, ln2_b, router_w, router_bias, exp_w_gate, exp_w_up, exp_w_down, sh_w_gate, sh_w_up, sh_w_down):
    depth = ln1_g.shape[0]
    alpha = (2 * depth) ** 0.25
    n_exp = router_w.shape[2]
    bp, sp, d_model = x_prompt.shape
    bs, ss, _ = x_sample.shape
    kw_lru = lru_conv_w.shape[1]
    kw_ccm = ccm_dw_w.shape[1]
    d_rnn = lru_conv_w.shape[2]
    d_conv = ccm_dw_w.shape[2]
    bf = lambda a: a.astype(BF16)

    zero_lru_conv = jnp.zeros((bp, kw_lru - 1, d_rnn), F32)
    zero_lru_h = jnp.zeros((bp, d_rnn), F32)
    zero_ccm_conv = jnp.zeros((bp, kw_ccm - 1, d_conv), F32)

    exp_gate_bf, exp_up_bf, exp_down_bf = bf(exp_w_gate), bf(exp_w_up), bf(exp_w_down)

    xp, xs = x_prompt, x_sample
    lru_conv_p, lru_h_p, ccm_conv_p = [], [], []
    lru_conv_s, lru_h_s, ccm_conv_s = [], [], []
    for layer in range(depth):
        j = layer // 2
        if layer % 2 == 0:
            weights = (bf(lru_w_in[j]), lru_b_in[j], lru_conv_w[j], lru_conv_b[j], bf(lru_w_a[j]), lru_b_a[j],
                       bf(lru_w_x[j]), lru_b_x[j], lru_lambda[j], bf(lru_w_out[j]), lru_b_out[j],
                       ln1_g[layer], ln1_b[layer])
            xp, cb, hl = _lru_mixer(xp, zero_lru_conv, zero_lru_h, *weights, seq_start=True, alpha=alpha)
            lru_conv_p.append(cb)
            lru_h_p.append(hl)
            xs, cb, hl = _lru_mixer(xs, state_lru_conv[j], state_lru_h[j], *weights, seq_start=False, alpha=alpha)
            lru_conv_s.append(cb)
            lru_h_s.append(hl)
        else:
            weights = (bf(ccm_w_in[j]), ccm_b_in[j], ccm_dw_w[j], ccm_dw_b[j], ccm_ln_g[j], ccm_ln_b[j],
                       bf(ccm_w_out[j]), ccm_b_out[j], ln1_g[layer], ln1_b[layer])
            xp, cb = _ccm_mixer(xp, zero_ccm_conv, *weights, alpha=alpha)
            ccm_conv_p.append(cb)
            xs, cb = _ccm_mixer(xs, state_ccm_conv[j], *weights, alpha=alpha)
            ccm_conv_s.append(cb)

        shared = (router_w[layer].T, router_bias[layer], bf(sh_w_gate[layer]), bf(sh_w_up[layer]),
                  bf(sh_w_down[layer]))
        xp2 = xp.reshape(bp * sp, d_model)
        xs2 = xs.reshape(bs * ss, d_model)
        no_tokens = jnp.zeros((MOE_CHUNKS, n_exp, LANES), F32)
        resid_p, idx_p, w_p, rank_p, cnt_p = _route_shared(xp2, *shared, no_tokens, alpha=alpha)
        resid_s, idx_s, w_s, rank_s, cnt = _route_shared(xs2, *shared, cnt_p, alpha=alpha)
        tiles = _plan_tiles(cnt[:, :, 0].astype(jnp.int32), idx_p, w_p, rank_p, idx_s, w_s, rank_s)
        xp2, xs2 = _moe_routed(*tiles, exp_gate_bf, exp_up_bf, exp_down_bf, ln2_g[layer], ln2_b[layer],
                               xp2, xs2, resid_p, resid_s, layer=layer, top_k=idx_p.shape[0])
        xp = xp2.reshape(bp, sp, d_model)
        xs = xs2.reshape(bs, ss, d_model)

    return (xp, xs, jnp.stack(lru_conv_p), jnp.stack(lru_h_p), jnp.stack(ccm_conv_p),
            jnp.stack(lru_conv_s), jnp.stack(lru_h_s), jnp.stack(ccm_conv_s))
```

```python
import functools

import jax
import jax.numpy as jnp
from jax import lax
from jax.experimental import pallas as pl
from jax.experimental.pallas import tpu as pltpu

LANES = 128
SUBLANES = 8
LANE_BITS = LANES.bit_length() - 1
VMEM_HEADROOM_BYTES = 8 << 20
VMEM_LIMIT_CAP_BYTES = 56 << 20
CONV_STATE_BLOCK_BYTES = 4 << 20

LN_EPS = 1e-5
LRU_C = 8.0
N_GROUPS = 8
TOPK_GROUPS = 4
TOP_K = 8
ROUTED_SCALE = 2.5

MIXER_ROWS = 512
ROUTE_ROWS = 512
MOE_TILE = 256
MOE_CHUNKS = 4
SCATTER_UNROLL = 16
WEIGHT_BUFFERS = 2
ACC_VREGS = 32
SCAN_VREGS = 16

F32 = jnp.float32
BF16 = jnp.bfloat16


def _layer_norm(v, g, b):
    mu = jnp.mean(v, axis=-1, keepdims=True)
    d = v - mu
    var = jnp.mean(d * d, axis=-1, keepdims=True)
    return d * lax.rsqrt(var + LN_EPS) * g + b


def _round_up(n, m):
    return (n + m - 1) // m * m


def _lru_body(x_ref, cs_ref, h0_ref, win_ref, bin_ref, cw_ref, cb_ref, wa_ref, ba_ref, wx_ref, bx_ref,
              lam_ref, wout_ref, bout_ref, g1_ref, b1_ref,
              x1_ref, ncs_ref, hl_ref,
              ubuf, a_s, b_s, h_s, *, seq_start, alpha, scan_panels):
    nb, tc, d_model = x_ref.shape
    d_rnn = h0_ref.shape[1]
    heads = wa_ref.shape[0]
    hb = d_rnn // heads
    kw = cw_ref.shape[0]
    hist0 = SUBLANES - (kw - 1)
    rows = nb * tc
    ti = pl.program_id(1)

    @pl.when(ti == 0)
    def _init_state():
        ubuf[:, hist0:SUBLANES, :] = cs_ref[...]
        h_s[...] = h0_ref[...]

    x = x_ref[...].reshape(rows, d_model)
    proj = jnp.dot(x.astype(BF16), win_ref[...], preferred_element_type=F32) + bin_ref[...]
    gate = jax.nn.gelu(proj[:, :d_rnn])
    u = proj[:, d_rnn:]
    ubuf[:, SUBLANES:SUBLANES + tc, :] = u.reshape(nb, tc, d_rnn)
    xc3 = jnp.broadcast_to(cb_ref[...].reshape(1, 1, d_rnn), (nb, tc, d_rnn))
    for j in range(kw):
        xc3 = xc3 + cw_ref[j:j + 1, :].reshape(1, 1, d_rnn) * ubuf[:, hist0 + j:hist0 + j + tc, :]
    tail = ubuf[:, tc + hist0:tc + SUBLANES, :]
    ncs_ref[...] = tail
    ubuf[:, hist0:SUBLANES, :] = tail
    xc = xc3.reshape(rows, d_rnn)

    xcb = xc.astype(BF16)
    ra = jnp.concatenate([jnp.dot(xcb[:, h * hb:(h + 1) * hb], wa_ref[h], preferred_element_type=F32)
                          for h in range(heads)], axis=1)
    ia = jnp.concatenate([jnp.dot(xcb[:, h * hb:(h + 1) * hb], wx_ref[h], preferred_element_type=F32)
                          for h in range(heads)], axis=1)
    r = jax.nn.sigmoid(ra + ba_ref[...])
    gi = jax.nn.sigmoid(ia + bx_ref[...])
    lam = lam_ref[...]
    softplus_neg_lam = jnp.maximum(-lam, 0.0) + jnp.log1p(jnp.exp(-jnp.abs(lam)))
    log_a = (-LRU_C * softplus_neg_lam) * r
    a = jnp.exp(log_a)
    mult = jnp.sqrt(-jnp.tanh(log_a) * (a * a + 1.0))
    if seq_start:
        t_in_seq = lax.broadcasted_iota(jnp.int32, (rows, 1), 0) % tc
        mult = jnp.where((t_in_seq == 0) & (ti == 0), 1.0, mult)
    bterm = xc * gi * mult
    npan = d_rnn // LANES
    for p in range(npan):
        a_s[p] = a[:, p * LANES:(p + 1) * LANES]
        b_s[p] = bterm[:, p * LANES:(p + 1) * LANES]

    for p0 in range(0, npan, scan_panels):
        group = range(p0, min(p0 + scan_panels, npan))
        hs = [h_s[:, p * LANES:(p + 1) * LANES] for p in group]
        for t in range(tc):
            rs = pl.ds(t, nb, stride=tc)
            for k, p in enumerate(group):
                hs[k] = a_s[p, rs, :] * hs[k] + b_s[p, rs, :]
                b_s[p, rs, :] = hs[k]
        for k, p in enumerate(group):
            h_s[:, p * LANES:(p + 1) * LANES] = hs[k]
    hl_ref[...] = h_s[...]

    h_all = jnp.concatenate([b_s[p] for p in range(npan)], axis=1)
    y = jnp.dot((h_all * gate).astype(BF16), wout_ref[...], preferred_element_type=F32) + bout_ref[...]
    x1 = _layer_norm(alpha * x + y, g1_ref[...], b1_ref[...])
    x1_ref[...] = x1.reshape(nb, tc, d_model)


def _lru_mixer(x, conv_state, h0, win, b_in, conv_w, conv_b, wa, ba, wx, bx, lam, wout, bout, g1, b1, *,
               seq_start, alpha):
    n, s, d_model = x.shape
    d_rnn = h0.shape[1]
    kw = conv_w.shape[0]
    tc = min(s, MIXER_ROWS // SUBLANES)
    nb = min(n, MIXER_ROWS // tc)
    rows = nb * tc
    scan_panels = max(1, SCAN_VREGS * SUBLANES // nb)
    row = lambda v: v.reshape(1, -1)
    const2 = lambda i, t: (0, 0)
    const3 = lambda i, t: (0, 0, 0)
    vmem = (2 * 2 * rows * d_model * 4 + 2 * (win.size + wout.size + wa.size + wx.size) * 2
            + (nb * (tc + SUBLANES) + 2 * rows + nb) * d_rnn * 4 + 4 * rows * 2 * d_rnn * 4 + VMEM_HEADROOM_BYTES)
    return pl.pallas_call(
        functools.partial(_lru_body, seq_start=seq_start, alpha=alpha, scan_panels=scan_panels),
        grid=(n // nb, s // tc),
        in_specs=[
            pl.BlockSpec((nb, tc, d_model), lambda i, t: (i, t, 0)),
            pl.BlockSpec((nb, kw - 1, d_rnn), lambda i, t: (i, 0, 0)),
            pl.BlockSpec((nb, d_rnn), lambda i, t: (i, 0)),
            pl.BlockSpec(win.shape, const2), pl.BlockSpec((1, 2 * d_rnn), const2),
            pl.BlockSpec(conv_w.shape, const2), pl.BlockSpec((1, d_rnn), const2),
            pl.BlockSpec(wa.shape, const3), pl.BlockSpec((1, d_rnn), const2),
            pl.BlockSpec(wx.shape, const3), pl.BlockSpec((1, d_rnn), const2),
            pl.BlockSpec((1, d_rnn), const2),
            pl.BlockSpec(wout.shape, const2), pl.BlockSpec((1, d_model), const2),
            pl.BlockSpec((1, d_model), const2), pl.BlockSpec((1, d_model), const2),
        ],
        out_specs=[
            pl.BlockSpec((nb, tc, d_model), lambda i, t: (i, t, 0)),
            pl.BlockSpec((nb, kw - 1, d_rnn), lambda i, t: (i, 0, 0)),
            pl.BlockSpec((nb, d_rnn), lambda i, t: (i, 0)),
        ],
        out_shape=[jax.ShapeDtypeStruct(x.shape, F32),
                   jax.ShapeDtypeStruct((n, kw - 1, d_rnn), F32),
                   jax.ShapeDtypeStruct((n, d_rnn), F32)],
        scratch_shapes=[pltpu.VMEM((nb, tc + SUBLANES, d_rnn), F32),
                        pltpu.VMEM((d_rnn // LANES, rows, LANES), F32),
                        pltpu.VMEM((d_rnn // LANES, rows, LANES), F32),
                        pltpu.VMEM((nb, d_rnn), F32)],
        compiler_params=pltpu.CompilerParams(dimension_semantics=("arbitrary", "arbitrary"),
                                             vmem_limit_bytes=min(vmem, VMEM_LIMIT_CAP_BYTES)),
        name="lru_mixer",
    )(x, conv_state, h0, win, row(b_in), conv_w, row(conv_b), wa, row(ba), wx, row(bx), row(lam),
      wout, row(bout), row(g1), row(b1))


def _ccm_body(x_ref, cs_ref, win_ref, bin_ref, dw_ref, dwb_ref, lng_ref, lnb_ref, wout_ref, bout_ref,
              g1_ref, b1_ref, x1_ref, ncs_ref, gbuf, cbuf, *, alpha, conv_cols):
    nb, tc, d_model = x_ref.shape
    kw, d_conv = dw_ref.shape
    hist_rows = _round_up(kw - 1, SUBLANES)
    hist0 = hist_rows - (kw - 1)
    rows = nb * tc
    ti = pl.program_id(1)

    @pl.when(ti == 0)
    def _init_state():
        gbuf[:, hist0:hist_rows, :] = cs_ref[...]

    x = x_ref[...].reshape(rows, d_model)
    p = jnp.dot(x.astype(BF16), win_ref[...], preferred_element_type=F32) + bin_ref[...]
    glu = p[:, :d_conv] * jax.nn.sigmoid(p[:, d_conv:])
    gbuf[:, hist_rows:hist_rows + tc, :] = glu.reshape(nb, tc, d_conv)

    def conv_seq(n, carry):
        for c0 in range(0, d_conv, conv_cols):
            cols = slice(c0, c0 + conv_cols)
            acc = jnp.broadcast_to(dwb_ref[:, cols], (tc, conv_cols))
            for j in range(kw):
                acc = acc + dw_ref[j:j + 1, cols] * gbuf[n, hist0 + j:hist0 + j + tc, cols]
            cbuf[n, :, cols] = acc
        return carry
    lax.fori_loop(0, nb, conv_seq, 0)

    tail = gbuf[:, tc + hist0:tc + hist_rows, :]
    ncs_ref[...] = tail
    gbuf[:, hist0:hist_rows, :] = tail

    c = cbuf[...].reshape(rows, d_conv)
    hn = _layer_norm(c, lng_ref[...], lnb_ref[...])
    hdn = hn * jax.nn.sigmoid(hn)
    y = jnp.dot(hdn.astype(BF16), wout_ref[...], preferred_element_type=F32) + bout_ref[...]
    x1 = _layer_norm(alpha * x + y, g1_ref[...], b1_ref[...])
    x1_ref[...] = x1.reshape(nb, tc, d_model)


def _ccm_mixer(x, conv_state, win, b_in, dw_w, dw_b, ln_g, ln_b, wout, bout, g1, b1, *, alpha):
    n, s, d_model = x.shape
    kw, d_conv = dw_w.shape
    tc = min(s, MIXER_ROWS // SUBLANES)
    nb = min(n, MIXER_ROWS // tc)
    while nb > SUBLANES and nb * (kw - 1) * d_conv * 4 > CONV_STATE_BLOCK_BYTES:
        nb //= 2
    rows = nb * tc
    hist_rows = _round_up(kw - 1, SUBLANES)
    conv_cols = min(d_conv, max(LANES, (ACC_VREGS * SUBLANES * LANES // tc) // LANES * LANES))
    while d_conv % conv_cols:
        conv_cols -= LANES
    row = lambda v: v.reshape(1, -1)
    const2 = lambda i, t: (0, 0)
    vmem = (2 * 2 * rows * d_model * 4 + 2 * (win.size + wout.size) * 2
            + (nb * (tc + hist_rows) + rows) * d_conv * 4 + 4 * nb * (kw - 1) * d_conv * 4
            + 4 * rows * 2 * d_conv * 4 + VMEM_HEADROOM_BYTES)
    return pl.pallas_call(
        functools.partial(_ccm_body, alpha=alpha, conv_cols=conv_cols),
        grid=(n // nb, s // tc),
        in_specs=[
            pl.BlockSpec((nb, tc, d_model), lambda i, t: (i, t, 0)),
            pl.BlockSpec((nb, kw - 1, d_conv), lambda i, t: (i, 0, 0)),
            pl.BlockSpec(win.shape, const2), pl.BlockSpec((1, 2 * d_conv), const2),
            pl.BlockSpec(dw_w.shape, const2), pl.BlockSpec((1, d_conv), const2),
            pl.BlockSpec((1, d_conv), const2), pl.BlockSpec((1, d_conv), const2),
            pl.BlockSpec(wout.shape, const2), pl.BlockSpec((1, d_model), const2),
            pl.BlockSpec((1, d_model), const2), pl.BlockSpec((1, d_model), const2),
        ],
        out_specs=[
            pl.BlockSpec((nb, tc, d_model), lambda i, t: (i, t, 0)),
            pl.BlockSpec((nb, kw - 1, d_conv), lambda i, t: (i, 0, 0)),
        ],
        out_shape=[jax.ShapeDtypeStruct(x.shape, F32),
                   jax.ShapeDtypeStruct((n, kw - 1, d_conv), F32)],
        scratch_shapes=[pltpu.VMEM((nb, tc + hist_rows, d_conv), F32),
                        pltpu.VMEM((nb, tc, d_conv), F32)],
        compiler_params=pltpu.CompilerParams(dimension_semantics=("arbitrary", "arbitrary"),
                                             vmem_limit_bytes=min(vmem, VMEM_LIMIT_CAP_BYTES)),
        name="ccm_mixer",
    )(x, conv_state, win, row(b_in), dw_w, row(dw_b), row(ln_g), row(ln_b), wout, row(bout), row(g1), row(b1))


def _first_argmax(v, axis):
    m = jnp.max(v, axis=axis, keepdims=True)
    iota = lax.broadcasted_iota(jnp.int32, v.shape, axis)
    first = jnp.min(jnp.where(v == m, iota, v.shape[axis]), axis=axis, keepdims=True)
    return m, iota == first, first


def _route_body(x_ref, rwt_ref, rb_ref, sg_ref, su_ref, sd_ref, cnt_in_ref,
                resid_ref, idx_ref, w_ref, rank_ref, cnt_ref, tri_s, run_s, *, alpha, steps_per_chunk):
    x = x_ref[...]
    n_exp = rwt_ref.shape[0]
    tr = x.shape[0]
    step = pl.program_id(0)

    @pl.when(step == 0)
    def _build_prefix_matrix():
        r = lax.broadcasted_iota(jnp.int32, (tr, tr), 0)
        c = lax.broadcasted_iota(jnp.int32, (tr, tr), 1)
        tri_s[...] = jnp.where(r <= c, 1.0, 0.0).astype(BF16)

    @pl.when(step % steps_per_chunk == 0)
    def _start_chunk():
        run_s[...] = cnt_in_ref[0]

    logits = lax.dot_general(rwt_ref[...], x, (((1,), (1,)), ((), ())),
                             precision=lax.Precision.HIGHEST, preferred_element_type=F32)
    scores = jax.nn.sigmoid(logits)
    biased = scores + rb_ref[...]
    per_group = n_exp // N_GROUPS
    grp = biased.reshape(N_GROUPS, per_group, tr)
    m1, hit1, _ = _first_argmax(grp, 1)
    m2 = jnp.max(jnp.where(hit1, -jnp.inf, grp), axis=1, keepdims=True)
    gscore = (m1 + m2).reshape(N_GROUPS, tr)
    gsel = jnp.zeros((N_GROUPS, tr), F32)
    for _ in range(TOPK_GROUPS):
        _, hit, _ = _first_argmax(gscore, 0)
        gsel = jnp.where(hit, 1.0, gsel)
        gscore = jnp.where(hit, -jnp.inf, gscore)
    emask = jnp.broadcast_to(gsel.reshape(N_GROUPS, 1, tr), (N_GROUPS, per_group, tr)).reshape(n_exp, tr)
    masked = jnp.where(emask > 0.0, biased, -jnp.inf)
    idx_rows, w_rows, hits = [], [], []
    for _ in range(TOP_K):
        _, hit, first = _first_argmax(masked, 0)
        idx_rows.append(first)
        hits.append(hit)
        w_rows.append(jnp.sum(jnp.where(hit, scores, 0.0), axis=0, keepdims=True))
        masked = jnp.where(hit, -jnp.inf, masked)
    w = jnp.concatenate(w_rows, axis=0)
    idx_ref[...] = jnp.concatenate(idx_rows, axis=0)
    w_ref[...] = w / jnp.sum(w, axis=0, keepdims=True) * ROUTED_SCALE

    sel = jnp.zeros((n_exp, tr), F32)
    for hit in hits:
        sel = jnp.where(hit, 1.0, sel)
    before = jnp.dot(sel.astype(BF16), tri_s[...], preferred_element_type=F32) - sel + run_s[:, 0:1]
    rank_ref[...] = jnp.concatenate(
        [jnp.sum(jnp.where(hit, before, 0.0), axis=0, keepdims=True) for hit in hits], axis=0).astype(jnp.int32)
    run_s[...] = run_s[...] + jnp.sum(sel, axis=1, keepdims=True)
    cnt_ref[0] = run_s[...]

    xb = x.astype(BF16)
    sgate = jnp.dot(xb, sg_ref[...], preferred_element_type=F32)
    sup = jnp.dot(xb, su_ref[...], preferred_element_type=F32)
    hs = (sgate * jax.nn.sigmoid(sgate) * sup).astype(BF16)
    resid_ref[...] = alpha * x + jnp.dot(hs, sd_ref[...], preferred_element_type=F32)


def _route_shared(x, router_wt, router_bias, sg, su, sd, cnt_in, *, alpha):
    t, d_model = x.shape
    n_exp = router_wt.shape[0]
    rows_per_chunk = t // MOE_CHUNKS
    tr = min(rows_per_chunk, ROUTE_ROWS)
    steps_per_chunk = rows_per_chunk // tr
    const2 = lambda i: (0, 0)
    chunk_map = lambda i: (i // steps_per_chunk, 0, 0)
    vmem = 2 * 2 * tr * d_model * 4 + 2 * (router_wt.size * 4 + 3 * sg.size * 2) + VMEM_HEADROOM_BYTES
    return pl.pallas_call(
        functools.partial(_route_body, alpha=alpha, steps_per_chunk=steps_per_chunk),
        grid=(t // tr,),
        in_specs=[pl.BlockSpec((tr, d_model), lambda i: (i, 0)),
                  pl.BlockSpec(router_wt.shape, const2), pl.BlockSpec((n_exp, 1), const2),
                  pl.BlockSpec(sg.shape, const2), pl.BlockSpec(su.shape, const2), pl.BlockSpec(sd.shape, const2),
                  pl.BlockSpec((1, n_exp, LANES), chunk_map)],
        out_specs=[pl.BlockSpec((tr, d_model), lambda i: (i, 0)),
                   pl.BlockSpec((TOP_K, tr), lambda i: (0, i)),
                   pl.BlockSpec((TOP_K, tr), lambda i: (0, i)),
                   pl.BlockSpec((TOP_K, tr), lambda i: (0, i)),
                   pl.BlockSpec((1, n_exp, LANES), chunk_map)],
        out_shape=[jax.ShapeDtypeStruct((t, d_model), F32),
                   jax.ShapeDtypeStruct((TOP_K, t), jnp.int32),
                   jax.ShapeDtypeStruct((TOP_K, t), F32),
                   jax.ShapeDtypeStruct((TOP_K, t), jnp.int32),
                   jax.ShapeDtypeStruct((MOE_CHUNKS, n_exp, LANES), F32)],
        scratch_shapes=[pltpu.VMEM((tr, tr), BF16), pltpu.VMEM((n_exp, LANES), F32)],
        compiler_params=pltpu.CompilerParams(dimension_semantics=("arbitrary",),
                                             vmem_limit_bytes=min(vmem, VMEM_LIMIT_CAP_BYTES)),
        name="route_shared",
    )(x, router_wt, router_bias.reshape(n_exp, 1), sg, su, sd, cnt_in)


def _moe_body(seg_cnt_ref, seg_base_ref,
              wg_ref, wu_ref, wd_ref, g2_ref, b2_ref,
              dest_hbm, wsel_hbm, xp_hbm, xs_hbm, rp_hbm, rs_hbm,
              op_hbm, os_hbm,
              x_s, y_s, xa_s, xb_s, oa_s, ob_s, tok_sm, wts_sm,
              dest_a_sm, dest_b_sm, wsel_a_sm, wsel_b_sm, sem_in, sem_out, sem_sm,
              *, tm, cp_rows, cs_rows, npan, top_k):
    chunk = cp_rows + cs_rows
    chunk_stride = chunk + SUBLANES
    half = tm // 2
    half_stride = half + SUBLANES
    dest_sm = (dest_a_sm, dest_b_sm)
    wsel_sm = (wsel_a_sm, wsel_b_sm)
    n_exp = seg_cnt_ref.shape[0] // MOE_CHUNKS
    seg = pl.program_id(0)
    c = seg // n_exp
    chunk_first = seg % n_exp == 0
    chunk_last = seg % n_exp == n_exp - 1
    n_rows = seg_cnt_ref[seg]
    seg_row0 = seg_base_ref[seg]
    n_whole = jnp.maximum(n_rows - half + tm - 1, 0) // tm
    rows_after = n_rows - n_whole * tm

    def chunk_copies(hbm_p, hbm_s, vmem, sem, to_vmem):
        copies = []
        for p in range(npan):
            for hbm, nrows, row0 in ((hbm_p, cp_rows, 0), (hbm_s, cs_rows, cp_rows)):
                h = hbm.at[pl.ds(c * nrows, nrows), pl.ds(p * LANES, LANES)]
                v = vmem.at[pl.ds(p * chunk_stride + row0, nrows), :]
                copies.append(pltpu.make_async_copy(h, v, sem.at[0]) if to_vmem
                              else pltpu.make_async_copy(v, h, sem.at[0]))
        return copies

    @pl.when(chunk_first)
    def _load_chunk():
        copies = (chunk_copies(xp_hbm, xs_hbm, x_s, sem_in, True)
                  + chunk_copies(rp_hbm, rs_hbm, y_s, sem_in, True))
        for cp in copies:
            cp.start()
        zeros = jnp.zeros((SUBLANES, LANES), F32)
        for p in range(npan):
            x_s[pl.ds(p * chunk_stride + chunk, SUBLANES), :] = zeros
            y_s[pl.ds(p * chunk_stride + chunk, SUBLANES), :] = zeros
        def stage(k):
            first = (c * top_k + k) * chunk
            return (pltpu.make_async_copy(dest_hbm.at[pl.ds(first, chunk)], dest_sm[k % 2], sem_sm.at[k % 2]),
                    pltpu.make_async_copy(wsel_hbm.at[pl.ds(first, chunk)], wsel_sm[k % 2], sem_sm.at[2 + k % 2]))
        staged = stage(0)
        for cp in staged:
            cp.start()
        for k in range(top_k):
            for cp in staged:
                cp.wait()
            if k + 1 < top_k:
                staged = stage(k + 1)
                for cp in staged:
                    cp.start()

            def fill(r, carry):
                for lane in range(LANES):
                    t = r * LANES + lane
                    d = dest_sm[k % 2][t]
                    tok_sm[d] = t
                    wts_sm[d] = wsel_sm[k % 2][t]
                return carry
            lax.fori_loop(0, chunk // LANES, fill, 0)

        def pad_segment(e, carry):
            n = seg_cnt_ref[c * n_exp + e]
            base = seg_base_ref[c * n_exp + e]

            def pad(i, carry2):
                tok_sm[i] = chunk
                wts_sm[i] = 0.0
                return carry2
            lax.fori_loop(base + n, base + (n + tm - 1) // tm * tm, pad, 0)
            return carry
        lax.fori_loop(0, n_exp, pad_segment, 0)

        for cp in copies:
            cp.wait()

    def gather_half(first, buf):
        for i in range(half):
            t = tok_sm[first + i]
            buf[pl.ds(i, npan, stride=half_stride), :] = x_s[pl.ds(t, npan, stride=chunk_stride), :]

    def ffn_half(src, dst):
        lhs = jnp.concatenate([src[pl.ds(p * half_stride, half), :] for p in range(npan)], axis=1).astype(BF16)
        g = jnp.dot(lhs, wg_ref[0, 0], preferred_element_type=F32)
        u = jnp.dot(lhs, wu_ref[0, 0], preferred_element_type=F32)
        hidden = (g * jax.nn.sigmoid(g) * u).astype(BF16)
        o = jnp.dot(hidden, wd_ref[0, 0], preferred_element_type=F32)
        for p in range(npan):
            dst[pl.ds(p * half_stride, half), :] = o[:, p * LANES:(p + 1) * LANES]

    def combine_half(first, buf):
        for i0 in range(0, half, SCATTER_UNROLL):
            toks, rows_new = [], []
            for i in range(i0, i0 + SCATTER_UNROLL):
                t = tok_sm[first + i]
                w = wts_sm[first + i]
                row = buf[pl.ds(i, npan, stride=half_stride), :]
                toks.append(t)
                rows_new.append(y_s[pl.ds(t, npan, stride=chunk_stride), :] + w * row)
            for t, new in zip(toks, rows_new):
                y_s[pl.ds(t, npan, stride=chunk_stride), :] = new

    def whole_tile(i, carry):
        first = seg_row0 + i * tm
        gather_half(first, xa_s)
        gather_half(first + half, xb_s)
        ffn_half(xa_s, oa_s)
        ffn_half(xb_s, ob_s)
        combine_half(first, oa_s)
        combine_half(first + half, ob_s)
        return carry
    lax.fori_loop(0, n_whole, whole_tile, 0)

    @pl.when(rows_after > 0)
    def _half_tile():
        first = seg_row0 + n_whole * tm
        gather_half(first, xa_s)
        ffn_half(xa_s, oa_s)
        combine_half(first, oa_s)

    @pl.when(chunk_last)
    def _finish_chunk():
        blk = MOE_TILE
        g2 = g2_ref[...]
        b2 = b2_ref[...]

        def ln_block(rb, carry):
            r0 = pl.multiple_of(rb * blk, blk)
            v = jnp.concatenate([y_s[pl.ds(p * chunk_stride + r0, blk), :] for p in range(npan)], axis=1)
            out = _layer_norm(v, g2, b2)
            for p in range(npan):
                y_s[pl.ds(p * chunk_stride + r0, blk), :] = out[:, p * LANES:(p + 1) * LANES]
            return carry
        lax.fori_loop(0, chunk // blk, ln_block, 0)
        copies = chunk_copies(op_hbm, os_hbm, y_s, sem_out, False)
        for cp in copies:
            cp.start()
        for cp in copies:
            cp.wait()


def _moe_routed(seg_cnt, seg_base, dest, wsel, wg, wu, wd, g2, b2, xp, xs, rp, rs, *, layer, top_k):
    _, n_exp, d_model, d_ff = wg.shape
    tm = MOE_TILE
    npan = d_model // LANES
    assert npan == SUBLANES, "one token row must fill exactly one (8, 128) register"
    cp_rows, cs_rows = xp.shape[0] // MOE_CHUNKS, xs.shape[0] // MOE_CHUNKS
    chunk = cp_rows + cs_rows
    assert chunk % MOE_TILE == 0 and cs_rows % SUBLANES == 0
    chunk_stride = chunk + SUBLANES
    half_stride = tm // 2 + SUBLANES
    list_rows = (chunk * top_k) // tm * tm + n_exp * tm
    any_spec = pl.BlockSpec(memory_space=pl.ANY)
    expert_map = lambda s, *_: (layer, s % n_exp, 0, 0)
    const2 = lambda s, *_: (0, 0)
    expert_spec = lambda shape: pl.BlockSpec((1, 1) + shape, expert_map)
    vmem = (2 * npan * chunk_stride * LANES * 4 + 4 * npan * half_stride * LANES * 4
            + WEIGHT_BUFFERS * 3 * d_model * d_ff * wg.dtype.itemsize + VMEM_HEADROOM_BYTES)
    return pl.pallas_call(
        functools.partial(_moe_body, tm=tm, cp_rows=cp_rows, cs_rows=cs_rows, npan=npan, top_k=top_k),
        grid_spec=pltpu.PrefetchScalarGridSpec(
            num_scalar_prefetch=2,
            grid=(MOE_CHUNKS * n_exp,),
            in_specs=[
                expert_spec((d_model, d_ff)),
                expert_spec((d_model, d_ff)),
                expert_spec((d_ff, d_model)),
                pl.BlockSpec((1, d_model), const2),
                pl.BlockSpec((1, d_model), const2),
                any_spec, any_spec, any_spec, any_spec, any_spec, any_spec,
            ],
            out_specs=[any_spec, any_spec],
            scratch_shapes=[
                pltpu.VMEM((npan * chunk_stride, LANES), F32),
                pltpu.VMEM((npan * chunk_stride, LANES), F32),
                pltpu.VMEM((npan * half_stride, LANES), F32),
                pltpu.VMEM((npan * half_stride, LANES), F32),
                pltpu.VMEM((npan * half_stride, LANES), F32),
                pltpu.VMEM((npan * half_stride, LANES), F32),
                pltpu.SMEM((list_rows,), jnp.int32),
                pltpu.SMEM((list_rows,), F32),
                pltpu.SMEM((chunk,), jnp.int32),
                pltpu.SMEM((chunk,), jnp.int32),
                pltpu.SMEM((chunk,), F32),
                pltpu.SMEM((chunk,), F32),
                pltpu.SemaphoreType.DMA((1,)),
                pltpu.SemaphoreType.DMA((1,)),
                pltpu.SemaphoreType.DMA((4,)),
            ]),
        out_shape=[jax.ShapeDtypeStruct(xp.shape, F32), jax.ShapeDtypeStruct(xs.shape, F32)],
        compiler_params=pltpu.CompilerParams(dimension_semantics=("arbitrary",),
                                             vmem_limit_bytes=min(vmem, VMEM_LIMIT_CAP_BYTES)),
        name="moe_routed",
    )(seg_cnt, seg_base, wg, wu, wd, g2.reshape(1, d_model), b2.reshape(1, d_model), dest, wsel, xp, xs, rp, rs)


def _plan_tiles(cnt, idx_p, w_p, rank_p, idx_s, w_s, rank_s):
    k, tp = idx_p.shape
    ts = idx_s.shape[1]
    n_chunks, n_exp = cnt.shape
    tm = MOE_TILE
    cp, cs = tp // n_chunks, ts // n_chunks
    experts = jnp.arange(n_exp, dtype=jnp.int32)

    seg_rows = (cnt + tm - 1) // tm * tm
    seg_base = jnp.sum(jnp.where(experts[None, :] < experts[:, None], seg_rows[:, None, :], 0), axis=-1)

    def rows_of(idx, rank, per_chunk):
        idx3 = idx.reshape(k, n_chunks, per_chunk)
        base = jnp.sum(jnp.where(idx3[..., None] == experts, seg_base[None, :, None, :], 0), axis=-1)
        return jnp.transpose(base + rank.reshape(k, n_chunks, per_chunk), (1, 0, 2))
    by_chunk = lambda a, per_chunk: jnp.transpose(a.reshape(k, n_chunks, per_chunk), (1, 0, 2))
    dest = jnp.concatenate([rows_of(idx_p, rank_p, cp), rows_of(idx_s, rank_s, cs)], axis=2).reshape(-1)
    wsel = jnp.concatenate([by_chunk(w_p, cp), by_chunk(w_s, cs)], axis=2).reshape(-1)
    return cnt.reshape(-1), seg_base.reshape(-1), dest, wsel


def kernel(x_prompt, x_sample, state_lru_conv, state_lru_h, state_ccm_conv, lru_w_in, lru_b_in, lru_conv_w, lru_conv_b, lru_w_a, lru_b_a, lru_w_x, lru_b_x, lru_lambda, lru_w_out, lru_b_out, ccm_w_in, ccm_b_in, ccm_dw_w, ccm_dw_b, ccm_ln_g, ccm_ln_b, ccm_w_out, ccm_b_out, ln1_g, ln1_b, ln2_g, ln2_b, router_w, router_bias, exp_w_gate, exp_w_up, exp_w_down, sh_w_gate, sh_w_up, sh_w_down):
    depth = ln1_g.shape[0]
    alpha = (2 * depth) ** 0.25
    n_exp = router_w.shape[2]
    bp, sp, d_model = x_prompt.shape
    bs, ss, _ = x_sample.shape
    kw_lru = lru_conv_w.shape[1]
    kw_ccm = ccm_dw_w.shape[1]
    d_rnn = lru_conv_w.shape[2]
    d_conv = ccm_dw_w.shape[2]
    bf = lambda a: a.astype(BF16)

    zero_lru_conv = jnp.zeros((bp, kw_lru - 1, d_rnn), F32)
    zero_lru_h = jnp.zeros((bp, d_rnn), F32)
    zero_ccm_conv = jnp.zeros((bp, kw_ccm - 1, d_conv), F32)

    exp_gate_bf, exp_up_bf, exp_down_bf = bf(exp_w_gate), bf(exp_w_up), bf(exp_w_down)

    xp, xs = x_prompt, x_sample
    lru_conv_p, lru_h_p, ccm_conv_p = [], [], []
    lru_conv_s, lru_h_s, ccm_conv_s = [], [], []
    for layer in range(depth):
        j = layer // 2
        if layer % 2 == 0:
            weights = (bf(lru_w_in[j]), lru_b_in[j], lru_conv_w[j], lru_conv_b[j], bf(lru_w_a[j]), lru_b_a[j],
                       bf(lru_w_x[j]), lru_b_x[j], lru_lambda[j], bf(lru_w_out[j]), lru_b_out[j],
                       ln1_g[layer], ln1_b[layer])
            xp, cb, hl = _lru_mixer(xp, zero_lru_conv, zero_lru_h, *weights, seq_start=True, alpha=alpha)
            lru_conv_p.append(cb)
            lru_h_p.append(hl)
            xs, cb, hl = _lru_mixer(xs, state_lru_conv[j], state_lru_h[j], *weights, seq_start=False, alpha=alpha)
            lru_conv_s.append(cb)
            lru_h_s.append(hl)
        else:
            weights = (bf(ccm_w_in[j]), ccm_b_in[j], ccm_dw_w[j], ccm_dw_b[j], ccm_ln_g[j], ccm_ln_b[j],
                       bf(ccm_w_out[j]), ccm_b_out[j], ln1_g[layer], ln1_b[layer])
            xp, cb = _ccm_mixer(xp, zero_ccm_conv, *weights, alpha=alpha)
            ccm_conv_p.append(cb)
            xs, cb = _ccm_mixer(xs, state_ccm_conv[j], *weights, alpha=alpha)
            ccm_conv_s.append(cb)

        shared = (router_w[layer].T, router_bias[layer], bf(sh_w_gate[layer]), bf(sh_w_up[layer]),
                  bf(sh_w_down[layer]))
        xp2 = xp.reshape(bp * sp, d_model)
        xs2 = xs.reshape(bs * ss, d_model)
        no_tokens = jnp.zeros((MOE_CHUNKS, n_exp, LANES), F32)
        resid_p, idx_p, w_p, rank_p, cnt_p = _route_shared(xp2, *shared, no_tokens, alpha=alpha)
        resid_s, idx_s, w_s, rank_s, cnt = _route_shared(xs2, *shared, cnt_p, alpha=alpha)
        tiles = _plan_tiles(cnt[:, :, 0].astype(jnp.int32), idx_p, w_p, rank_p, idx_s, w_s, rank_s)
        xp2, xs2 = _moe_routed(*tiles, exp_gate_bf, exp_up_bf, exp_down_bf, ln2_g[layer], ln2_b[layer],
                               xp2, xs2, resid_p, resid_s, layer=layer, top_k=idx_p.shape[0])
        xp = xp2.reshape(bp, sp, d_model)
        xs = xs2.reshape(bs, ss, d_model)

    return (xp, xs, jnp.stack(lru_conv_p), jnp.stack(lru_h_p), jnp.stack(ccm_conv_p),
            jnp.stack(lru_conv_s), jnp.stack(lru_h_s), jnp.stack(ccm_conv_s))
```

```python
import functools

import jax
import jax.numpy as jnp
from jax import lax
from jax.experimental import pallas as pl
from jax.experimental.pallas import tpu as pltpu
from jax.experimental.pallas import tpu_sc as plsc

LANES = 128
SUBLANES = 8
SC_CORES = 2
SC_SUBCORES = 16
SC_LANES = 16
VMEM_HEADROOM_BYTES = 8 << 20
VMEM_LIMIT_CAP_BYTES = 56 << 20
CONV_STATE_BLOCK_BYTES = 4 << 20

LN_EPS = 1e-5
LRU_C = 8.0
N_GROUPS = 8
TOPK_GROUPS = 4
TOP_K = 8
ROUTED_SCALE = 2.5

MIXER_ROWS = 512
ROUTE_ROWS = 512
MOE_TILE = 256
MOE_CHUNKS = 4
SCATTER_UNROLL = 16
WEIGHT_BUFFERS = 2
ACC_VREGS = 32
SCAN_VREGS = 16

F32 = jnp.float32
BF16 = jnp.bfloat16


def _layer_norm(v, g, b):
    mu = jnp.mean(v, axis=-1, keepdims=True)
    d = v - mu
    var = jnp.mean(d * d, axis=-1, keepdims=True)
    return d * lax.rsqrt(var + LN_EPS) * g + b


def _round_up(n, m):
    return (n + m - 1) // m * m


def _lru_body(x_ref, cs_ref, h0_ref, win_ref, bin_ref, cw_ref, cb_ref, wa_ref, ba_ref, wx_ref, bx_ref,
              lam_ref, wout_ref, bout_ref, g1_ref, b1_ref,
              x1_ref, ncs_ref, hl_ref,
              ubuf, a_s, b_s, h_s, *, seq_start, alpha, scan_panels):
    nb, tc, d_model = x_ref.shape
    d_rnn = h0_ref.shape[1]
    heads = wa_ref.shape[0]
    hb = d_rnn // heads
    kw = cw_ref.shape[0]
    hist0 = SUBLANES - (kw - 1)
    rows = nb * tc
    ti = pl.program_id(1)

    @pl.when(ti == 0)
    def _init_state():
        ubuf[:, hist0:SUBLANES, :] = cs_ref[...]
        h_s[...] = h0_ref[...]

    x = x_ref[...].reshape(rows, d_model)
    proj = jnp.dot(x.astype(BF16), win_ref[...], preferred_element_type=F32) + bin_ref[...]
    gate = jax.nn.gelu(proj[:, :d_rnn])
    u = proj[:, d_rnn:]
    ubuf[:, SUBLANES:SUBLANES + tc, :] = u.reshape(nb, tc, d_rnn)
    xc3 = jnp.broadcast_to(cb_ref[...].reshape(1, 1, d_rnn), (nb, tc, d_rnn))
    for j in range(kw):
        xc3 = xc3 + cw_ref[j:j + 1, :].reshape(1, 1, d_rnn) * ubuf[:, hist0 + j:hist0 + j + tc, :]
    tail = ubuf[:, tc + hist0:tc + SUBLANES, :]
    ncs_ref[...] = tail
    ubuf[:, hist0:SUBLANES, :] = tail
    xc = xc3.reshape(rows, d_rnn)

    xcb = xc.astype(BF16)
    ra = jnp.concatenate([jnp.dot(xcb[:, h * hb:(h + 1) * hb], wa_ref[h], preferred_element_type=F32)
                          for h in range(heads)], axis=1)
    ia = jnp.concatenate([jnp.dot(xcb[:, h * hb:(h + 1) * hb], wx_ref[h], preferred_element_type=F32)
                          for h in range(heads)], axis=1)
    r = jax.nn.sigmoid(ra + ba_ref[...])
    gi = jax.nn.sigmoid(ia + bx_ref[...])
    lam = lam_ref[...]
    softplus_neg_lam = jnp.maximum(-lam, 0.0) + jnp.log1p(jnp.exp(-jnp.abs(lam)))
    log_a = (-LRU_C * softplus_neg_lam) * r
    a = jnp.exp(log_a)
    mult = jnp.sqrt(-jnp.tanh(log_a) * (a * a + 1.0))
    if seq_start:
        t_in_seq = lax.broadcasted_iota(jnp.int32, (rows, 1), 0) % tc
        mult = jnp.where((t_in_seq == 0) & (ti == 0), 1.0, mult)
    bterm = xc * gi * mult
    npan = d_rnn // LANES
    for p in range(npan):
        a_s[p] = a[:, p * LANES:(p + 1) * LANES]
        b_s[p] = bterm[:, p * LANES:(p + 1) * LANES]

    for p0 in range(0, npan, scan_panels):
        group = range(p0, min(p0 + scan_panels, npan))
        hs = [h_s[:, p * LANES:(p + 1) * LANES] for p in group]
        for t in range(tc):
            rs = pl.ds(t, nb, stride=tc)
            for k, p in enumerate(group):
                hs[k] = a_s[p, rs, :] * hs[k] + b_s[p, rs, :]
                b_s[p, rs, :] = hs[k]
        for k, p in enumerate(group):
            h_s[:, p * LANES:(p + 1) * LANES] = hs[k]
    hl_ref[...] = h_s[...]

    h_all = jnp.concatenate([b_s[p] for p in range(npan)], axis=1)
    y = jnp.dot((h_all * gate).astype(BF16), wout_ref[...], preferred_element_type=F32) + bout_ref[...]
    x1 = _layer_norm(alpha * x + y, g1_ref[...], b1_ref[...])
    x1_ref[...] = x1.reshape(nb, tc, d_model)


def _lru_mixer(x, conv_state, h0, win, b_in, conv_w, conv_b, wa, ba, wx, bx, lam, wout, bout, g1, b1, *,
               seq_start, alpha):
    n, s, d_model = x.shape
    d_rnn = h0.shape[1]
    kw = conv_w.shape[0]
    tc = min(s, MIXER_ROWS // SUBLANES)
    nb = min(n, MIXER_ROWS // tc)
    rows = nb * tc
    scan_panels = max(1, SCAN_VREGS * SUBLANES // nb)
    row = lambda v: v.reshape(1, -1)
    const2 = lambda i, t: (0, 0)
    const3 = lambda i, t: (0, 0, 0)
    vmem = (2 * 2 * rows * d_model * 4 + 2 * (win.size + wout.size + wa.size + wx.size) * 2
            + (nb * (tc + SUBLANES) + 2 * rows + nb) * d_rnn * 4 + 4 * rows * 2 * d_rnn * 4 + VMEM_HEADROOM_BYTES)
    return pl.pallas_call(
        functools.partial(_lru_body, seq_start=seq_start, alpha=alpha, scan_panels=scan_panels),
        grid=(n // nb, s // tc),
        in_specs=[
            pl.BlockSpec((nb, tc, d_model), lambda i, t: (i, t, 0)),
            pl.BlockSpec((nb, kw - 1, d_rnn), lambda i, t: (i, 0, 0)),
            pl.BlockSpec((nb, d_rnn), lambda i, t: (i, 0)),
            pl.BlockSpec(win.shape, const2), pl.BlockSpec((1, 2 * d_rnn), const2),
            pl.BlockSpec(conv_w.shape, const2), pl.BlockSpec((1, d_rnn), const2),
            pl.BlockSpec(wa.shape, const3), pl.BlockSpec((1, d_rnn), const2),
            pl.BlockSpec(wx.shape, const3), pl.BlockSpec((1, d_rnn), const2),
            pl.BlockSpec((1, d_rnn), const2),
            pl.BlockSpec(wout.shape, const2), pl.BlockSpec((1, d_model), const2),
            pl.BlockSpec((1, d_model), const2), pl.BlockSpec((1, d_model), const2),
        ],
        out_specs=[
            pl.BlockSpec((nb, tc, d_model), lambda i, t: (i, t, 0)),
            pl.BlockSpec((nb, kw - 1, d_rnn), lambda i, t: (i, 0, 0)),
            pl.BlockSpec((nb, d_rnn), lambda i, t: (i, 0)),
        ],
        out_shape=[jax.ShapeDtypeStruct(x.shape, F32),
                   jax.ShapeDtypeStruct((n, kw - 1, d_rnn), F32),
                   jax.ShapeDtypeStruct((n, d_rnn), F32)],
        scratch_shapes=[pltpu.VMEM((nb, tc + SUBLANES, d_rnn), F32),
                        pltpu.VMEM((d_rnn // LANES, rows, LANES), F32),
                        pltpu.VMEM((d_rnn // LANES, rows, LANES), F32),
                        pltpu.VMEM((nb, d_rnn), F32)],
        compiler_params=pltpu.CompilerParams(dimension_semantics=("arbitrary", "arbitrary"),
                                             vmem_limit_bytes=min(vmem, VMEM_LIMIT_CAP_BYTES)),
        name="lru_mixer",
    )(x, conv_state, h0, win, row(b_in), conv_w, row(conv_b), wa, row(ba), wx, row(bx), row(lam),
      wout, row(bout), row(g1), row(b1))


def _ccm_body(x_ref, cs_ref, win_ref, bin_ref, dw_ref, dwb_ref, lng_ref, lnb_ref, wout_ref, bout_ref,
              g1_ref, b1_ref, x1_ref, ncs_ref, gbuf, cbuf, *, alpha, conv_cols):
    nb, tc, d_model = x_ref.shape
    kw, d_conv = dw_ref.shape
    hist_rows = _round_up(kw - 1, SUBLANES)
    hist0 = hist_rows - (kw - 1)
    rows = nb * tc
    ti = pl.program_id(1)

    @pl.when(ti == 0)
    def _init_state():
        gbuf[:, hist0:hist_rows, :] = cs_ref[...]

    x = x_ref[...].reshape(rows, d_model)
    p = jnp.dot(x.astype(BF16), win_ref[...], preferred_element_type=F32) + bin_ref[...]
    glu = p[:, :d_conv] * jax.nn.sigmoid(p[:, d_conv:])
    gbuf[:, hist_rows:hist_rows + tc, :] = glu.reshape(nb, tc, d_conv)

    def conv_seq(n, carry):
        for c0 in range(0, d_conv, conv_cols):
            cols = slice(c0, c0 + conv_cols)
            acc = jnp.broadcast_to(dwb_ref[:, cols], (tc, conv_cols))
            for j in range(kw):
                acc = acc + dw_ref[j:j + 1, cols] * gbuf[n, hist0 + j:hist0 + j + tc, cols]
            cbuf[n, :, cols] = acc
        return carry
    lax.fori_loop(0, nb, conv_seq, 0)

    tail = gbuf[:, tc + hist0:tc + hist_rows, :]
    ncs_ref[...] = tail
    gbuf[:, hist0:hist_rows, :] = tail

    c = cbuf[...].reshape(rows, d_conv)
    hn = _layer_norm(c, lng_ref[...], lnb_ref[...])
    hdn = hn * jax.nn.sigmoid(hn)
    y = jnp.dot(hdn.astype(BF16), wout_ref[...], preferred_element_type=F32) + bout_ref[...]
    x1 = _layer_norm(alpha * x + y, g1_ref[...], b1_ref[...])
    x1_ref[...] = x1.reshape(nb, tc, d_model)


def _ccm_mixer(x, conv_state, win, b_in, dw_w, dw_b, ln_g, ln_b, wout, bout, g1, b1, *, alpha):
    n, s, d_model = x.shape
    kw, d_conv = dw_w.shape
    tc = min(s, MIXER_ROWS // SUBLANES)
    nb = min(n, MIXER_ROWS // tc)
    while nb > SUBLANES and nb * (kw - 1) * d_conv * 4 > CONV_STATE_BLOCK_BYTES:
        nb //= 2
    rows = nb * tc
    hist_rows = _round_up(kw - 1, SUBLANES)
    conv_cols = min(d_conv, max(LANES, (ACC_VREGS * SUBLANES * LANES // tc) // LANES * LANES))
    while d_conv % conv_cols:
        conv_cols -= LANES
    row = lambda v: v.reshape(1, -1)
    const2 = lambda i, t: (0, 0)
    vmem = (2 * 2 * rows * d_model * 4 + 2 * (win.size + wout.size) * 2
            + (nb * (tc + hist_rows) + rows) * d_conv * 4 + 4 * nb * (kw - 1) * d_conv * 4
            + 4 * rows * 2 * d_conv * 4 + VMEM_HEADROOM_BYTES)
    return pl.pallas_call(
        functools.partial(_ccm_body, alpha=alpha, conv_cols=conv_cols),
        grid=(n // nb, s // tc),
        in_specs=[
            pl.BlockSpec((nb, tc, d_model), lambda i, t: (i, t, 0)),
            pl.BlockSpec((nb, kw - 1, d_conv), lambda i, t: (i, 0, 0)),
            pl.BlockSpec(win.shape, const2), pl.BlockSpec((1, 2 * d_conv), const2),
            pl.BlockSpec(dw_w.shape, const2), pl.BlockSpec((1, d_conv), const2),
            pl.BlockSpec((1, d_conv), const2), pl.BlockSpec((1, d_conv), const2),
            pl.BlockSpec(wout.shape, const2), pl.BlockSpec((1, d_model), const2),
            pl.BlockSpec((1, d_model), const2), pl.BlockSpec((1, d_model), const2),
        ],
        out_specs=[
            pl.BlockSpec((nb, tc, d_model), lambda i, t: (i, t, 0)),
            pl.BlockSpec((nb, kw - 1, d_conv), lambda i, t: (i, 0, 0)),
        ],
        out_shape=[jax.ShapeDtypeStruct(x.shape, F32),
                   jax.ShapeDtypeStruct((n, kw - 1, d_conv), F32)],
        scratch_shapes=[pltpu.VMEM((nb, tc + hist_rows, d_conv), F32),
                        pltpu.VMEM((nb, tc, d_conv), F32)],
        compiler_params=pltpu.CompilerParams(dimension_semantics=("arbitrary", "arbitrary"),
                                             vmem_limit_bytes=min(vmem, VMEM_LIMIT_CAP_BYTES)),
        name="ccm_mixer",
    )(x, conv_state, win, row(b_in), dw_w, row(dw_b), row(ln_g), row(ln_b), wout, row(bout), row(g1), row(b1))


def _first_argmax(v, axis):
    m = jnp.max(v, axis=axis, keepdims=True)
    iota = lax.broadcasted_iota(jnp.int32, v.shape, axis)
    first = jnp.min(jnp.where(v == m, iota, v.shape[axis]), axis=axis, keepdims=True)
    return m, iota == first, first


def _route_body(x_ref, rwt_ref, rb_ref, sg_ref, su_ref, sd_ref, cnt_in_ref,
                resid_ref, idx_ref, w_ref, rank_ref, cnt_ref, tri_s, run_s, *, alpha, steps_per_chunk):
    x = x_ref[...]
    n_exp = rwt_ref.shape[0]
    tr = x.shape[0]
    step = pl.program_id(0)

    @pl.when(step == 0)
    def _build_prefix_matrix():
        r = lax.broadcasted_iota(jnp.int32, (tr, tr), 0)
        c = lax.broadcasted_iota(jnp.int32, (tr, tr), 1)
        tri_s[...] = jnp.where(r <= c, 1.0, 0.0).astype(BF16)

    @pl.when(step % steps_per_chunk == 0)
    def _start_chunk():
        run_s[...] = cnt_in_ref[0]

    logits = lax.dot_general(rwt_ref[...], x, (((1,), (1,)), ((), ())),
                             precision=lax.Precision.HIGHEST, preferred_element_type=F32)
    scores = jax.nn.sigmoid(logits)
    biased = scores + rb_ref[...]
    per_group = n_exp // N_GROUPS
    grp = biased.reshape(N_GROUPS, per_group, tr)
    m1, hit1, _ = _first_argmax(grp, 1)
    m2 = jnp.max(jnp.where(hit1, -jnp.inf, grp), axis=1, keepdims=True)
    gscore = (m1 + m2).reshape(N_GROUPS, tr)
    gsel = jnp.zeros((N_GROUPS, tr), F32)
    for _ in range(TOPK_GROUPS):
        _, hit, _ = _first_argmax(gscore, 0)
        gsel = jnp.where(hit, 1.0, gsel)
        gscore = jnp.where(hit, -jnp.inf, gscore)
    emask = jnp.broadcast_to(gsel.reshape(N_GROUPS, 1, tr), (N_GROUPS, per_group, tr)).reshape(n_exp, tr)
    masked = jnp.where(emask > 0.0, biased, -jnp.inf)
    idx_rows, w_rows, hits = [], [], []
    for _ in range(TOP_K):
        _, hit, first = _first_argmax(masked, 0)
        idx_rows.append(first)
        hits.append(hit)
        w_rows.append(jnp.sum(jnp.where(hit, scores, 0.0), axis=0, keepdims=True))
        masked = jnp.where(hit, -jnp.inf, masked)
    w = jnp.concatenate(w_rows, axis=0)
    idx_ref[...] = jnp.concatenate(idx_rows, axis=0)
    w_ref[...] = w / jnp.sum(w, axis=0, keepdims=True) * ROUTED_SCALE

    sel = jnp.zeros((n_exp, tr), F32)
    for hit in hits:
        sel = jnp.where(hit, 1.0, sel)
    before = jnp.dot(sel.astype(BF16), tri_s[...], preferred_element_type=F32) - sel + run_s[:, 0:1]
    rank_ref[...] = jnp.concatenate(
        [jnp.sum(jnp.where(hit, before, 0.0), axis=0, keepdims=True) for hit in hits], axis=0).astype(jnp.int32)
    run_s[...] = run_s[...] + jnp.sum(sel, axis=1, keepdims=True)
    cnt_ref[0] = run_s[...]

    xb = x.astype(BF16)
    sgate = jnp.dot(xb, sg_ref[...], preferred_element_type=F32)
    sup = jnp.dot(xb, su_ref[...], preferred_element_type=F32)
    hs = (sgate * jax.nn.sigmoid(sgate) * sup).astype(BF16)
    resid_ref[...] = alpha * x + jnp.dot(hs, sd_ref[...], preferred_element_type=F32)


def _route_shared(x, router_wt, router_bias, sg, su, sd, cnt_in, *, alpha):
    t, d_model = x.shape
    n_exp = router_wt.shape[0]
    rows_per_chunk = t // MOE_CHUNKS
    tr = min(rows_per_chunk, ROUTE_ROWS)
    steps_per_chunk = rows_per_chunk // tr
    const2 = lambda i: (0, 0)
    chunk_map = lambda i: (i // steps_per_chunk, 0, 0)
    vmem = 2 * 2 * tr * d_model * 4 + 2 * (router_wt.size * 4 + 3 * sg.size * 2) + VMEM_HEADROOM_BYTES
    return pl.pallas_call(
        functools.partial(_route_body, alpha=alpha, steps_per_chunk=steps_per_chunk),
        grid=(t // tr,),
        in_specs=[pl.BlockSpec((tr, d_model), lambda i: (i, 0)),
                  pl.BlockSpec(router_wt.shape, const2), pl.BlockSpec((n_exp, 1), const2),
                  pl.BlockSpec(sg.shape, const2), pl.BlockSpec(su.shape, const2), pl.BlockSpec(sd.shape, const2),
                  pl.BlockSpec((1, n_exp, LANES), chunk_map)],
        out_specs=[pl.BlockSpec((tr, d_model), lambda i: (i, 0)),
                   pl.BlockSpec((TOP_K, tr), lambda i: (0, i)),
                   pl.BlockSpec((TOP_K, tr), lambda i: (0, i)),
                   pl.BlockSpec((TOP_K, tr), lambda i: (0, i)),
                   pl.BlockSpec((1, n_exp, LANES), chunk_map)],
        out_shape=[jax.ShapeDtypeStruct((t, d_model), F32),
                   jax.ShapeDtypeStruct((TOP_K, t), jnp.int32),
                   jax.ShapeDtypeStruct((TOP_K, t), F32),
                   jax.ShapeDtypeStruct((TOP_K, t), jnp.int32),
                   jax.ShapeDtypeStruct((MOE_CHUNKS, n_exp, LANES), F32)],
        scratch_shapes=[pltpu.VMEM((tr, tr), BF16), pltpu.VMEM((n_exp, LANES), F32)],
        compiler_params=pltpu.CompilerParams(dimension_semantics=("arbitrary",),
                                             vmem_limit_bytes=min(vmem, VMEM_LIMIT_CAP_BYTES)),
        name="route_shared",
    )(x, router_wt, router_bias.reshape(n_exp, 1), sg, su, sd, cnt_in)


def _moe_body(seg_cnt_ref, seg_base_ref,
              wg_ref, wu_ref, wd_ref, g2_ref, b2_ref,
              tok_hbm, wts_hbm, xp_hbm, xs_hbm, rp_hbm, rs_hbm,
              op_hbm, os_hbm,
              x_s, y_s, xa_s, xb_s, oa_s, ob_s, tok_sm, wts_sm, sem_in, sem_out, sem_sm,
              *, tm, cp_rows, cs_rows, npan):
    chunk = cp_rows + cs_rows
    chunk_stride = chunk + SUBLANES
    half = tm // 2
    half_stride = half + SUBLANES
    n_exp = seg_cnt_ref.shape[0] // MOE_CHUNKS
    seg = pl.program_id(0)
    c = seg // n_exp
    chunk_first = seg % n_exp == 0
    chunk_last = seg % n_exp == n_exp - 1
    n_rows = seg_cnt_ref[seg]
    seg_row0 = seg_base_ref[seg]
    n_whole = jnp.maximum(n_rows - half + tm - 1, 0) // tm
    rows_after = n_rows - n_whole * tm

    def chunk_copies(hbm_p, hbm_s, vmem, sem, to_vmem):
        copies = []
        for p in range(npan):
            for hbm, nrows, row0 in ((hbm_p, cp_rows, 0), (hbm_s, cs_rows, cp_rows)):
                h = hbm.at[pl.ds(c * nrows, nrows), pl.ds(p * LANES, LANES)]
                v = vmem.at[pl.ds(p * chunk_stride + row0, nrows), :]
                copies.append(pltpu.make_async_copy(h, v, sem.at[0]) if to_vmem
                              else pltpu.make_async_copy(v, h, sem.at[0]))
        return copies

    @pl.when(chunk_first)
    def _load_chunk():
        copies = (chunk_copies(xp_hbm, xs_hbm, x_s, sem_in, True)
                  + chunk_copies(rp_hbm, rs_hbm, y_s, sem_in, True))
        for cp in copies:
            cp.start()
        zeros = jnp.zeros((SUBLANES, LANES), F32)
        for p in range(npan):
            x_s[pl.ds(p * chunk_stride + chunk, SUBLANES), :] = zeros
            y_s[pl.ds(p * chunk_stride + chunk, SUBLANES), :] = zeros
        list_rows = tok_sm.shape[0]
        lists = (pltpu.make_async_copy(tok_hbm.at[pl.ds(c * list_rows, list_rows)], tok_sm, sem_sm.at[0]),
                 pltpu.make_async_copy(wts_hbm.at[pl.ds(c * list_rows, list_rows)], wts_sm, sem_sm.at[1]))
        for cp in lists:
            cp.start()
        for cp in lists + tuple(copies):
            cp.wait()

    def gather_half(first, buf):
        for i in range(half):
            t = tok_sm[first + i]
            buf[pl.ds(i, npan, stride=half_stride), :] = x_s[pl.ds(t, npan, stride=chunk_stride), :]

    def ffn_half(src, dst):
        lhs = jnp.concatenate([src[pl.ds(p * half_stride, half), :] for p in range(npan)], axis=1).astype(BF16)
        g = jnp.dot(lhs, wg_ref[0, 0], preferred_element_type=F32)
        u = jnp.dot(lhs, wu_ref[0, 0], preferred_element_type=F32)
        hidden = (g * jax.nn.sigmoid(g) * u).astype(BF16)
        o = jnp.dot(hidden, wd_ref[0, 0], preferred_element_type=F32)
        for p in range(npan):
            dst[pl.ds(p * half_stride, half), :] = o[:, p * LANES:(p + 1) * LANES]

    def combine_half(first, buf):
        for i0 in range(0, half, SCATTER_UNROLL):
            toks, rows_new = [], []
            for i in range(i0, i0 + SCATTER_UNROLL):
                t = tok_sm[first + i]
                w = wts_sm[first + i]
                row = buf[pl.ds(i, npan, stride=half_stride), :]
                toks.append(t)
                rows_new.append(y_s[pl.ds(t, npan, stride=chunk_stride), :] + w * row)
            for t, new in zip(toks, rows_new):
                y_s[pl.ds(t, npan, stride=chunk_stride), :] = new

    def whole_tile(i, carry):
        first = seg_row0 + i * tm
        gather_half(first, xa_s)
        gather_half(first + half, xb_s)
        ffn_half(xa_s, oa_s)
        ffn_half(xb_s, ob_s)
        combine_half(first, oa_s)
        combine_half(first + half, ob_s)
        return carry
    lax.fori_loop(0, n_whole, whole_tile, 0)

    @pl.when(rows_after > 0)
    def _half_tile():
        first = seg_row0 + n_whole * tm
        gather_half(first, xa_s)
        ffn_half(xa_s, oa_s)
        combine_half(first, oa_s)

    @pl.when(chunk_last)
    def _finish_chunk():
        blk = MOE_TILE
        g2 = g2_ref[...]
        b2 = b2_ref[...]

        def ln_block(rb, carry):
            r0 = pl.multiple_of(rb * blk, blk)
            v = jnp.concatenate([y_s[pl.ds(p * chunk_stride + r0, blk), :] for p in range(npan)], axis=1)
            out = _layer_norm(v, g2, b2)
            for p in range(npan):
                y_s[pl.ds(p * chunk_stride + r0, blk), :] = out[:, p * LANES:(p + 1) * LANES]
            return carry
        lax.fori_loop(0, chunk // blk, ln_block, 0)
        copies = chunk_copies(op_hbm, os_hbm, y_s, sem_out, False)
        for cp in copies:
            cp.start()
        for cp in copies:
            cp.wait()


def _list_rows(chunk, top_k, n_exp):
    return (chunk * top_k) // MOE_TILE * MOE_TILE + n_exp * MOE_TILE


def _build_row_lists(dest, wsel, *, chunk, top_k, n_exp):
    n_pairs = top_k * chunk
    list_rows = _list_rows(chunk, top_k, n_exp)
    n_jobs = 2 * MOE_CHUNKS
    assert n_jobs <= SC_CORES * SC_SUBCORES and n_pairs % SC_LANES == 0 and list_rows % SC_LANES == 0
    tokens = jnp.tile(jnp.arange(chunk, dtype=jnp.int32), MOE_CHUNKS * top_k)
    vals = jnp.concatenate([tokens, lax.bitcast_convert_type(wsel, jnp.int32)])
    pad = jnp.concatenate([jnp.full((SC_LANES,), chunk, jnp.int32), jnp.zeros((SC_LANES,), jnp.int32)])
    mesh = plsc.VectorSubcoreMesh(core_axis_name="c", subcore_axis_name="s",
                                  num_cores=SC_CORES, num_subcores=SC_SUBCORES)

    def body(dest_hbm, vals_hbm, pad_hbm, out_hbm, dest_v, vals_v, list_v, pad_v):
        job = lax.axis_index("s") * SC_CORES + lax.axis_index("c")

        @pl.when(job < n_jobs)
        def _scatter_one_list():
            pltpu.sync_copy(dest_hbm.at[pl.ds((job % MOE_CHUNKS) * n_pairs, n_pairs)], dest_v)
            pltpu.sync_copy(vals_hbm.at[pl.ds(job * n_pairs, n_pairs)], vals_v)
            pltpu.sync_copy(pad_hbm.at[pl.ds((job // MOE_CHUNKS) * SC_LANES, SC_LANES)], pad_v)
            pad_vec = pad_v[...]

            @pl.loop(0, list_rows // SC_LANES)
            def _fill(i):
                list_v[pl.ds(i * SC_LANES, SC_LANES)] = pad_vec

            @pl.loop(0, n_pairs // SC_LANES)
            def _scatter(i):
                lanes = pl.ds(i * SC_LANES, SC_LANES)
                plsc.store_scatter(list_v, [dest_v[lanes]], vals_v[lanes])

            pltpu.sync_copy(list_v, out_hbm.at[pl.ds(job * list_rows, list_rows)])

    lists = pl.kernel(
        body, mesh=mesh,
        out_type=jax.ShapeDtypeStruct((n_jobs * list_rows,), jnp.int32),
        scratch_types=[pltpu.VMEM((n_pairs,), jnp.int32), pltpu.VMEM((n_pairs,), jnp.int32),
                       pltpu.VMEM((list_rows,), jnp.int32), pltpu.VMEM((SC_LANES,), jnp.int32)],
        compiler_params=pltpu.CompilerParams(needs_layout_passes=False),
        name="build_row_lists",
    )(dest, vals, pad)
    half = MOE_CHUNKS * list_rows
    return lists[:half], lax.bitcast_convert_type(lists[half:], F32)


def _moe_routed(seg_cnt, seg_base, tok_list, wts_list, wg, wu, wd, g2, b2, xp, xs, rp, rs, *, layer):
    _, n_exp, d_model, d_ff = wg.shape
    tm = MOE_TILE
    npan = d_model // LANES
    assert npan == SUBLANES, "one token row must fill exactly one (8, 128) register"
    cp_rows, cs_rows = xp.shape[0] // MOE_CHUNKS, xs.shape[0] // MOE_CHUNKS
    chunk = cp_rows + cs_rows
    assert chunk % MOE_TILE == 0 and cs_rows % SUBLANES == 0
    chunk_stride = chunk + SUBLANES
    half_stride = tm // 2 + SUBLANES
    list_rows = tok_list.shape[0] // MOE_CHUNKS
    any_spec = pl.BlockSpec(memory_space=pl.ANY)
    expert_map = lambda s, *_: (layer, s % n_exp, 0, 0)
    const2 = lambda s, *_: (0, 0)
    expert_spec = lambda shape: pl.BlockSpec((1, 1) + shape, expert_map)
    vmem = (2 * npan * chunk_stride * LANES * 4 + 4 * npan * half_stride * LANES * 4
            + WEIGHT_BUFFERS * 3 * d_model * d_ff * wg.dtype.itemsize + VMEM_HEADROOM_BYTES)
    return pl.pallas_call(
        functools.partial(_moe_body, tm=tm, cp_rows=cp_rows, cs_rows=cs_rows, npan=npan),
        grid_spec=pltpu.PrefetchScalarGridSpec(
            num_scalar_prefetch=2,
            grid=(MOE_CHUNKS * n_exp,),
            in_specs=[
                expert_spec((d_model, d_ff)),
                expert_spec((d_model, d_ff)),
                expert_spec((d_ff, d_model)),
                pl.BlockSpec((1, d_model), const2),
                pl.BlockSpec((1, d_model), const2),
                any_spec, any_spec, any_spec, any_spec, any_spec, any_spec,
            ],
            out_specs=[any_spec, any_spec],
            scratch_shapes=[
                pltpu.VMEM((npan * chunk_stride, LANES), F32),
                pltpu.VMEM((npan * chunk_stride, LANES), F32),
                pltpu.VMEM((npan * half_stride, LANES), F32),
                pltpu.VMEM((npan * half_stride, LANES), F32),
                pltpu.VMEM((npan * half_stride, LANES), F32),
                pltpu.VMEM((npan * half_stride, LANES), F32),
                pltpu.SMEM((list_rows,), jnp.int32),
                pltpu.SMEM((list_rows,), F32),
                pltpu.SemaphoreType.DMA((1,)),
                pltpu.SemaphoreType.DMA((1,)),
                pltpu.SemaphoreType.DMA((2,)),
            ]),
        out_shape=[jax.ShapeDtypeStruct(xp.shape, F32), jax.ShapeDtypeStruct(xs.shape, F32)],
        compiler_params=pltpu.CompilerParams(dimension_semantics=("arbitrary",),
                                             vmem_limit_bytes=min(vmem, VMEM_LIMIT_CAP_BYTES)),
        name="moe_routed",
    )(seg_cnt, seg_base, wg, wu, wd, g2.reshape(1, d_model), b2.reshape(1, d_model),
      tok_list, wts_list, xp, xs, rp, rs)


def _plan_tiles(cnt, idx_p, w_p, rank_p, idx_s, w_s, rank_s):
    k, tp = idx_p.shape
    ts = idx_s.shape[1]
    n_chunks, n_exp = cnt.shape
    tm = MOE_TILE
    cp, cs = tp // n_chunks, ts // n_chunks
    experts = jnp.arange(n_exp, dtype=jnp.int32)

    seg_rows = (cnt + tm - 1) // tm * tm
    seg_base = jnp.sum(jnp.where(experts[None, :] < experts[:, None], seg_rows[:, None, :], 0), axis=-1)

    def rows_of(idx, rank, per_chunk):
        idx3 = idx.reshape(k, n_chunks, per_chunk)
        base = jnp.sum(jnp.where(idx3[..., None] == experts, seg_base[None, :, None, :], 0), axis=-1)
        return jnp.transpose(base + rank.reshape(k, n_chunks, per_chunk), (1, 0, 2))
    by_chunk = lambda a, per_chunk: jnp.transpose(a.reshape(k, n_chunks, per_chunk), (1, 0, 2))
    dest = jnp.concatenate([rows_of(idx_p, rank_p, cp), rows_of(idx_s, rank_s, cs)], axis=2).reshape(-1)
    wsel = jnp.concatenate([by_chunk(w_p, cp), by_chunk(w_s, cs)], axis=2).reshape(-1)
    return cnt.reshape(-1), seg_base.reshape(-1), dest, wsel


def kernel(x_prompt, x_sample, state_lru_conv, state_lru_h, state_ccm_conv, lru_w_in, lru_b_in, lru_conv_w, lru_conv_b, lru_w_a, lru_b_a, lru_w_x, lru_b_x, lru_lambda, lru_w_out, lru_b_out, ccm_w_in, ccm_b_in, ccm_dw_w, ccm_dw_b, ccm_ln_g, ccm_ln_b, ccm_w_out, ccm_b_out, ln1_g, ln1_b, ln2_g, ln2_b, router_w, router_bias, exp_w_gate, exp_w_up, exp_w_down, sh_w_gate, sh_w_up, sh_w_down):
    depth = ln1_g.shape[0]
    alpha = (2 * depth) ** 0.25
    n_exp = router_w.shape[2]
    bp, sp, d_model = x_prompt.shape
    bs, ss, _ = x_sample.shape
    kw_lru = lru_conv_w.shape[1]
    kw_ccm = ccm_dw_w.shape[1]
    d_rnn = lru_conv_w.shape[2]
    d_conv = ccm_dw_w.shape[2]
    bf = lambda a: a.astype(BF16)

    zero_lru_conv = jnp.zeros((bp, kw_lru - 1, d_rnn), F32)
    zero_lru_h = jnp.zeros((bp, d_rnn), F32)
    zero_ccm_conv = jnp.zeros((bp, kw_ccm - 1, d_conv), F32)

    exp_gate_bf, exp_up_bf, exp_down_bf = bf(exp_w_gate), bf(exp_w_up), bf(exp_w_down)

    xp, xs = x_prompt, x_sample
    lru_conv_p, lru_h_p, ccm_conv_p = [], [], []
    lru_conv_s, lru_h_s, ccm_conv_s = [], [], []
    for layer in range(depth):
        j = layer // 2
        if layer % 2 == 0:
            weights = (bf(lru_w_in[j]), lru_b_in[j], lru_conv_w[j], lru_conv_b[j], bf(lru_w_a[j]), lru_b_a[j],
                       bf(lru_w_x[j]), lru_b_x[j], lru_lambda[j], bf(lru_w_out[j]), lru_b_out[j],
                       ln1_g[layer], ln1_b[layer])
            xp, cb, hl = _lru_mixer(xp, zero_lru_conv, zero_lru_h, *weights, seq_start=True, alpha=alpha)
            lru_conv_p.append(cb)
            lru_h_p.append(hl)
            xs, cb, hl = _lru_mixer(xs, state_lru_conv[j], state_lru_h[j], *weights, seq_start=False, alpha=alpha)
            lru_conv_s.append(cb)
            lru_h_s.append(hl)
        else:
            weights = (bf(ccm_w_in[j]), ccm_b_in[j], ccm_dw_w[j], ccm_dw_b[j], ccm_ln_g[j], ccm_ln_b[j],
                       bf(ccm_w_out[j]), ccm_b_out[j], ln1_g[layer], ln1_b[layer])
            xp, cb = _ccm_mixer(xp, zero_ccm_conv, *weights, alpha=alpha)
            ccm_conv_p.append(cb)
            xs, cb = _ccm_mixer(xs, state_ccm_conv[j], *weights, alpha=alpha)
            ccm_conv_s.append(cb)

        shared = (router_w[layer].T, router_bias[layer], bf(sh_w_gate[layer]), bf(sh_w_up[layer]),
                  bf(sh_w_down[layer]))
        xp2 = xp.reshape(bp * sp, d_model)
        xs2 = xs.reshape(bs * ss, d_model)
        no_tokens = jnp.zeros((MOE_CHUNKS, n_exp, LANES), F32)
        resid_p, idx_p, w_p, rank_p, cnt_p = _route_shared(xp2, *shared, no_tokens, alpha=alpha)
        resid_s, idx_s, w_s, rank_s, cnt = _route_shared(xs2, *shared, cnt_p, alpha=alpha)
        seg_cnt, seg_base, dest, wsel = _plan_tiles(cnt[:, :, 0].astype(jnp.int32), idx_p, w_p, rank_p,
                                                    idx_s, w_s, rank_s)
        tok_list, wts_list = _build_row_lists(dest, wsel, chunk=(xp2.shape[0] + xs2.shape[0]) // MOE_CHUNKS,
                                              top_k=idx_p.shape[0], n_exp=n_exp)
        xp2, xs2 = _moe_routed(seg_cnt, seg_base, tok_list, wts_list, exp_gate_bf, exp_up_bf, exp_down_bf,
                               ln2_g[layer], ln2_b[layer], xp2, xs2, resid_p, resid_s, layer=layer)
        xp = xp2.reshape(bp, sp, d_model)
        xs = xs2.reshape(bs, ss, d_model)

    return (xp, xs, jnp.stack(lru_conv_p), jnp.stack(lru_h_p), jnp.stack(ccm_conv_p),
            jnp.stack(lru_conv_s), jnp.stack(lru_h_s), jnp.stack(ccm_conv_s))
```

```python
import functools

import jax
import jax.numpy as jnp
from jax import lax
from jax.experimental import pallas as pl
from jax.experimental.pallas import tpu as pltpu
from jax.experimental.pallas import tpu_sc as plsc

LANES = 128
SUBLANES = 8
SC_CORES = 2
SC_SUBCORES = 16
SC_LANES = 16
VMEM_HEADROOM_BYTES = 8 << 20
VMEM_LIMIT_CAP_BYTES = 56 << 20
CONV_STATE_BLOCK_BYTES = 4 << 20

LN_EPS = 1e-5
LRU_C = 8.0
N_GROUPS = 8
TOPK_GROUPS = 4
TOP_K = 8
ROUTED_SCALE = 2.5

MIXER_ROWS = 512
ROUTE_ROWS = 512
MOE_TILE = 256
MOE_CHUNKS = 4
SCATTER_UNROLL = 8
WEIGHT_BUFFERS = 2
ACC_VREGS = 16
SCAN_VREGS = 16

F32 = jnp.float32
BF16 = jnp.bfloat16


def _layer_norm(v, g, b):
    mu = jnp.mean(v, axis=-1, keepdims=True)
    d = v - mu
    var = jnp.mean(d * d, axis=-1, keepdims=True)
    return d * lax.rsqrt(var + LN_EPS) * g + b


def _round_up(n, m):
    return (n + m - 1) // m * m


def _lru_body(x_ref, cs_ref, h0_ref, win_ref, bin_ref, cw_ref, cb_ref, wa_ref, ba_ref, wx_ref, bx_ref,
              lam_ref, wout_ref, bout_ref, g1_ref, b1_ref,
              x1_ref, ncs_ref, hl_ref,
              ubuf, a_s, b_s, h_s, *, seq_start, alpha, scan_panels):
    nb, tc, d_model = x_ref.shape
    d_rnn = h0_ref.shape[1]
    heads = wa_ref.shape[0]
    hb = d_rnn // heads
    kw = cw_ref.shape[0]
    hist0 = SUBLANES - (kw - 1)
    rows = nb * tc
    ti = pl.program_id(1)

    @pl.when(ti == 0)
    def _init_state():
        ubuf[:, hist0:SUBLANES, :] = cs_ref[...]
        h_s[...] = h0_ref[...]

    x = x_ref[...].reshape(rows, d_model)
    proj = jnp.dot(x.astype(BF16), win_ref[...], preferred_element_type=F32) + bin_ref[...]
    gate = jax.nn.gelu(proj[:, :d_rnn])
    u = proj[:, d_rnn:]
    ubuf[:, SUBLANES:SUBLANES + tc, :] = u.reshape(nb, tc, d_rnn)
    xc3 = jnp.broadcast_to(cb_ref[...].reshape(1, 1, d_rnn), (nb, tc, d_rnn))
    for j in range(kw):
        xc3 = xc3 + cw_ref[j:j + 1, :].reshape(1, 1, d_rnn) * ubuf[:, hist0 + j:hist0 + j + tc, :]
    tail = ubuf[:, tc + hist0:tc + SUBLANES, :]
    ncs_ref[...] = tail
    ubuf[:, hist0:SUBLANES, :] = tail
    xc = xc3.reshape(rows, d_rnn)

    xcb = xc.astype(BF16)
    ra = jnp.concatenate([jnp.dot(xcb[:, h * hb:(h + 1) * hb], wa_ref[h], preferred_element_type=F32)
                          for h in range(heads)], axis=1)
    ia = jnp.concatenate([jnp.dot(xcb[:, h * hb:(h + 1) * hb], wx_ref[h], preferred_element_type=F32)
                          for h in range(heads)], axis=1)
    r = jax.nn.sigmoid(ra + ba_ref[...])
    gi = jax.nn.sigmoid(ia + bx_ref[...])
    lam = lam_ref[...]
    softplus_neg_lam = jnp.maximum(-lam, 0.0) + jnp.log1p(jnp.exp(-jnp.abs(lam)))
    log_a = (-LRU_C * softplus_neg_lam) * r
    a = jnp.exp(log_a)
    mult = jnp.sqrt(-jnp.tanh(log_a) * (a * a + 1.0))
    if seq_start:
        t_in_seq = lax.broadcasted_iota(jnp.int32, (rows, 1), 0) % tc
        mult = jnp.where((t_in_seq == 0) & (ti == 0), 1.0, mult)
    bterm = xc * gi * mult
    npan = d_rnn // LANES
    for p in range(npan):
        a_s[p] = a[:, p * LANES:(p + 1) * LANES]
        b_s[p] = bterm[:, p * LANES:(p + 1) * LANES]

    for p0 in range(0, npan, scan_panels):
        group = range(p0, min(p0 + scan_panels, npan))
        hs = [h_s[:, p * LANES:(p + 1) * LANES] for p in group]
        for t in range(tc):
            rs = pl.ds(t, nb, stride=tc)
            for k, p in enumerate(group):
                hs[k] = a_s[p, rs, :] * hs[k] + b_s[p, rs, :]
                b_s[p, rs, :] = hs[k]
        for k, p in enumerate(group):
            h_s[:, p * LANES:(p + 1) * LANES] = hs[k]
    hl_ref[...] = h_s[...]

    h_all = jnp.concatenate([b_s[p] for p in range(npan)], axis=1)
    y = jnp.dot((h_all * gate).astype(BF16), wout_ref[...], preferred_element_type=F32) + bout_ref[...]
    x1 = _layer_norm(alpha * x + y, g1_ref[...], b1_ref[...])
    x1_ref[...] = x1.reshape(nb, tc, d_model)


def _lru_mixer(x, conv_state, h0, win, b_in, conv_w, conv_b, wa, ba, wx, bx, lam, wout, bout, g1, b1, *,
               seq_start, alpha):
    n, s, d_model = x.shape
    d_rnn = h0.shape[1]
    kw = conv_w.shape[0]
    tc = min(s, MIXER_ROWS // SUBLANES)
    nb = min(n, MIXER_ROWS // tc)
    rows = nb * tc
    scan_panels = max(1, SCAN_VREGS * SUBLANES // nb)
    row = lambda v: v.reshape(1, -1)
    const2 = lambda i, t: (0, 0)
    const3 = lambda i, t: (0, 0, 0)
    vmem = (2 * 2 * rows * d_model * 4 + 2 * (win.size + wout.size + wa.size + wx.size) * 2
            + (nb * (tc + SUBLANES) + 2 * rows + nb) * d_rnn * 4 + 4 * rows * 2 * d_rnn * 4 + VMEM_HEADROOM_BYTES)
    return pl.pallas_call(
        functools.partial(_lru_body, seq_start=seq_start, alpha=alpha, scan_panels=scan_panels),
        grid=(n // nb, s // tc),
        in_specs=[
            pl.BlockSpec((nb, tc, d_model), lambda i, t: (i, t, 0)),
            pl.BlockSpec((nb, kw - 1, d_rnn), lambda i, t: (i, 0, 0)),
            pl.BlockSpec((nb, d_rnn), lambda i, t: (i, 0)),
            pl.BlockSpec(win.shape, const2), pl.BlockSpec((1, 2 * d_rnn), const2),
            pl.BlockSpec(conv_w.shape, const2), pl.BlockSpec((1, d_rnn), const2),
            pl.BlockSpec(wa.shape, const3), pl.BlockSpec((1, d_rnn), const2),
            pl.BlockSpec(wx.shape, const3), pl.BlockSpec((1, d_rnn), const2),
            pl.BlockSpec((1, d_rnn), const2),
            pl.BlockSpec(wout.shape, const2), pl.BlockSpec((1, d_model), const2),
            pl.BlockSpec((1, d_model), const2), pl.BlockSpec((1, d_model), const2),
        ],
        out_specs=[
            pl.BlockSpec((nb, tc, d_model), lambda i, t: (i, t, 0)),
            pl.BlockSpec((nb, kw - 1, d_rnn), lambda i, t: (i, 0, 0)),
            pl.BlockSpec((nb, d_rnn), lambda i, t: (i, 0)),
        ],
        out_shape=[jax.ShapeDtypeStruct(x.shape, F32),
                   jax.ShapeDtypeStruct((n, kw - 1, d_rnn), F32),
                   jax.ShapeDtypeStruct((n, d_rnn), F32)],
        scratch_shapes=[pltpu.VMEM((nb, tc + SUBLANES, d_rnn), F32),
                        pltpu.VMEM((d_rnn // LANES, rows, LANES), F32),
                        pltpu.VMEM((d_rnn // LANES, rows, LANES), F32),
                        pltpu.VMEM((nb, d_rnn), F32)],
        compiler_params=pltpu.CompilerParams(dimension_semantics=("arbitrary", "arbitrary"),
                                             vmem_limit_bytes=min(vmem, VMEM_LIMIT_CAP_BYTES)),
        name="lru_mixer",
    )(x, conv_state, h0, win, row(b_in), conv_w, row(conv_b), wa, row(ba), wx, row(bx), row(lam),
      wout, row(bout), row(g1), row(b1))


def _ccm_body(x_ref, cs_ref, win_ref, bin_ref, dw_ref, dwb_ref, lng_ref, lnb_ref, wout_ref, bout_ref,
              g1_ref, b1_ref, x1_ref, ncs_ref, gbuf, cbuf, *, alpha, conv_cols):
    nb, tc, d_model = x_ref.shape
    kw, d_conv = dw_ref.shape
    hist_rows = _round_up(kw - 1, SUBLANES)
    hist0 = hist_rows - (kw - 1)
    rows = nb * tc
    ti = pl.program_id(1)

    @pl.when(ti == 0)
    def _init_state():
        gbuf[:, hist0:hist_rows, :] = cs_ref[...]

    x = x_ref[...].reshape(rows, d_model)
    p = jnp.dot(x.astype(BF16), win_ref[...], preferred_element_type=F32) + bin_ref[...]
    glu = p[:, :d_conv] * jax.nn.sigmoid(p[:, d_conv:])
    gbuf[:, hist_rows:hist_rows + tc, :] = glu.reshape(nb, tc, d_conv)

    def conv_seq(n, carry):
        for c0 in range(0, d_conv, conv_cols):
            cols = slice(c0, c0 + conv_cols)
            acc = jnp.broadcast_to(dwb_ref[:, cols], (tc, conv_cols))
            for shift in range(SUBLANES):
                taps = [j for j in range(kw) if (hist0 + j) % SUBLANES == shift]
                if not taps:
                    continue
                rows = tc if shift == 0 else tc + SUBLANES
                part = jnp.zeros((rows, conv_cols), F32)
                for j in taps:
                    start = hist0 + j - shift
                    part = part + dw_ref[j:j + 1, cols] * gbuf[n, start:start + rows, cols]
                acc = acc + part[shift:shift + tc]
            cbuf[n, :, cols] = acc
        return carry
    lax.fori_loop(0, nb, conv_seq, 0)

    tail = gbuf[:, tc + hist0:tc + hist_rows, :]
    ncs_ref[...] = tail
    gbuf[:, hist0:hist_rows, :] = tail

    c = cbuf[...].reshape(rows, d_conv)
    hn = _layer_norm(c, lng_ref[...], lnb_ref[...])
    hdn = hn * jax.nn.sigmoid(hn)
    y = jnp.dot(hdn.astype(BF16), wout_ref[...], preferred_element_type=F32) + bout_ref[...]
    x1 = _layer_norm(alpha * x + y, g1_ref[...], b1_ref[...])
    x1_ref[...] = x1.reshape(nb, tc, d_model)


def _ccm_mixer(x, conv_state, win, b_in, dw_w, dw_b, ln_g, ln_b, wout, bout, g1, b1, *, alpha):
    n, s, d_model = x.shape
    kw, d_conv = dw_w.shape
    tc = min(s, MIXER_ROWS // SUBLANES)
    nb = min(n, MIXER_ROWS // tc)
    while nb > SUBLANES and nb * (kw - 1) * d_conv * 4 > CONV_STATE_BLOCK_BYTES:
        nb //= 2
    rows = nb * tc
    hist_rows = _round_up(kw - 1, SUBLANES)
    conv_cols = min(d_conv, max(LANES, (ACC_VREGS * SUBLANES * LANES // tc) // LANES * LANES))
    while d_conv % conv_cols:
        conv_cols -= LANES
    row = lambda v: v.reshape(1, -1)
    const2 = lambda i, t: (0, 0)
    vmem = (2 * 2 * rows * d_model * 4 + 2 * (win.size + wout.size) * 2
            + (nb * (tc + hist_rows) + rows) * d_conv * 4 + 4 * nb * (kw - 1) * d_conv * 4
            + 4 * rows * 2 * d_conv * 4 + VMEM_HEADROOM_BYTES)
    return pl.pallas_call(
        functools.partial(_ccm_body, alpha=alpha, conv_cols=conv_cols),
        grid=(n // nb, s // tc),
        in_specs=[
            pl.BlockSpec((nb, tc, d_model), lambda i, t: (i, t, 0)),
            pl.BlockSpec((nb, kw - 1, d_conv), lambda i, t: (i, 0, 0)),
            pl.BlockSpec(win.shape, const2), pl.BlockSpec((1, 2 * d_conv), const2),
            pl.BlockSpec(dw_w.shape, const2), pl.BlockSpec((1, d_conv), const2),
            pl.BlockSpec((1, d_conv), const2), pl.BlockSpec((1, d_conv), const2),
            pl.BlockSpec(wout.shape, const2), pl.BlockSpec((1, d_model), const2),
            pl.BlockSpec((1, d_model), const2), pl.BlockSpec((1, d_model), const2),
        ],
        out_specs=[
            pl.BlockSpec((nb, tc, d_model), lambda i, t: (i, t, 0)),
            pl.BlockSpec((nb, kw - 1, d_conv), lambda i, t: (i, 0, 0)),
        ],
        out_shape=[jax.ShapeDtypeStruct(x.shape, F32),
                   jax.ShapeDtypeStruct((n, kw - 1, d_conv), F32)],
        scratch_shapes=[pltpu.VMEM((nb, tc + hist_rows, d_conv), F32),
                        pltpu.VMEM((nb, tc, d_conv), F32)],
        compiler_params=pltpu.CompilerParams(dimension_semantics=("arbitrary", "arbitrary"),
                                             vmem_limit_bytes=min(vmem, VMEM_LIMIT_CAP_BYTES)),
        name="ccm_mixer",
    )(x, conv_state, win, row(b_in), dw_w, row(dw_b), row(ln_g), row(ln_b), wout, row(bout), row(g1), row(b1))


def _first_argmax(v, axis):
    m = jnp.max(v, axis=axis, keepdims=True)
    iota = lax.broadcasted_iota(jnp.int32, v.shape, axis)
    first = jnp.min(jnp.where(v == m, iota, v.shape[axis]), axis=axis, keepdims=True)
    return m, iota == first, first


def _route_body(x_ref, rwt_ref, rb_ref, sg_ref, su_ref, sd_ref, cnt_in_ref,
                resid_ref, idx_ref, w_ref, rank_ref, cnt_ref, tri_s, run_s, *, alpha, steps_per_chunk):
    x = x_ref[...]
    n_exp = rwt_ref.shape[0]
    tr = x.shape[0]
    step = pl.program_id(0)

    @pl.when(step == 0)
    def _build_prefix_matrix():
        r = lax.broadcasted_iota(jnp.int32, (tr, tr), 0)
        c = lax.broadcasted_iota(jnp.int32, (tr, tr), 1)
        tri_s[...] = jnp.where(r <= c, 1.0, 0.0).astype(BF16)

    @pl.when(step % steps_per_chunk == 0)
    def _start_chunk():
        run_s[...] = cnt_in_ref[0]

    logits = lax.dot_general(rwt_ref[...], x, (((1,), (1,)), ((), ())),
                             precision=lax.Precision.HIGHEST, preferred_element_type=F32)
    scores = jax.nn.sigmoid(logits)
    biased = scores + rb_ref[...]
    per_group = n_exp // N_GROUPS
    grp = biased.reshape(N_GROUPS, per_group, tr)
    m1, hit1, _ = _first_argmax(grp, 1)
    m2 = jnp.max(jnp.where(hit1, -jnp.inf, grp), axis=1, keepdims=True)
    gscore = (m1 + m2).reshape(N_GROUPS, tr)
    gsel = jnp.zeros((N_GROUPS, tr), F32)
    for _ in range(TOPK_GROUPS):
        _, hit, _ = _first_argmax(gscore, 0)
        gsel = jnp.where(hit, 1.0, gsel)
        gscore = jnp.where(hit, -jnp.inf, gscore)
    emask = jnp.broadcast_to(gsel.reshape(N_GROUPS, 1, tr), (N_GROUPS, per_group, tr)).reshape(n_exp, tr)
    masked = jnp.where(emask > 0.0, biased, -jnp.inf)
    idx_rows, w_rows, hits = [], [], []
    for _ in range(TOP_K):
        _, hit, first = _first_argmax(masked, 0)
        idx_rows.append(first)
        hits.append(hit)
        w_rows.append(jnp.sum(jnp.where(hit, scores, 0.0), axis=0, keepdims=True))
        masked = jnp.where(hit, -jnp.inf, masked)
    w = jnp.concatenate(w_rows, axis=0)
    idx_ref[...] = jnp.concatenate(idx_rows, axis=0)
    w_ref[...] = w / jnp.sum(w, axis=0, keepdims=True) * ROUTED_SCALE

    sel = jnp.zeros((n_exp, tr), F32)
    for hit in hits:
        sel = jnp.where(hit, 1.0, sel)
    before = jnp.dot(sel.astype(BF16), tri_s[...], preferred_element_type=F32) - sel + run_s[:, 0:1]
    rank_ref[...] = jnp.concatenate(
        [jnp.sum(jnp.where(hit, before, 0.0), axis=0, keepdims=True) for hit in hits], axis=0).astype(jnp.int32)
    run_s[...] = run_s[...] + jnp.sum(sel, axis=1, keepdims=True)
    cnt_ref[0] = run_s[...]

    xb = x.astype(BF16)
    sgate = jnp.dot(xb, sg_ref[...], preferred_element_type=F32)
    sup = jnp.dot(xb, su_ref[...], preferred_element_type=F32)
    hs = (sgate * jax.nn.sigmoid(sgate) * sup).astype(BF16)
    resid_ref[...] = alpha * x + jnp.dot(hs, sd_ref[...], preferred_element_type=F32)


def _route_shared(x, router_wt, router_bias, sg, su, sd, cnt_in, *, alpha):
    t, d_model = x.shape
    n_exp = router_wt.shape[0]
    rows_per_chunk = t // MOE_CHUNKS
    tr = min(rows_per_chunk, ROUTE_ROWS)
    steps_per_chunk = rows_per_chunk // tr
    const2 = lambda i: (0, 0)
    chunk_map = lambda i: (i // steps_per_chunk, 0, 0)
    vmem = 2 * 2 * tr * d_model * 4 + 2 * (router_wt.size * 4 + 3 * sg.size * 2) + VMEM_HEADROOM_BYTES
    return pl.pallas_call(
        functools.partial(_route_body, alpha=alpha, steps_per_chunk=steps_per_chunk),
        grid=(t // tr,),
        in_specs=[pl.BlockSpec((tr, d_model), lambda i: (i, 0)),
                  pl.BlockSpec(router_wt.shape, const2), pl.BlockSpec((n_exp, 1), const2),
                  pl.BlockSpec(sg.shape, const2), pl.BlockSpec(su.shape, const2), pl.BlockSpec(sd.shape, const2),
                  pl.BlockSpec((1, n_exp, LANES), chunk_map)],
        out_specs=[pl.BlockSpec((tr, d_model), lambda i: (i, 0)),
                   pl.BlockSpec((TOP_K, tr), lambda i: (0, i)),
                   pl.BlockSpec((TOP_K, tr), lambda i: (0, i)),
                   pl.BlockSpec((TOP_K, tr), lambda i: (0, i)),
                   pl.BlockSpec((1, n_exp, LANES), chunk_map)],
        out_shape=[jax.ShapeDtypeStruct((t, d_model), F32),
                   jax.ShapeDtypeStruct((TOP_K, t), jnp.int32),
                   jax.ShapeDtypeStruct((TOP_K, t), F32),
                   jax.ShapeDtypeStruct((TOP_K, t), jnp.int32),
                   jax.ShapeDtypeStruct((MOE_CHUNKS, n_exp, LANES), F32)],
        scratch_shapes=[pltpu.VMEM((tr, tr), BF16), pltpu.VMEM((n_exp, LANES), F32)],
        compiler_params=pltpu.CompilerParams(dimension_semantics=("arbitrary",),
                                             vmem_limit_bytes=min(vmem, VMEM_LIMIT_CAP_BYTES)),
        name="route_shared",
    )(x, router_wt, router_bias.reshape(n_exp, 1), sg, su, sd, cnt_in)


def _moe_body(seg_cnt_ref, seg_base_ref,
              wg_ref, wu_ref, wd_ref, g2_ref, b2_ref,
              tok_hbm, wts_hbm, xp_hbm, xs_hbm, rp_hbm, rs_hbm,
              op_hbm, os_hbm,
              x_s, y_s, xa_s, xb_s, oa_s, ob_s, tok_sm, wts_sm, pend_sm, sem_in, sem_out, sem_sm,
              *, tm, cp_rows, cs_rows, npan):
    chunk = cp_rows + cs_rows
    chunk_stride = chunk + SUBLANES
    half = tm // 2
    half_stride = half + SUBLANES
    n_exp = seg_cnt_ref.shape[0] // MOE_CHUNKS
    seg = pl.program_id(0)
    c = seg // n_exp
    chunk_first = seg % n_exp == 0
    chunk_last = seg % n_exp == n_exp - 1
    n_rows = seg_cnt_ref[seg]
    seg_row0 = seg_base_ref[seg]
    n_whole = jnp.maximum(n_rows - half + tm - 1, 0) // tm
    rows_after = n_rows - n_whole * tm

    def chunk_copies(hbm_p, hbm_s, vmem, sem, to_vmem):
        copies = []
        for p in range(npan):
            for hbm, nrows, row0 in ((hbm_p, cp_rows, 0), (hbm_s, cs_rows, cp_rows)):
                h = hbm.at[pl.ds(c * nrows, nrows), pl.ds(p * LANES, LANES)]
                v = vmem.at[pl.ds(p * chunk_stride + row0, nrows), :]
                copies.append(pltpu.make_async_copy(h, v, sem.at[0]) if to_vmem
                              else pltpu.make_async_copy(v, h, sem.at[0]))
        return copies

    @pl.when(chunk_first)
    def _load_chunk():
        copies = (chunk_copies(xp_hbm, xs_hbm, x_s, sem_in, True)
                  + chunk_copies(rp_hbm, rs_hbm, y_s, sem_in, True))
        for cp in copies:
            cp.start()
        zeros = jnp.zeros((SUBLANES, LANES), F32)
        for p in range(npan):
            x_s[pl.ds(p * chunk_stride + chunk, SUBLANES), :] = zeros
            y_s[pl.ds(p * chunk_stride + chunk, SUBLANES), :] = zeros
        ob_s[...] = jnp.zeros(ob_s.shape, F32)
        pend_sm[0] = 0
        list_rows = tok_sm.shape[0]
        lists = (pltpu.make_async_copy(tok_hbm.at[pl.ds(c * list_rows, list_rows)], tok_sm, sem_sm.at[0]),
                 pltpu.make_async_copy(wts_hbm.at[pl.ds(c * list_rows, list_rows)], wts_sm, sem_sm.at[1]))
        for cp in lists:
            cp.start()
        for cp in lists + tuple(copies):
            cp.wait()

    def gather_half(first, buf):
        for i in range(half):
            t = tok_sm[first + i]
            buf[pl.ds(i, npan, stride=half_stride), :] = x_s[pl.ds(t, npan, stride=chunk_stride), :]

    def ffn_half(src, dst):
        lhs = jnp.concatenate([src[pl.ds(p * half_stride, half), :] for p in range(npan)], axis=1).astype(BF16)
        g = jnp.dot(lhs, wg_ref[0, 0], preferred_element_type=F32)
        u = jnp.dot(lhs, wu_ref[0, 0], preferred_element_type=F32)
        hidden = (g * jax.nn.sigmoid(g) * u).astype(BF16)
        o = jnp.dot(hidden, wd_ref[0, 0], preferred_element_type=F32)
        for p in range(npan):
            dst[pl.ds(p * half_stride, half), :] = o[:, p * LANES:(p + 1) * LANES]

    def combine_half(first, buf):
        for i0 in range(0, half, SCATTER_UNROLL):
            toks, rows_new = [], []
            for i in range(i0, i0 + SCATTER_UNROLL):
                t = tok_sm[first + i]
                w = wts_sm[first + i]
                row = buf[pl.ds(i, npan, stride=half_stride), :]
                toks.append(t)
                rows_new.append(y_s[pl.ds(t, npan, stride=chunk_stride), :] + w * row)
            for t, new in zip(toks, rows_new):
                y_s[pl.ds(t, npan, stride=chunk_stride), :] = new

    def whole_tile(i, carry):
        first = seg_row0 + i * tm
        gather_half(first, xa_s)
        ffn_half(xa_s, oa_s)
        combine_half(pend_sm[0], ob_s)
        gather_half(first + half, xb_s)
        ffn_half(xb_s, ob_s)
        combine_half(first, oa_s)
        pend_sm[0] = first + half
        return carry
    lax.fori_loop(0, n_whole, whole_tile, 0)

    @pl.when(rows_after > 0)
    def _half_tile():
        first = seg_row0 + n_whole * tm
        gather_half(first, xa_s)
        ffn_half(xa_s, oa_s)
        combine_half(pend_sm[0], ob_s)
        ob_s[...] = jnp.zeros(ob_s.shape, F32)
        combine_half(first, oa_s)

    @pl.when(chunk_last)
    def _finish_chunk():
        combine_half(pend_sm[0], ob_s)
        blk = MOE_TILE
        g2 = g2_ref[...]
        b2 = b2_ref[...]

        def ln_block(rb, carry):
            r0 = pl.multiple_of(rb * blk, blk)
            v = jnp.concatenate([y_s[pl.ds(p * chunk_stride + r0, blk), :] for p in range(npan)], axis=1)
            out = _layer_norm(v, g2, b2)
            for p in range(npan):
                y_s[pl.ds(p * chunk_stride + r0, blk), :] = out[:, p * LANES:(p + 1) * LANES]
            return carry
        lax.fori_loop(0, chunk // blk, ln_block, 0)
        copies = chunk_copies(op_hbm, os_hbm, y_s, sem_out, False)
        for cp in copies:
            cp.start()
        for cp in copies:
            cp.wait()


def _list_rows(chunk, top_k, n_exp):
    return (chunk * top_k) // MOE_TILE * MOE_TILE + n_exp * MOE_TILE


def _build_row_lists(dest, wsel, *, chunk, top_k, n_exp):
    n_pairs = top_k * chunk
    list_rows = _list_rows(chunk, top_k, n_exp)
    n_jobs = 2 * MOE_CHUNKS
    assert n_jobs <= SC_CORES * SC_SUBCORES and n_pairs % SC_LANES == 0 and list_rows % SC_LANES == 0
    tokens = jnp.tile(jnp.arange(chunk, dtype=jnp.int32), MOE_CHUNKS * top_k)
    vals = jnp.concatenate([tokens, lax.bitcast_convert_type(wsel, jnp.int32)])
    pad = jnp.concatenate([jnp.full((SC_LANES,), chunk, jnp.int32), jnp.zeros((SC_LANES,), jnp.int32)])
    mesh = plsc.VectorSubcoreMesh(core_axis_name="c", subcore_axis_name="s",
                                  num_cores=SC_CORES, num_subcores=SC_SUBCORES)

    def body(dest_hbm, vals_hbm, pad_hbm, out_hbm, dest_v, vals_v, list_v, pad_v):
        job = lax.axis_index("s") * SC_CORES + lax.axis_index("c")

        @pl.when(job < n_jobs)
        def _scatter_one_list():
            pltpu.sync_copy(dest_hbm.at[pl.ds((job % MOE_CHUNKS) * n_pairs, n_pairs)], dest_v)
            pltpu.sync_copy(vals_hbm.at[pl.ds(job * n_pairs, n_pairs)], vals_v)
            pltpu.sync_copy(pad_hbm.at[pl.ds((job // MOE_CHUNKS) * SC_LANES, SC_LANES)], pad_v)
            pad_vec = pad_v[...]

            @pl.loop(0, list_rows // SC_LANES)
            def _fill(i):
                list_v[pl.ds(i * SC_LANES, SC_LANES)] = pad_vec

            @pl.loop(0, n_pairs // SC_LANES)
            def _scatter(i):
                lanes = pl.ds(i * SC_LANES, SC_LANES)
                plsc.store_scatter(list_v, [dest_v[lanes]], vals_v[lanes])

            pltpu.sync_copy(list_v, out_hbm.at[pl.ds(job * list_rows, list_rows)])

    lists = pl.kernel(
        body, mesh=mesh,
        out_type=jax.ShapeDtypeStruct((n_jobs * list_rows,), jnp.int32),
        scratch_types=[pltpu.VMEM((n_pairs,), jnp.int32), pltpu.VMEM((n_pairs,), jnp.int32),
                       pltpu.VMEM((list_rows,), jnp.int32), pltpu.VMEM((SC_LANES,), jnp.int32)],
        compiler_params=pltpu.CompilerParams(needs_layout_passes=False),
        name="build_row_lists",
    )(dest, vals, pad)
    half = MOE_CHUNKS * list_rows
    return lists[:half], lax.bitcast_convert_type(lists[half:], F32)


def _moe_routed(seg_cnt, seg_base, tok_list, wts_list, wg, wu, wd, g2, b2, xp, xs, rp, rs, *, layer):
    _, n_exp, d_model, d_ff = wg.shape
    tm = MOE_TILE
    npan = d_model // LANES
    assert npan == SUBLANES, "one token row must fill exactly one (8, 128) register"
    cp_rows, cs_rows = xp.shape[0] // MOE_CHUNKS, xs.shape[0] // MOE_CHUNKS
    chunk = cp_rows + cs_rows
    assert chunk % MOE_TILE == 0 and cs_rows % SUBLANES == 0
    chunk_stride = chunk + SUBLANES
    half_stride = tm // 2 + SUBLANES
    list_rows = tok_list.shape[0] // MOE_CHUNKS
    any_spec = pl.BlockSpec(memory_space=pl.ANY)
    expert_map = lambda s, *_: (layer, s % n_exp, 0, 0)
    const2 = lambda s, *_: (0, 0)
    expert_spec = lambda shape: pl.BlockSpec((1, 1) + shape, expert_map)
    vmem = (2 * npan * chunk_stride * LANES * 4 + 4 * npan * half_stride * LANES * 4
            + WEIGHT_BUFFERS * 3 * d_model * d_ff * wg.dtype.itemsize + VMEM_HEADROOM_BYTES)
    return pl.pallas_call(
        functools.partial(_moe_body, tm=tm, cp_rows=cp_rows, cs_rows=cs_rows, npan=npan),
        grid_spec=pltpu.PrefetchScalarGridSpec(
            num_scalar_prefetch=2,
            grid=(MOE_CHUNKS * n_exp,),
            in_specs=[
                expert_spec((d_model, d_ff)),
                expert_spec((d_model, d_ff)),
                expert_spec((d_ff, d_model)),
                pl.BlockSpec((1, d_model), const2),
                pl.BlockSpec((1, d_model), const2),
                any_spec, any_spec, any_spec, any_spec, any_spec, any_spec,
            ],
            out_specs=[any_spec, any_spec],
            scratch_shapes=[
                pltpu.VMEM((npan * chunk_stride, LANES), F32),
                pltpu.VMEM((npan * chunk_stride, LANES), F32),
                pltpu.VMEM((npan * half_stride, LANES), F32),
                pltpu.VMEM((npan * half_stride, LANES), F32),
                pltpu.VMEM((npan * half_stride, LANES), F32),
                pltpu.VMEM((npan * half_stride, LANES), F32),
                pltpu.SMEM((list_rows,), jnp.int32),
                pltpu.SMEM((list_rows,), F32),
                pltpu.SMEM((1,), jnp.int32),
                pltpu.SemaphoreType.DMA((1,)),
                pltpu.SemaphoreType.DMA((1,)),
                pltpu.SemaphoreType.DMA((2,)),
            ]),
        out_shape=[jax.ShapeDtypeStruct(xp.shape, F32), jax.ShapeDtypeStruct(xs.shape, F32)],
        compiler_params=pltpu.CompilerParams(dimension_semantics=("arbitrary",),
                                             vmem_limit_bytes=min(vmem, VMEM_LIMIT_CAP_BYTES)),
        name="moe_routed",
    )(seg_cnt, seg_base, wg, wu, wd, g2.reshape(1, d_model), b2.reshape(1, d_model),
      tok_list, wts_list, xp, xs, rp, rs)


def _plan_tiles(cnt, idx_p, w_p, rank_p, idx_s, w_s, rank_s):
    k, tp = idx_p.shape
    ts = idx_s.shape[1]
    n_chunks, n_exp = cnt.shape
    tm = MOE_TILE
    cp, cs = tp // n_chunks, ts // n_chunks
    experts = jnp.arange(n_exp, dtype=jnp.int32)

    seg_rows = (cnt + tm - 1) // tm * tm
    seg_base = jnp.sum(jnp.where(experts[None, :] < experts[:, None], seg_rows[:, None, :], 0), axis=-1)

    def rows_of(idx, rank, per_chunk):
        idx3 = idx.reshape(k, n_chunks, per_chunk)
        base = jnp.sum(jnp.where(idx3[..., None] == experts, seg_base[None, :, None, :], 0), axis=-1)
        return jnp.transpose(base + rank.reshape(k, n_chunks, per_chunk), (1, 0, 2))
    by_chunk = lambda a, per_chunk: jnp.transpose(a.reshape(k, n_chunks, per_chunk), (1, 0, 2))
    dest = jnp.concatenate([rows_of(idx_p, rank_p, cp), rows_of(idx_s, rank_s, cs)], axis=2).reshape(-1)
    wsel = jnp.concatenate([by_chunk(w_p, cp), by_chunk(w_s, cs)], axis=2).reshape(-1)
    return cnt.reshape(-1), seg_base.reshape(-1), dest, wsel


def kernel(x_prompt, x_sample, state_lru_conv, state_lru_h, state_ccm_conv, lru_w_in, lru_b_in, lru_conv_w, lru_conv_b, lru_w_a, lru_b_a, lru_w_x, lru_b_x, lru_lambda, lru_w_out, lru_b_out, ccm_w_in, ccm_b_in, ccm_dw_w, ccm_dw_b, ccm_ln_g, ccm_ln_b, ccm_w_out, ccm_b_out, ln1_g, ln1_b, ln2_g, ln2_b, router_w, router_bias, exp_w_gate, exp_w_up, exp_w_down, sh_w_gate, sh_w_up, sh_w_down):
    depth = ln1_g.shape[0]
    alpha = (2 * depth) ** 0.25
    n_exp = router_w.shape[2]
    bp, sp, d_model = x_prompt.shape
    bs, ss, _ = x_sample.shape
    kw_lru = lru_conv_w.shape[1]
    kw_ccm = ccm_dw_w.shape[1]
    d_rnn = lru_conv_w.shape[2]
    d_conv = ccm_dw_w.shape[2]
    bf = lambda a: a.astype(BF16)

    zero_lru_conv = jnp.zeros((bp, kw_lru - 1, d_rnn), F32)
    zero_lru_h = jnp.zeros((bp, d_rnn), F32)
    zero_ccm_conv = jnp.zeros((bp, kw_ccm - 1, d_conv), F32)

    exp_gate_bf, exp_up_bf, exp_down_bf = bf(exp_w_gate), bf(exp_w_up), bf(exp_w_down)

    xp, xs = x_prompt, x_sample
    lru_conv_p, lru_h_p, ccm_conv_p = [], [], []
    lru_conv_s, lru_h_s, ccm_conv_s = [], [], []
    for layer in range(depth):
        j = layer // 2
        if layer % 2 == 0:
            weights = (bf(lru_w_in[j]), lru_b_in[j], lru_conv_w[j], lru_conv_b[j], bf(lru_w_a[j]), lru_b_a[j],
                       bf(lru_w_x[j]), lru_b_x[j], lru_lambda[j], bf(lru_w_out[j]), lru_b_out[j],
                       ln1_g[layer], ln1_b[layer])
            xp, cb, hl = _lru_mixer(xp, zero_lru_conv, zero_lru_h, *weights, seq_start=True, alpha=alpha)
            lru_conv_p.append(cb)
            lru_h_p.append(hl)
            xs, cb, hl = _lru_mixer(xs, state_lru_conv[j], state_lru_h[j], *weights, seq_start=False, alpha=alpha)
            lru_conv_s.append(cb)
            lru_h_s.append(hl)
        else:
            weights = (bf(ccm_w_in[j]), ccm_b_in[j], ccm_dw_w[j], ccm_dw_b[j], ccm_ln_g[j], ccm_ln_b[j],
                       bf(ccm_w_out[j]), ccm_b_out[j], ln1_g[layer], ln1_b[layer])
            xp, cb = _ccm_mixer(xp, zero_ccm_conv, *weights, alpha=alpha)
            ccm_conv_p.append(cb)
            xs, cb = _ccm_mixer(xs, state_ccm_conv[j], *weights, alpha=alpha)
            ccm_conv_s.append(cb)

        shared = (router_w[layer].T, router_bias[layer], bf(sh_w_gate[layer]), bf(sh_w_up[layer]),
                  bf(sh_w_down[layer]))
        xp2 = xp.reshape(bp * sp, d_model)
        xs2 = xs.reshape(bs * ss, d_model)
        no_tokens = jnp.zeros((MOE_CHUNKS, n_exp, LANES), F32)
        resid_p, idx_p, w_p, rank_p, cnt_p = _route_shared(xp2, *shared, no_tokens, alpha=alpha)
        resid_s, idx_s, w_s, rank_s, cnt = _route_shared(xs2, *shared, cnt_p, alpha=alpha)
        seg_cnt, seg_base, dest, wsel = _plan_tiles(cnt[:, :, 0].astype(jnp.int32), idx_p, w_p, rank_p,
                                                    idx_s, w_s, rank_s)
        tok_list, wts_list = _build_row_lists(dest, wsel, chunk=(xp2.shape[0] + xs2.shape[0]) // MOE_CHUNKS,
                                              top_k=idx_p.shape[0], n_exp=n_exp)
        xp2, xs2 = _moe_routed(seg_cnt, seg_base, tok_list, wts_list, exp_gate_bf, exp_up_bf, exp_down_bf,
                               ln2_g[layer], ln2_b[layer], xp2, xs2, resid_p, resid_s, layer=layer)
        xp = xp2.reshape(bp, sp, d_model)
        xs = xs2.reshape(bs, ss, d_model)

    return (xp, xs, jnp.stack(lru_conv_p), jnp.stack(lru_h_p), jnp.stack(ccm_conv_p),
            jnp.stack(lru_conv_s), jnp.stack(lru_h_s), jnp.stack(ccm_conv_s))
```

```python
import functools

import jax
import jax.numpy as jnp
from jax import lax
from jax.experimental import pallas as pl
from jax.experimental.pallas import tpu as pltpu
from jax.experimental.pallas import tpu_sc as plsc

LANES = 128
SUBLANES = 8
SC_CORES = 2
SC_SUBCORES = 16
SC_LANES = 16
VMEM_HEADROOM_BYTES = 8 << 20
VMEM_LIMIT_CAP_BYTES = 56 << 20
CONV_STATE_BLOCK_BYTES = 4 << 20

LN_EPS = 1e-5
LRU_C = 8.0
N_GROUPS = 8
TOPK_GROUPS = 4
TOP_K = 8
ROUTED_SCALE = 2.5

MIXER_ROWS = 512
ROUTE_ROWS = 512
MOE_TILE = 256
MOE_CHUNKS = 4
SCATTER_UNROLL = 8
WEIGHT_BUFFERS = 2
ACC_VREGS = 16
SCAN_VREGS = 16

F32 = jnp.float32
BF16 = jnp.bfloat16


def _layer_norm(v, g, b):
    mu = jnp.mean(v, axis=-1, keepdims=True)
    d = v - mu
    var = jnp.mean(d * d, axis=-1, keepdims=True)
    return d * lax.rsqrt(var + LN_EPS) * g + b


def _round_up(n, m):
    return (n + m - 1) // m * m


def _lru_body(x_ref, cs_ref, h0_ref, win_ref, bin_ref, cw_ref, cb_ref, wa_ref, ba_ref, wx_ref, bx_ref,
              lam_ref, wout_ref, bout_ref, g1_ref, b1_ref,
              x1_ref, ncs_ref, hl_ref,
              ubuf, a_s, b_s, h_s, *, seq_start, alpha, scan_panels):
    nb, tc, d_model = x_ref.shape
    d_rnn = h0_ref.shape[1]
    heads = wa_ref.shape[0]
    hb = d_rnn // heads
    kw = cw_ref.shape[0]
    hist0 = SUBLANES - (kw - 1)
    rows = nb * tc
    ti = pl.program_id(1)

    @pl.when(ti == 0)
    def _init_state():
        ubuf[:, hist0:SUBLANES, :] = cs_ref[...]
        h_s[...] = h0_ref[...]

    x = x_ref[...].reshape(rows, d_model)
    proj = jnp.dot(x.astype(BF16), win_ref[...], preferred_element_type=F32) + bin_ref[...]
    gate = jax.nn.gelu(proj[:, :d_rnn])
    u = proj[:, d_rnn:]
    ubuf[:, SUBLANES:SUBLANES + tc, :] = u.reshape(nb, tc, d_rnn)
    xc3 = jnp.broadcast_to(cb_ref[...].reshape(1, 1, d_rnn), (nb, tc, d_rnn))
    for j in range(kw):
        xc3 = xc3 + cw_ref[j:j + 1, :].reshape(1, 1, d_rnn) * ubuf[:, hist0 + j:hist0 + j + tc, :]
    tail = ubuf[:, tc + hist0:tc + SUBLANES, :]
    ncs_ref[...] = tail
    ubuf[:, hist0:SUBLANES, :] = tail
    xc = xc3.reshape(rows, d_rnn)

    xcb = xc.astype(BF16)
    ra = jnp.concatenate([jnp.dot(xcb[:, h * hb:(h + 1) * hb], wa_ref[h], preferred_element_type=F32)
                          for h in range(heads)], axis=1)
    ia = jnp.concatenate([jnp.dot(xcb[:, h * hb:(h + 1) * hb], wx_ref[h], preferred_element_type=F32)
                          for h in range(heads)], axis=1)
    r = jax.nn.sigmoid(ra + ba_ref[...])
    gi = jax.nn.sigmoid(ia + bx_ref[...])
    lam = lam_ref[...]
    softplus_neg_lam = jnp.maximum(-lam, 0.0) + jnp.log1p(jnp.exp(-jnp.abs(lam)))
    log_a = (-LRU_C * softplus_neg_lam) * r
    a = jnp.exp(log_a)
    mult = jnp.sqrt(-jnp.tanh(log_a) * (a * a + 1.0))
    if seq_start:
        t_in_seq = lax.broadcasted_iota(jnp.int32, (rows, 1), 0) % tc
        mult = jnp.where((t_in_seq == 0) & (ti == 0), 1.0, mult)
    bterm = xc * gi * mult
    npan = d_rnn // LANES
    seq_stride = a_s.shape[1] // nb
    for p in range(npan):
        for n in range(nb):
            dst = pl.ds(n * seq_stride, tc)
            a_s[p, dst, :] = a[n * tc:(n + 1) * tc, p * LANES:(p + 1) * LANES]
            b_s[p, dst, :] = bterm[n * tc:(n + 1) * tc, p * LANES:(p + 1) * LANES]

    for p0 in range(0, npan, scan_panels):
        group = range(p0, min(p0 + scan_panels, npan))
        hs = [h_s[:, p * LANES:(p + 1) * LANES] for p in group]
        for t in range(tc):
            rs = pl.ds(t, nb, stride=seq_stride)
            for k, p in enumerate(group):
                hs[k] = a_s[p, rs, :] * hs[k] + b_s[p, rs, :]
                b_s[p, rs, :] = hs[k]
        for k, p in enumerate(group):
            h_s[:, p * LANES:(p + 1) * LANES] = hs[k]
    hl_ref[...] = h_s[...]

    h_all = jnp.concatenate(
        [jnp.concatenate([b_s[p, pl.ds(n * seq_stride, tc), :] for n in range(nb)], axis=0) for p in range(npan)],
        axis=1)
    y = jnp.dot((h_all * gate).astype(BF16), wout_ref[...], preferred_element_type=F32) + bout_ref[...]
    x1 = _layer_norm(alpha * x + y, g1_ref[...], b1_ref[...])
    x1_ref[...] = x1.reshape(nb, tc, d_model)


def _lru_mixer(x, conv_state, h0, win, b_in, conv_w, conv_b, wa, ba, wx, bx, lam, wout, bout, g1, b1, *,
               seq_start, alpha):
    n, s, d_model = x.shape
    d_rnn = h0.shape[1]
    kw = conv_w.shape[0]
    tc = min(s, MIXER_ROWS // SUBLANES)
    nb = min(n, MIXER_ROWS // tc)
    rows = nb * tc
    scan_panels = max(1, SCAN_VREGS * SUBLANES // nb)
    seq_stride = tc + SUBLANES if tc % (2 * SUBLANES) == 0 else tc
    row = lambda v: v.reshape(1, -1)
    const2 = lambda i, t: (0, 0)
    const3 = lambda i, t: (0, 0, 0)
    vmem = (2 * 2 * rows * d_model * 4 + 2 * (win.size + wout.size + wa.size + wx.size) * 2
            + (nb * (tc + SUBLANES) + 2 * rows + nb) * d_rnn * 4 + 4 * rows * 2 * d_rnn * 4 + VMEM_HEADROOM_BYTES)
    return pl.pallas_call(
        functools.partial(_lru_body, seq_start=seq_start, alpha=alpha, scan_panels=scan_panels),
        grid=(n // nb, s // tc),
        in_specs=[
            pl.BlockSpec((nb, tc, d_model), lambda i, t: (i, t, 0)),
            pl.BlockSpec((nb, kw - 1, d_rnn), lambda i, t: (i, 0, 0)),
            pl.BlockSpec((nb, d_rnn), lambda i, t: (i, 0)),
            pl.BlockSpec(win.shape, const2), pl.BlockSpec((1, 2 * d_rnn), const2),
            pl.BlockSpec(conv_w.shape, const2), pl.BlockSpec((1, d_rnn), const2),
            pl.BlockSpec(wa.shape, const3), pl.BlockSpec((1, d_rnn), const2),
            pl.BlockSpec(wx.shape, const3), pl.BlockSpec((1, d_rnn), const2),
            pl.BlockSpec((1, d_rnn), const2),
            pl.BlockSpec(wout.shape, const2), pl.BlockSpec((1, d_model), const2),
            pl.BlockSpec((1, d_model), const2), pl.BlockSpec((1, d_model), const2),
        ],
        out_specs=[
            pl.BlockSpec((nb, tc, d_model), lambda i, t: (i, t, 0)),
            pl.BlockSpec((nb, kw - 1, d_rnn), lambda i, t: (i, 0, 0)),
            pl.BlockSpec((nb, d_rnn), lambda i, t: (i, 0)),
        ],
        out_shape=[jax.ShapeDtypeStruct(x.shape, F32),
                   jax.ShapeDtypeStruct((n, kw - 1, d_rnn), F32),
                   jax.ShapeDtypeStruct((n, d_rnn), F32)],
        scratch_shapes=[pltpu.VMEM((nb, tc + SUBLANES, d_rnn), F32),
                        pltpu.VMEM((d_rnn // LANES, nb * seq_stride, LANES), F32),
                        pltpu.VMEM((d_rnn // LANES, nb * seq_stride, LANES), F32),
                        pltpu.VMEM((nb, d_rnn), F32)],
        compiler_params=pltpu.CompilerParams(dimension_semantics=("arbitrary", "arbitrary"),
                                             vmem_limit_bytes=min(vmem, VMEM_LIMIT_CAP_BYTES)),
        name="lru_mixer",
    )(x, conv_state, h0, win, row(b_in), conv_w, row(conv_b), wa, row(ba), wx, row(bx), row(lam),
      wout, row(bout), row(g1), row(b1))


def _ccm_body(x_ref, cs_ref, win_ref, bin_ref, dw_ref, dwb_ref, lng_ref, lnb_ref, wout_ref, bout_ref,
              g1_ref, b1_ref, x1_ref, ncs_ref, gbuf, cbuf, *, alpha, conv_cols):
    nb, tc, d_model = x_ref.shape
    kw, d_conv = dw_ref.shape
    hist_rows = _round_up(kw - 1, SUBLANES)
    hist0 = hist_rows - (kw - 1)
    rows = nb * tc
    ti = pl.program_id(1)

    @pl.when(ti == 0)
    def _init_state():
        gbuf[:, hist0:hist_rows, :] = cs_ref[...]

    x = x_ref[...].reshape(rows, d_model)
    p = jnp.dot(x.astype(BF16), win_ref[...], preferred_element_type=F32) + bin_ref[...]
    glu = p[:, :d_conv] * jax.nn.sigmoid(p[:, d_conv:])
    gbuf[:, hist_rows:hist_rows + tc, :] = glu.reshape(nb, tc, d_conv)

    def conv_seq(n, carry):
        for c0 in range(0, d_conv, conv_cols):
            cols = slice(c0, c0 + conv_cols)
            acc = jnp.broadcast_to(dwb_ref[:, cols], (tc, conv_cols))
            for shift in range(SUBLANES):
                taps = [j for j in range(kw) if (hist0 + j) % SUBLANES == shift]
                if not taps:
                    continue
                rows = tc if shift == 0 else tc + SUBLANES
                part = jnp.zeros((rows, conv_cols), F32)
                for j in taps:
                    start = hist0 + j - shift
                    part = part + dw_ref[j:j + 1, cols] * gbuf[n, start:start + rows, cols]
                acc = acc + part[shift:shift + tc]
            cbuf[n, :, cols] = acc
        return carry
    lax.fori_loop(0, nb, conv_seq, 0)

    tail = gbuf[:, tc + hist0:tc + hist_rows, :]
    ncs_ref[...] = tail
    gbuf[:, hist0:hist_rows, :] = tail

    c = cbuf[...].reshape(rows, d_conv)
    hn = _layer_norm(c, lng_ref[...], lnb_ref[...])
    hdn = hn * jax.nn.sigmoid(hn)
    y = jnp.dot(hdn.astype(BF16), wout_ref[...], preferred_element_type=F32) + bout_ref[...]
    x1 = _layer_norm(alpha * x + y, g1_ref[...], b1_ref[...])
    x1_ref[...] = x1.reshape(nb, tc, d_model)


def _ccm_mixer(x, conv_state, win, b_in, dw_w, dw_b, ln_g, ln_b, wout, bout, g1, b1, *, alpha):
    n, s, d_model = x.shape
    kw, d_conv = dw_w.shape
    tc = min(s, MIXER_ROWS // SUBLANES)
    nb = min(n, MIXER_ROWS // tc)
    while nb > SUBLANES and nb * (kw - 1) * d_conv * 4 > CONV_STATE_BLOCK_BYTES:
        nb //= 2
    rows = nb * tc
    hist_rows = _round_up(kw - 1, SUBLANES)
    conv_cols = min(d_conv, max(LANES, (ACC_VREGS * SUBLANES * LANES // tc) // LANES * LANES))
    while d_conv % conv_cols:
        conv_cols -= LANES
    row = lambda v: v.reshape(1, -1)
    const2 = lambda i, t: (0, 0)
    vmem = (2 * 2 * rows * d_model * 4 + 2 * (win.size + wout.size) * 2
            + (nb * (tc + hist_rows) + rows) * d_conv * 4 + 4 * nb * (kw - 1) * d_conv * 4
            + 4 * rows * 2 * d_conv * 4 + VMEM_HEADROOM_BYTES)
    return pl.pallas_call(
        functools.partial(_ccm_body, alpha=alpha, conv_cols=conv_cols),
        grid=(n // nb, s // tc),
        in_specs=[
            pl.BlockSpec((nb, tc, d_model), lambda i, t: (i, t, 0)),
            pl.BlockSpec((nb, kw - 1, d_conv), lambda i, t: (i, 0, 0)),
            pl.BlockSpec(win.shape, const2), pl.BlockSpec((1, 2 * d_conv), const2),
            pl.BlockSpec(dw_w.shape, const2), pl.BlockSpec((1, d_conv), const2),
            pl.BlockSpec((1, d_conv), const2), pl.BlockSpec((1, d_conv), const2),
            pl.BlockSpec(wout.shape, const2), pl.BlockSpec((1, d_model), const2),
            pl.BlockSpec((1, d_model), const2), pl.BlockSpec((1, d_model), const2),
        ],
        out_specs=[
            pl.BlockSpec((nb, tc, d_model), lambda i, t: (i, t, 0)),
            pl.BlockSpec((nb, kw - 1, d_conv), lambda i, t: (i, 0, 0)),
        ],
        out_shape=[jax.ShapeDtypeStruct(x.shape, F32),
                   jax.ShapeDtypeStruct((n, kw - 1, d_conv), F32)],
        scratch_shapes=[pltpu.VMEM((nb, tc + hist_rows, d_conv), F32),
                        pltpu.VMEM((nb, tc, d_conv), F32)],
        compiler_params=pltpu.CompilerParams(dimension_semantics=("arbitrary", "arbitrary"),
                                             vmem_limit_bytes=min(vmem, VMEM_LIMIT_CAP_BYTES)),
        name="ccm_mixer",
    )(x, conv_state, win, row(b_in), dw_w, row(dw_b), row(ln_g), row(ln_b), wout, row(bout), row(g1), row(b1))


def _first_argmax(v, axis):
    m = jnp.max(v, axis=axis, keepdims=True)
    iota = lax.broadcasted_iota(jnp.int32, v.shape, axis)
    first = jnp.min(jnp.where(v == m, iota, v.shape[axis]), axis=axis, keepdims=True)
    return m, iota == first, first


def _route_body(x_ref, rwt_ref, rb_ref, sg_ref, su_ref, sd_ref, cnt_in_ref,
                resid_ref, idx_ref, w_ref, rank_ref, cnt_ref, tri_s, run_s, *, alpha, steps_per_chunk):
    x = x_ref[...]
    n_exp = rwt_ref.shape[0]
    tr = x.shape[0]
    step = pl.program_id(0)

    @pl.when(step == 0)
    def _build_prefix_matrix():
        r = lax.broadcasted_iota(jnp.int32, (tr, tr), 0)
        c = lax.broadcasted_iota(jnp.int32, (tr, tr), 1)
        tri_s[...] = jnp.where(r <= c, 1.0, 0.0).astype(BF16)

    @pl.when(step % steps_per_chunk == 0)
    def _start_chunk():
        run_s[...] = cnt_in_ref[0]

    logits = lax.dot_general(rwt_ref[...], x, (((1,), (1,)), ((), ())),
                             precision=lax.Precision.HIGHEST, preferred_element_type=F32)
    scores = jax.nn.sigmoid(logits)
    biased = scores + rb_ref[...]
    per_group = n_exp // N_GROUPS
    grp = biased.reshape(N_GROUPS, per_group, tr)
    m1, hit1, _ = _first_argmax(grp, 1)
    m2 = jnp.max(jnp.where(hit1, -jnp.inf, grp), axis=1, keepdims=True)
    gscore = (m1 + m2).reshape(N_GROUPS, tr)
    gsel = jnp.zeros((N_GROUPS, tr), F32)
    for _ in range(TOPK_GROUPS):
        _, hit, _ = _first_argmax(gscore, 0)
        gsel = jnp.where(hit, 1.0, gsel)
        gscore = jnp.where(hit, -jnp.inf, gscore)
    emask = jnp.broadcast_to(gsel.reshape(N_GROUPS, 1, tr), (N_GROUPS, per_group, tr)).reshape(n_exp, tr)
    masked = jnp.where(emask > 0.0, biased, -jnp.inf)
    idx_rows, w_rows, hits = [], [], []
    for _ in range(TOP_K):
        _, hit, first = _first_argmax(masked, 0)
        idx_rows.append(first)
        hits.append(hit)
        w_rows.append(jnp.sum(jnp.where(hit, scores, 0.0), axis=0, keepdims=True))
        masked = jnp.where(hit, -jnp.inf, masked)
    w = jnp.concatenate(w_rows, axis=0)
    idx_ref[...] = jnp.concatenate(idx_rows, axis=0)
    w_ref[...] = w / jnp.sum(w, axis=0, keepdims=True) * ROUTED_SCALE

    sel = jnp.zeros((n_exp, tr), F32)
    for hit in hits:
        sel = jnp.where(hit, 1.0, sel)
    before = jnp.dot(sel.astype(BF16), tri_s[...], preferred_element_type=F32) - sel + run_s[:, 0:1]
    rank_ref[...] = jnp.concatenate(
        [jnp.sum(jnp.where(hit, before, 0.0), axis=0, keepdims=True) for hit in hits], axis=0).astype(jnp.int32)
    run_s[...] = run_s[...] + jnp.sum(sel, axis=1, keepdims=True)
    cnt_ref[0] = run_s[...]

    xb = x.astype(BF16)
    sgate = jnp.dot(xb, sg_ref[...], preferred_element_type=F32)
    sup = jnp.dot(xb, su_ref[...], preferred_element_type=F32)
    hs = (sgate * jax.nn.sigmoid(sgate) * sup).astype(BF16)
    resid_ref[...] = alpha * x + jnp.dot(hs, sd_ref[...], preferred_element_type=F32)


def _route_shared(x, router_wt, router_bias, sg, su, sd, cnt_in, *, alpha):
    t, d_model = x.shape
    n_exp = router_wt.shape[0]
    rows_per_chunk = t // MOE_CHUNKS
    tr = min(rows_per_chunk, ROUTE_ROWS)
    steps_per_chunk = rows_per_chunk // tr
    const2 = lambda i: (0, 0)
    chunk_map = lambda i: (i // steps_per_chunk, 0, 0)
    vmem = 2 * 2 * tr * d_model * 4 + 2 * (router_wt.size * 4 + 3 * sg.size * 2) + VMEM_HEADROOM_BYTES
    return pl.pallas_call(
        functools.partial(_route_body, alpha=alpha, steps_per_chunk=steps_per_chunk),
        grid=(t // tr,),
        in_specs=[pl.BlockSpec((tr, d_model), lambda i: (i, 0)),
                  pl.BlockSpec(router_wt.shape, const2), pl.BlockSpec((n_exp, 1), const2),
                  pl.BlockSpec(sg.shape, const2), pl.BlockSpec(su.shape, const2), pl.BlockSpec(sd.shape, const2),
                  pl.BlockSpec((1, n_exp, LANES), chunk_map)],
        out_specs=[pl.BlockSpec((tr, d_model), lambda i: (i, 0)),
                   pl.BlockSpec((TOP_K, tr), lambda i: (0, i)),
                   pl.BlockSpec((TOP_K, tr), lambda i: (0, i)),
                   pl.BlockSpec((TOP_K, tr), lambda i: (0, i)),
                   pl.BlockSpec((1, n_exp, LANES), chunk_map)],
        out_shape=[jax.ShapeDtypeStruct((t, d_model), F32),
                   jax.ShapeDtypeStruct((TOP_K, t), jnp.int32),
                   jax.ShapeDtypeStruct((TOP_K, t), F32),
                   jax.ShapeDtypeStruct((TOP_K, t), jnp.int32),
                   jax.ShapeDtypeStruct((MOE_CHUNKS, n_exp, LANES), F32)],
        scratch_shapes=[pltpu.VMEM((tr, tr), BF16), pltpu.VMEM((n_exp, LANES), F32)],
        compiler_params=pltpu.CompilerParams(dimension_semantics=("arbitrary",),
                                             vmem_limit_bytes=min(vmem, VMEM_LIMIT_CAP_BYTES)),
        name="route_shared",
    )(x, router_wt, router_bias.reshape(n_exp, 1), sg, su, sd, cnt_in)


def _moe_body(seg_cnt_ref, seg_base_ref,
              wg_ref, wu_ref, wd_ref, g2_ref, b2_ref,
              tok_hbm, wts_hbm, xp_hbm, xs_hbm, rp_hbm, rs_hbm,
              op_hbm, os_hbm,
              x_s, y_s, xa_s, xb_s, oa_s, ob_s, tok_sm, wts_sm, pend_sm, sem_in, sem_out, sem_sm,
              *, tm, cp_rows, cs_rows, npan):
    chunk = cp_rows + cs_rows
    chunk_stride = chunk + SUBLANES
    half = tm // 2
    half_stride = half + SUBLANES
    n_exp = seg_cnt_ref.shape[0] // MOE_CHUNKS
    seg = pl.program_id(0)
    c = seg // n_exp
    chunk_first = seg % n_exp == 0
    chunk_last = seg % n_exp == n_exp - 1
    n_rows = seg_cnt_ref[seg]
    seg_row0 = seg_base_ref[seg]
    n_whole = jnp.maximum(n_rows - half + tm - 1, 0) // tm
    rows_after = n_rows - n_whole * tm

    def chunk_copies(hbm_p, hbm_s, vmem, sem, to_vmem):
        copies = []
        for p in range(npan):
            for hbm, nrows, row0 in ((hbm_p, cp_rows, 0), (hbm_s, cs_rows, cp_rows)):
                h = hbm.at[pl.ds(c * nrows, nrows), pl.ds(p * LANES, LANES)]
                v = vmem.at[pl.ds(p * chunk_stride + row0, nrows), :]
                copies.append(pltpu.make_async_copy(h, v, sem.at[0]) if to_vmem
                              else pltpu.make_async_copy(v, h, sem.at[0]))
        return copies

    @pl.when(chunk_first)
    def _load_chunk():
        copies = (chunk_copies(xp_hbm, xs_hbm, x_s, sem_in, True)
                  + chunk_copies(rp_hbm, rs_hbm, y_s, sem_in, True))
        for cp in copies:
            cp.start()
        zeros = jnp.zeros((SUBLANES, LANES), F32)
        for p in range(npan):
            x_s[pl.ds(p * chunk_stride + chunk, SUBLANES), :] = zeros
            y_s[pl.ds(p * chunk_stride + chunk, SUBLANES), :] = zeros
        ob_s[...] = jnp.zeros(ob_s.shape, F32)
        pend_sm[0] = 0
        list_rows = tok_sm.shape[0]
        lists = (pltpu.make_async_copy(tok_hbm.at[pl.ds(c * list_rows, list_rows)], tok_sm, sem_sm.at[0]),
                 pltpu.make_async_copy(wts_hbm.at[pl.ds(c * list_rows, list_rows)], wts_sm, sem_sm.at[1]))
        for cp in lists:
            cp.start()
        for cp in lists + tuple(copies):
            cp.wait()

    def gather_half(first, buf):
        for i in range(half):
            t = tok_sm[first + i]
            buf[pl.ds(i, npan, stride=half_stride), :] = x_s[pl.ds(t, npan, stride=chunk_stride), :]

    def ffn_half(src, dst):
        lhs = jnp.concatenate([src[pl.ds(p * half_stride, half), :] for p in range(npan)], axis=1).astype(BF16)
        g = jnp.dot(lhs, wg_ref[0, 0], preferred_element_type=F32)
        u = jnp.dot(lhs, wu_ref[0, 0], preferred_element_type=F32)
        hidden = (g * jax.nn.sigmoid(g) * u).astype(BF16)
        o = jnp.dot(hidden, wd_ref[0, 0], preferred_element_type=F32)
        for p in range(npan):
            dst[pl.ds(p * half_stride, half), :] = o[:, p * LANES:(p + 1) * LANES]

    def combine_half(first, buf):
        for i0 in range(0, half, SCATTER_UNROLL):
            toks, rows_new = [], []
            for i in range(i0, i0 + SCATTER_UNROLL):
                t = tok_sm[first + i]
                w = wts_sm[first + i]
                row = buf[pl.ds(i, npan, stride=half_stride), :]
                toks.append(t)
                rows_new.append(y_s[pl.ds(t, npan, stride=chunk_stride), :] + w * row)
            for t, new in zip(toks, rows_new):
                y_s[pl.ds(t, npan, stride=chunk_stride), :] = new

    def whole_tile(i, carry):
        first = seg_row0 + i * tm
        gather_half(first, xa_s)
        ffn_half(xa_s, oa_s)
        combine_half(pend_sm[0], ob_s)
        gather_half(first + half, xb_s)
        ffn_half(xb_s, ob_s)
        combine_half(first, oa_s)
        pend_sm[0] = first + half
        return carry
    lax.fori_loop(0, n_whole, whole_tile, 0)

    @pl.when(rows_after > 0)
    def _half_tile():
        first = seg_row0 + n_whole * tm
        gather_half(first, xa_s)
        ffn_half(xa_s, oa_s)
        combine_half(pend_sm[0], ob_s)
        ob_s[...] = jnp.zeros(ob_s.shape, F32)
        combine_half(first, oa_s)

    @pl.when(chunk_last)
    def _finish_chunk():
        combine_half(pend_sm[0], ob_s)
        blk = MOE_TILE
        g2 = g2_ref[...]
        b2 = b2_ref[...]

        def ln_block(rb, carry):
            r0 = pl.multiple_of(rb * blk, blk)
            v = jnp.concatenate([y_s[pl.ds(p * chunk_stride + r0, blk), :] for p in range(npan)], axis=1)
            out = _layer_norm(v, g2, b2)
            for p in range(npan):
                y_s[pl.ds(p * chunk_stride + r0, blk), :] = out[:, p * LANES:(p + 1) * LANES]
            return carry
        lax.fori_loop(0, chunk // blk, ln_block, 0)
        copies = chunk_copies(op_hbm, os_hbm, y_s, sem_out, False)
        for cp in copies:
            cp.start()
        for cp in copies:
            cp.wait()


def _list_rows(chunk, top_k, n_exp):
    return (chunk * top_k) // MOE_TILE * MOE_TILE + n_exp * MOE_TILE


def _build_row_lists(dest, wsel, *, chunk, top_k, n_exp):
    n_pairs = top_k * chunk
    list_rows = _list_rows(chunk, top_k, n_exp)
    n_jobs = 2 * MOE_CHUNKS
    assert n_jobs <= SC_CORES * SC_SUBCORES and n_pairs % SC_LANES == 0 and list_rows % SC_LANES == 0
    tokens = jnp.tile(jnp.arange(chunk, dtype=jnp.int32), MOE_CHUNKS * top_k)
    vals = jnp.concatenate([tokens, lax.bitcast_convert_type(wsel, jnp.int32)])
    pad = jnp.concatenate([jnp.full((SC_LANES,), chunk, jnp.int32), jnp.zeros((SC_LANES,), jnp.int32)])
    mesh = plsc.VectorSubcoreMesh(core_axis_name="c", subcore_axis_name="s",
                                  num_cores=SC_CORES, num_subcores=SC_SUBCORES)

    def body(dest_hbm, vals_hbm, pad_hbm, out_hbm, dest_v, vals_v, list_v, pad_v):
        job = lax.axis_index("s") * SC_CORES + lax.axis_index("c")

        @pl.when(job < n_jobs)
        def _scatter_one_list():
            pltpu.sync_copy(dest_hbm.at[pl.ds((job % MOE_CHUNKS) * n_pairs, n_pairs)], dest_v)
            pltpu.sync_copy(vals_hbm.at[pl.ds(job * n_pairs, n_pairs)], vals_v)
            pltpu.sync_copy(pad_hbm.at[pl.ds((job // MOE_CHUNKS) * SC_LANES, SC_LANES)], pad_v)
            pad_vec = pad_v[...]

            @pl.loop(0, list_rows // SC_LANES)
            def _fill(i):
                list_v[pl.ds(i * SC_LANES, SC_LANES)] = pad_vec

            @pl.loop(0, n_pairs // SC_LANES)
            def _scatter(i):
                lanes = pl.ds(i * SC_LANES, SC_LANES)
                plsc.store_scatter(list_v, [dest_v[lanes]], vals_v[lanes])

            pltpu.sync_copy(list_v, out_hbm.at[pl.ds(job * list_rows, list_rows)])

    lists = pl.kernel(
        body, mesh=mesh,
        out_type=jax.ShapeDtypeStruct((n_jobs * list_rows,), jnp.int32),
        scratch_types=[pltpu.VMEM((n_pairs,), jnp.int32), pltpu.VMEM((n_pairs,), jnp.int32),
                       pltpu.VMEM((list_rows,), jnp.int32), pltpu.VMEM((SC_LANES,), jnp.int32)],
        compiler_params=pltpu.CompilerParams(needs_layout_passes=False),
        name="build_row_lists",
    )(dest, vals, pad)
    half = MOE_CHUNKS * list_rows
    return lists[:half], lax.bitcast_convert_type(lists[half:], F32)


def _moe_routed(seg_cnt, seg_base, tok_list, wts_list, wg, wu, wd, g2, b2, xp, xs, rp, rs, *, layer):
    _, n_exp, d_model, d_ff = wg.shape
    tm = MOE_TILE
    npan = d_model // LANES
    assert npan == SUBLANES, "one token row must fill exactly one (8, 128) register"
    cp_rows, cs_rows = xp.shape[0] // MOE_CHUNKS, xs.shape[0] // MOE_CHUNKS
    chunk = cp_rows + cs_rows
    assert chunk % MOE_TILE == 0 and cs_rows % SUBLANES == 0
    chunk_stride = chunk + SUBLANES
    half_stride = tm // 2 + SUBLANES
    list_rows = tok_list.shape[0] // MOE_CHUNKS
    any_spec = pl.BlockSpec(memory_space=pl.ANY)
    expert_map = lambda s, *_: (layer, s % n_exp, 0, 0)
    const2 = lambda s, *_: (0, 0)
    expert_spec = lambda shape: pl.BlockSpec((1, 1) + shape, expert_map)
    vmem = (2 * npan * chunk_stride * LANES * 4 + 4 * npan * half_stride * LANES * 4
            + WEIGHT_BUFFERS * 3 * d_model * d_ff * wg.dtype.itemsize + VMEM_HEADROOM_BYTES)
    return pl.pallas_call(
        functools.partial(_moe_body, tm=tm, cp_rows=cp_rows, cs_rows=cs_rows, npan=npan),
        grid_spec=pltpu.PrefetchScalarGridSpec(
            num_scalar_prefetch=2,
            grid=(MOE_CHUNKS * n_exp,),
            in_specs=[
                expert_spec((d_model, d_ff)),
                expert_spec((d_model, d_ff)),
                expert_spec((d_ff, d_model)),
                pl.BlockSpec((1, d_model), const2),
                pl.BlockSpec((1, d_model), const2),
                any_spec, any_spec, any_spec, any_spec, any_spec, any_spec,
            ],
            out_specs=[any_spec, any_spec],
            scratch_shapes=[
                pltpu.VMEM((npan * chunk_stride, LANES), F32),
                pltpu.VMEM((npan * chunk_stride, LANES), F32),
                pltpu.VMEM((npan * half_stride, LANES), F32),
                pltpu.VMEM((npan * half_stride, LANES), F32),
                pltpu.VMEM((npan * half_stride, LANES), F32),
                pltpu.VMEM((npan * half_stride, LANES), F32),
                pltpu.SMEM((list_rows,), jnp.int32),
                pltpu.SMEM((list_rows,), F32),
                pltpu.SMEM((1,), jnp.int32),
                pltpu.SemaphoreType.DMA((1,)),
                pltpu.SemaphoreType.DMA((1,)),
                pltpu.SemaphoreType.DMA((2,)),
            ]),
        out_shape=[jax.ShapeDtypeStruct(xp.shape, F32), jax.ShapeDtypeStruct(xs.shape, F32)],
        compiler_params=pltpu.CompilerParams(dimension_semantics=("arbitrary",),
                                             vmem_limit_bytes=min(vmem, VMEM_LIMIT_CAP_BYTES)),
        name="moe_routed",
    )(seg_cnt, seg_base, wg, wu, wd, g2.reshape(1, d_model), b2.reshape(1, d_model),
      tok_list, wts_list, xp, xs, rp, rs)


def _plan_tiles(cnt, idx_p, w_p, rank_p, idx_s, w_s, rank_s):
    k, tp = idx_p.shape
    ts = idx_s.shape[1]
    n_chunks, n_exp = cnt.shape
    tm = MOE_TILE
    cp, cs = tp // n_chunks, ts // n_chunks
    experts = jnp.arange(n_exp, dtype=jnp.int32)

    seg_rows = (cnt + tm - 1) // tm * tm
    seg_base = jnp.sum(jnp.where(experts[None, :] < experts[:, None], seg_rows[:, None, :], 0), axis=-1)

    def rows_of(idx, rank, per_chunk):
        idx3 = idx.reshape(k, n_chunks, per_chunk)
        base = jnp.sum(jnp.where(idx3[..., None] == experts, seg_base[None, :, None, :], 0), axis=-1)
        return jnp.transpose(base + rank.reshape(k, n_chunks, per_chunk), (1, 0, 2))
    by_chunk = lambda a, per_chunk: jnp.transpose(a.reshape(k, n_chunks, per_chunk), (1, 0, 2))
    dest = jnp.concatenate([rows_of(idx_p, rank_p, cp), rows_of(idx_s, rank_s, cs)], axis=2).reshape(-1)
    wsel = jnp.concatenate([by_chunk(w_p, cp), by_chunk(w_s, cs)], axis=2).reshape(-1)
    return cnt.reshape(-1), seg_base.reshape(-1), dest, wsel


def kernel(x_prompt, x_sample, state_lru_conv, state_lru_h, state_ccm_conv, lru_w_in, lru_b_in, lru_conv_w, lru_conv_b, lru_w_a, lru_b_a, lru_w_x, lru_b_x, lru_lambda, lru_w_out, lru_b_out, ccm_w_in, ccm_b_in, ccm_dw_w, ccm_dw_b, ccm_ln_g, ccm_ln_b, ccm_w_out, ccm_b_out, ln1_g, ln1_b, ln2_g, ln2_b, router_w, router_bias, exp_w_gate, exp_w_up, exp_w_down, sh_w_gate, sh_w_up, sh_w_down):
    depth = ln1_g.shape[0]
    alpha = (2 * depth) ** 0.25
    n_exp = router_w.shape[2]
    bp, sp, d_model = x_prompt.shape
    bs, ss, _ = x_sample.shape
    kw_lru = lru_conv_w.shape[1]
    kw_ccm = ccm_dw_w.shape[1]
    d_rnn = lru_conv_w.shape[2]
    d_conv = ccm_dw_w.shape[2]
    bf = lambda a: a.astype(BF16)

    zero_lru_conv = jnp.zeros((bp, kw_lru - 1, d_rnn), F32)
    zero_lru_h = jnp.zeros((bp, d_rnn), F32)
    zero_ccm_conv = jnp.zeros((bp, kw_ccm - 1, d_conv), F32)

    exp_gate_bf, exp_up_bf, exp_down_bf = bf(exp_w_gate), bf(exp_w_up), bf(exp_w_down)

    xp, xs = x_prompt, x_sample
    lru_conv_p, lru_h_p, ccm_conv_p = [], [], []
    lru_conv_s, lru_h_s, ccm_conv_s = [], [], []
    for layer in range(depth):
        j = layer // 2
        if layer % 2 == 0:
            weights = (bf(lru_w_in[j]), lru_b_in[j], lru_conv_w[j], lru_conv_b[j], bf(lru_w_a[j]), lru_b_a[j],
                       bf(lru_w_x[j]), lru_b_x[j], lru_lambda[j], bf(lru_w_out[j]), lru_b_out[j],
                       ln1_g[layer], ln1_b[layer])
            xp, cb, hl = _lru_mixer(xp, zero_lru_conv, zero_lru_h, *weights, seq_start=True, alpha=alpha)
            lru_conv_p.append(cb)
            lru_h_p.append(hl)
            xs, cb, hl = _lru_mixer(xs, state_lru_conv[j], state_lru_h[j], *weights, seq_start=False, alpha=alpha)
            lru_conv_s.append(cb)
            lru_h_s.append(hl)
        else:
            weights = (bf(ccm_w_in[j]), ccm_b_in[j], ccm_dw_w[j], ccm_dw_b[j], ccm_ln_g[j], ccm_ln_b[j],
                       bf(ccm_w_out[j]), ccm_b_out[j], ln1_g[layer], ln1_b[layer])
            xp, cb = _ccm_mixer(xp, zero_ccm_conv, *weights, alpha=alpha)
            ccm_conv_p.append(cb)
            xs, cb = _ccm_mixer(xs, state_ccm_conv[j], *weights, alpha=alpha)
            ccm_conv_s.append(cb)

        shared = (router_w[layer].T, router_bias[layer], bf(sh_w_gate[layer]), bf(sh_w_up[layer]),
                  bf(sh_w_down[layer]))
        xp2 = xp.reshape(bp * sp, d_model)
        xs2 = xs.reshape(bs * ss, d_model)
        no_tokens = jnp.zeros((MOE_CHUNKS, n_exp, LANES), F32)
        resid_p, idx_p, w_p, rank_p, cnt_p = _route_shared(xp2, *shared, no_tokens, alpha=alpha)
        resid_s, idx_s, w_s, rank_s, cnt = _route_shared(xs2, *shared, cnt_p, alpha=alpha)
        seg_cnt, seg_base, dest, wsel = _plan_tiles(cnt[:, :, 0].astype(jnp.int32), idx_p, w_p, rank_p,
                                                    idx_s, w_s, rank_s)
        tok_list, wts_list = _build_row_lists(dest, wsel, chunk=(xp2.shape[0] + xs2.shape[0]) // MOE_CHUNKS,
                                              top_k=idx_p.shape[0], n_exp=n_exp)
        xp2, xs2 = _moe_routed(seg_cnt, seg_base, tok_list, wts_list, exp_gate_bf, exp_up_bf, exp_down_bf,
                               ln2_g[layer], ln2_b[layer], xp2, xs2, resid_p, resid_s, layer=layer)
        xp = xp2.reshape(bp, sp, d_model)
        xs = xs2.reshape(bs, ss, d_model)

    return (xp, xs, jnp.stack(lru_conv_p), jnp.stack(lru_h_p), jnp.stack(ccm_conv_p),
            jnp.stack(lru_conv_s), jnp.stack(lru_h_s), jnp.stack(ccm_conv_s))
```

```python
import functools

import jax
import jax.numpy as jnp
from jax import lax
from jax.experimental import pallas as pl
from jax.experimental.pallas import tpu as pltpu
from jax.experimental.pallas import tpu_sc as plsc

LANES = 128
SUBLANES = 8
SC_CORES = 2
SC_SUBCORES = 16
SC_LANES = 16
VMEM_HEADROOM_BYTES = 8 << 20
VMEM_LIMIT_CAP_BYTES = 56 << 20
CONV_STATE_BLOCK_BYTES = 4 << 20

LN_EPS = 1e-5
LRU_C = 8.0
N_GROUPS = 8
TOPK_GROUPS = 4
TOP_K = 8
ROUTED_SCALE = 2.5

MIXER_ROWS = 512
ROUTE_ROWS = 512
MOE_TILE = 256
MOE_CHUNKS = 4
SCATTER_UNROLL = 8
WEIGHT_BUFFERS = 2
ACC_VREGS = 16
SCAN_VREGS = 16

F32 = jnp.float32
BF16 = jnp.bfloat16


def _layer_norm(v, g, b):
    mu = jnp.mean(v, axis=-1, keepdims=True)
    d = v - mu
    var = jnp.mean(d * d, axis=-1, keepdims=True)
    return d * lax.rsqrt(var + LN_EPS) * g + b


def _round_up(n, m):
    return (n + m - 1) // m * m


def _lru_body(x_ref, cs_ref, h0_ref, win_ref, bin_ref, cw_ref, cb_ref, wa_ref, ba_ref, wx_ref, bx_ref,
              lam_ref, wout_ref, bout_ref, g1_ref, b1_ref,
              x1_ref, ncs_ref, hl_ref,
              ubuf, a_s, b_s, h_s, *, seq_start, alpha, scan_panels):
    nb, tc, d_model = x_ref.shape
    d_rnn = h0_ref.shape[1]
    heads = wa_ref.shape[0]
    hb = d_rnn // heads
    kw = cw_ref.shape[0]
    hist0 = SUBLANES - (kw - 1)
    rows = nb * tc
    ti = pl.program_id(1)

    @pl.when(ti == 0)
    def _init_state():
        ubuf[:, hist0:SUBLANES, :] = cs_ref[...]
        h_s[...] = h0_ref[...]

    x = x_ref[...].reshape(rows, d_model)
    proj = jnp.dot(x.astype(BF16), win_ref[...], preferred_element_type=F32) + bin_ref[...]
    gate = jax.nn.gelu(proj[:, :d_rnn])
    u = proj[:, d_rnn:]
    ubuf[:, SUBLANES:SUBLANES + tc, :] = u.reshape(nb, tc, d_rnn)
    xc3 = jnp.broadcast_to(cb_ref[...].reshape(1, 1, d_rnn), (nb, tc, d_rnn))
    for j in range(kw):
        xc3 = xc3 + cw_ref[j:j + 1, :].reshape(1, 1, d_rnn) * ubuf[:, hist0 + j:hist0 + j + tc, :]
    tail = ubuf[:, tc + hist0:tc + SUBLANES, :]
    ncs_ref[...] = tail
    ubuf[:, hist0:SUBLANES, :] = tail
    xc = xc3.reshape(rows, d_rnn)

    xcb = xc.astype(BF16)
    ra = jnp.concatenate([jnp.dot(xcb[:, h * hb:(h + 1) * hb], wa_ref[h], preferred_element_type=F32)
                          for h in range(heads)], axis=1)
    ia = jnp.concatenate([jnp.dot(xcb[:, h * hb:(h + 1) * hb], wx_ref[h], preferred_element_type=F32)
                          for h in range(heads)], axis=1)
    r = jax.nn.sigmoid(ra + ba_ref[...])
    gi = jax.nn.sigmoid(ia + bx_ref[...])
    lam = lam_ref[...]
    softplus_neg_lam = jnp.maximum(-lam, 0.0) + jnp.log1p(jnp.exp(-jnp.abs(lam)))
    log_a = (-LRU_C * softplus_neg_lam) * r
    a = jnp.exp(log_a)
    mult = jnp.sqrt(-jnp.tanh(log_a) * (a * a + 1.0))
    if seq_start:
        t_in_seq = lax.broadcasted_iota(jnp.int32, (rows, 1), 0) % tc
        mult = jnp.where((t_in_seq == 0) & (ti == 0), 1.0, mult)
    bterm = xc * gi * mult
    npan = d_rnn // LANES
    seq_stride = a_s.shape[1] // nb
    for p in range(npan):
        for n in range(nb):
            dst = pl.ds(n * seq_stride, tc)
            a_s[p, dst, :] = a[n * tc:(n + 1) * tc, p * LANES:(p + 1) * LANES]
            b_s[p, dst, :] = bterm[n * tc:(n + 1) * tc, p * LANES:(p + 1) * LANES]

    for p0 in range(0, npan, scan_panels):
        group = range(p0, min(p0 + scan_panels, npan))
        hs = [h_s[:, p * LANES:(p + 1) * LANES] for p in group]
        for t in range(tc):
            rs = pl.ds(t, nb, stride=seq_stride)
            for k, p in enumerate(group):
                hs[k] = a_s[p, rs, :] * hs[k] + b_s[p, rs, :]
                b_s[p, rs, :] = hs[k]
        for k, p in enumerate(group):
            h_s[:, p * LANES:(p + 1) * LANES] = hs[k]
    hl_ref[...] = h_s[...]

    h_all = jnp.concatenate(
        [jnp.concatenate([b_s[p, pl.ds(n * seq_stride, tc), :] for n in range(nb)], axis=0) for p in range(npan)],
        axis=1)
    y = jnp.dot((h_all * gate).astype(BF16), wout_ref[...], preferred_element_type=F32) + bout_ref[...]
    x1 = _layer_norm(alpha * x + y, g1_ref[...], b1_ref[...])
    x1_ref[...] = x1.reshape(nb, tc, d_model)


def _lru_mixer(x, conv_state, h0, win, b_in, conv_w, conv_b, wa, ba, wx, bx, lam, wout, bout, g1, b1, *,
               seq_start, alpha):
    n, s, d_model = x.shape
    d_rnn = h0.shape[1]
    kw = conv_w.shape[0]
    tc = min(s, MIXER_ROWS // SUBLANES)
    nb = min(n, MIXER_ROWS // tc)
    rows = nb * tc
    scan_panels = max(1, SCAN_VREGS * SUBLANES // nb)
    seq_stride = tc + SUBLANES if tc % (2 * SUBLANES) == 0 else tc
    row = lambda v: v.reshape(1, -1)
    const2 = lambda i, t: (0, 0)
    const3 = lambda i, t: (0, 0, 0)
    vmem = (2 * 2 * rows * d_model * 4 + 2 * (win.size + wout.size + wa.size + wx.size) * 2
            + (nb * (tc + SUBLANES) + 2 * rows + nb) * d_rnn * 4 + 4 * rows * 2 * d_rnn * 4 + VMEM_HEADROOM_BYTES)
    return pl.pallas_call(
        functools.partial(_lru_body, seq_start=seq_start, alpha=alpha, scan_panels=scan_panels),
        grid=(n // nb, s // tc),
        in_specs=[
            pl.BlockSpec((nb, tc, d_model), lambda i, t: (i, t, 0)),
            pl.BlockSpec((nb, kw - 1, d_rnn), lambda i, t: (i, 0, 0)),
            pl.BlockSpec((nb, d_rnn), lambda i, t: (i, 0)),
            pl.BlockSpec(win.shape, const2), pl.BlockSpec((1, 2 * d_rnn), const2),
            pl.BlockSpec(conv_w.shape, const2), pl.BlockSpec((1, d_rnn), const2),
            pl.BlockSpec(wa.shape, const3), pl.BlockSpec((1, d_rnn), const2),
            pl.BlockSpec(wx.shape, const3), pl.BlockSpec((1, d_rnn), const2),
            pl.BlockSpec((1, d_rnn), const2),
            pl.BlockSpec(wout.shape, const2), pl.BlockSpec((1, d_model), const2),
            pl.BlockSpec((1, d_model), const2), pl.BlockSpec((1, d_model), const2),
        ],
        out_specs=[
            pl.BlockSpec((nb, tc, d_model), lambda i, t: (i, t, 0)),
            pl.BlockSpec((nb, kw - 1, d_rnn), lambda i, t: (i, 0, 0)),
            pl.BlockSpec((nb, d_rnn), lambda i, t: (i, 0)),
        ],
        out_shape=[jax.ShapeDtypeStruct(x.shape, F32),
                   jax.ShapeDtypeStruct((n, kw - 1, d_rnn), F32),
                   jax.ShapeDtypeStruct((n, d_rnn), F32)],
        scratch_shapes=[pltpu.VMEM((nb, tc + SUBLANES, d_rnn), F32),
                        pltpu.VMEM((d_rnn // LANES, nb * seq_stride, LANES), F32),
                        pltpu.VMEM((d_rnn // LANES, nb * seq_stride, LANES), F32),
                        pltpu.VMEM((nb, d_rnn), F32)],
        compiler_params=pltpu.CompilerParams(dimension_semantics=("arbitrary", "arbitrary"),
                                             vmem_limit_bytes=min(vmem, VMEM_LIMIT_CAP_BYTES)),
        name="lru_mixer",
    )(x, conv_state, h0, win, row(b_in), conv_w, row(conv_b), wa, row(ba), wx, row(bx), row(lam),
      wout, row(bout), row(g1), row(b1))


def _ccm_body(x_ref, cs_ref, win_ref, bin_ref, dw_ref, dwb_ref, lng_ref, lnb_ref, wout_ref, bout_ref,
              g1_ref, b1_ref, x1_ref, ncs_ref, gbuf, cbuf, *, alpha, conv_cols):
    nb, tc, d_model = x_ref.shape
    kw, d_conv = dw_ref.shape
    hist_rows = _round_up(kw - 1, SUBLANES)
    hist0 = hist_rows - (kw - 1)
    rows = nb * tc
    ti = pl.program_id(1)

    @pl.when(ti == 0)
    def _init_state():
        gbuf[:, hist0:hist_rows, :] = cs_ref[...]

    x = x_ref[...].reshape(rows, d_model)
    p = jnp.dot(x.astype(BF16), win_ref[...], preferred_element_type=F32) + bin_ref[...]
    glu = p[:, :d_conv] * jax.nn.sigmoid(p[:, d_conv:])
    gbuf[:, hist_rows:hist_rows + tc, :] = glu.reshape(nb, tc, d_conv)

    def conv_seq(n, carry):
        for c0 in range(0, d_conv, conv_cols):
            cols = slice(c0, c0 + conv_cols)
            acc = jnp.broadcast_to(dwb_ref[:, cols], (tc, conv_cols))
            for shift in range(SUBLANES):
                taps = [j for j in range(kw) if (hist0 + j) % SUBLANES == shift]
                if not taps:
                    continue
                rows = tc if shift == 0 else tc + SUBLANES
                part = jnp.zeros((rows, conv_cols), F32)
                for j in taps:
                    start = hist0 + j - shift
                    part = part + dw_ref[j:j + 1, cols] * gbuf[n, start:start + rows, cols]
                acc = acc + part[shift:shift + tc]
            cbuf[n, :, cols] = acc
        return carry
    lax.fori_loop(0, nb, conv_seq, 0)

    tail = gbuf[:, tc + hist0:tc + hist_rows, :]
    ncs_ref[...] = tail
    gbuf[:, hist0:hist_rows, :] = tail

    c = cbuf[...].reshape(rows, d_conv)
    hn = _layer_norm(c, lng_ref[...], lnb_ref[...])
    hdn = hn * jax.nn.sigmoid(hn)
    y = jnp.dot(hdn.astype(BF16), wout_ref[...], preferred_element_type=F32) + bout_ref[...]
    x1 = _layer_norm(alpha * x + y, g1_ref[...], b1_ref[...])
    x1_ref[...] = x1.reshape(nb, tc, d_model)


def _ccm_mixer(x, conv_state, win, b_in, dw_w, dw_b, ln_g, ln_b, wout, bout, g1, b1, *, alpha):
    n, s, d_model = x.shape
    kw, d_conv = dw_w.shape
    tc = min(s, MIXER_ROWS // SUBLANES)
    nb = min(n, MIXER_ROWS // tc)
    while nb > SUBLANES and nb * (kw - 1) * d_conv * 4 > CONV_STATE_BLOCK_BYTES:
        nb //= 2
    rows = nb * tc
    hist_rows = _round_up(kw - 1, SUBLANES)
    conv_cols = min(d_conv, max(LANES, (ACC_VREGS * SUBLANES * LANES // tc) // LANES * LANES))
    while d_conv % conv_cols:
        conv_cols -= LANES
    row = lambda v: v.reshape(1, -1)
    const2 = lambda i, t: (0, 0)
    vmem = (2 * 2 * rows * d_model * 4 + 2 * (win.size + wout.size) * 2
            + (nb * (tc + hist_rows) + rows) * d_conv * 4 + 4 * nb * (kw - 1) * d_conv * 4
            + 4 * rows * 2 * d_conv * 4 + VMEM_HEADROOM_BYTES)
    return pl.pallas_call(
        functools.partial(_ccm_body, alpha=alpha, conv_cols=conv_cols),
        grid=(n // nb, s // tc),
        in_specs=[
            pl.BlockSpec((nb, tc, d_model), lambda i, t: (i, t, 0)),
            pl.BlockSpec((nb, kw - 1, d_conv), lambda i, t: (i, 0, 0)),
            pl.BlockSpec(win.shape, const2), pl.BlockSpec((1, 2 * d_conv), const2),
            pl.BlockSpec(dw_w.shape, const2), pl.BlockSpec((1, d_conv), const2),
            pl.BlockSpec((1, d_conv), const2), pl.BlockSpec((1, d_conv), const2),
            pl.BlockSpec(wout.shape, const2), pl.BlockSpec((1, d_model), const2),
            pl.BlockSpec((1, d_model), const2), pl.BlockSpec((1, d_model), const2),
        ],
        out_specs=[
            pl.BlockSpec((nb, tc, d_model), lambda i, t: (i, t, 0)),
            pl.BlockSpec((nb, kw - 1, d_conv), lambda i, t: (i, 0, 0)),
        ],
        out_shape=[jax.ShapeDtypeStruct(x.shape, F32),
                   jax.ShapeDtypeStruct((n, kw - 1, d_conv), F32)],
        scratch_shapes=[pltpu.VMEM((nb, tc + hist_rows, d_conv), F32),
                        pltpu.VMEM((nb, tc, d_conv), F32)],
        compiler_params=pltpu.CompilerParams(dimension_semantics=("arbitrary", "arbitrary"),
                                             vmem_limit_bytes=min(vmem, VMEM_LIMIT_CAP_BYTES)),
        name="ccm_mixer",
    )(x, conv_state, win, row(b_in), dw_w, row(dw_b), row(ln_g), row(ln_b), wout, row(bout), row(g1), row(b1))


def _first_argmax(v, axis):
    m = jnp.max(v, axis=axis, keepdims=True)
    iota = lax.broadcasted_iota(jnp.int32, v.shape, axis)
    first = jnp.min(jnp.where(v == m, iota, v.shape[axis]), axis=axis, keepdims=True)
    return m, iota == first, first


def _route_body(x_ref, rwt_ref, rb_ref, sg_ref, su_ref, sd_ref, cnt_in_ref,
                resid_ref, idx_ref, w_ref, rank_ref, cnt_ref, tri_s, run_s, *, alpha, steps_per_chunk):
    x = x_ref[...]
    n_exp = rwt_ref.shape[0]
    tr = x.shape[0]
    step = pl.program_id(0)

    @pl.when(step == 0)
    def _build_prefix_matrix():
        r = lax.broadcasted_iota(jnp.int32, (tr, tr), 0)
        c = lax.broadcasted_iota(jnp.int32, (tr, tr), 1)
        tri_s[...] = jnp.where(r <= c, 1.0, 0.0).astype(BF16)

    @pl.when(step % steps_per_chunk == 0)
    def _start_chunk():
        run_s[...] = cnt_in_ref[0]

    logits = lax.dot_general(rwt_ref[...], x, (((1,), (1,)), ((), ())),
                             precision=lax.Precision.HIGHEST, preferred_element_type=F32)
    scores = jax.nn.sigmoid(logits)
    biased = scores + rb_ref[...]
    per_group = n_exp // N_GROUPS
    grp = biased.reshape(N_GROUPS, per_group, tr)
    m1, hit1, _ = _first_argmax(grp, 1)
    m2 = jnp.max(jnp.where(hit1, -jnp.inf, grp), axis=1, keepdims=True)
    gscore = (m1 + m2).reshape(N_GROUPS, tr)
    gsel = jnp.zeros((N_GROUPS, tr), F32)
    for _ in range(TOPK_GROUPS):
        _, hit, _ = _first_argmax(gscore, 0)
        gsel = jnp.where(hit, 1.0, gsel)
        gscore = jnp.where(hit, -jnp.inf, gscore)
    emask = jnp.broadcast_to(gsel.reshape(N_GROUPS, 1, tr), (N_GROUPS, per_group, tr)).reshape(n_exp, tr)
    masked = jnp.where(emask > 0.0, biased, -jnp.inf)
    idx_rows, w_rows, hits = [], [], []
    for _ in range(TOP_K):
        _, hit, first = _first_argmax(masked, 0)
        idx_rows.append(first)
        hits.append(hit)
        w_rows.append(jnp.sum(jnp.where(hit, scores, 0.0), axis=0, keepdims=True))
        masked = jnp.where(hit, -jnp.inf, masked)
    w = jnp.concatenate(w_rows, axis=0)
    idx_ref[...] = jnp.concatenate(idx_rows, axis=0)
    w_ref[...] = w / jnp.sum(w, axis=0, keepdims=True) * ROUTED_SCALE

    sel = jnp.zeros((n_exp, tr), F32)
    for hit in hits:
        sel = jnp.where(hit, 1.0, sel)
    before = jnp.dot(sel.astype(BF16), tri_s[...], preferred_element_type=F32) - sel + run_s[:, 0:1]
    rank_ref[...] = jnp.concatenate(
        [jnp.sum(jnp.where(hit, before, 0.0), axis=0, keepdims=True) for hit in hits], axis=0).astype(jnp.int32)
    run_s[...] = run_s[...] + jnp.sum(sel, axis=1, keepdims=True)
    cnt_ref[0] = run_s[...]

    xb = x.astype(BF16)
    sgate = jnp.dot(xb, sg_ref[...], preferred_element_type=F32)
    sup = jnp.dot(xb, su_ref[...], preferred_element_type=F32)
    hs = (sgate * jax.nn.sigmoid(sgate) * sup).astype(BF16)
    resid_ref[...] = alpha * x + jnp.dot(hs, sd_ref[...], preferred_element_type=F32)


def _route_shared(x, router_wt, router_bias, sg, su, sd, cnt_in, *, alpha):
    t, d_model = x.shape
    n_exp = router_wt.shape[0]
    rows_per_chunk = t // MOE_CHUNKS
    tr = min(rows_per_chunk, ROUTE_ROWS)
    steps_per_chunk = rows_per_chunk // tr
    const2 = lambda i: (0, 0)
    chunk_map = lambda i: (i // steps_per_chunk, 0, 0)
    vmem = 2 * 2 * tr * d_model * 4 + 2 * (router_wt.size * 4 + 3 * sg.size * 2) + VMEM_HEADROOM_BYTES
    return pl.pallas_call(
        functools.partial(_route_body, alpha=alpha, steps_per_chunk=steps_per_chunk),
        grid=(t // tr,),
        in_specs=[pl.BlockSpec((tr, d_model), lambda i: (i, 0)),
                  pl.BlockSpec(router_wt.shape, const2), pl.BlockSpec((n_exp, 1), const2),
                  pl.BlockSpec(sg.shape, const2), pl.BlockSpec(su.shape, const2), pl.BlockSpec(sd.shape, const2),
                  pl.BlockSpec((1, n_exp, LANES), chunk_map)],
        out_specs=[pl.BlockSpec((tr, d_model), lambda i: (i, 0)),
                   pl.BlockSpec((TOP_K, tr), lambda i: (0, i)),
                   pl.BlockSpec((TOP_K, tr), lambda i: (0, i)),
                   pl.BlockSpec((TOP_K, tr), lambda i: (0, i)),
                   pl.BlockSpec((1, n_exp, LANES), chunk_map)],
        out_shape=[jax.ShapeDtypeStruct((t, d_model), F32),
                   jax.ShapeDtypeStruct((TOP_K, t), jnp.int32),
                   jax.ShapeDtypeStruct((TOP_K, t), F32),
                   jax.ShapeDtypeStruct((TOP_K, t), jnp.int32),
                   jax.ShapeDtypeStruct((MOE_CHUNKS, n_exp, LANES), F32)],
        scratch_shapes=[pltpu.VMEM((tr, tr), BF16), pltpu.VMEM((n_exp, LANES), F32)],
        compiler_params=pltpu.CompilerParams(dimension_semantics=("arbitrary",),
                                             vmem_limit_bytes=min(vmem, VMEM_LIMIT_CAP_BYTES)),
        name="route_shared",
    )(x, router_wt, router_bias.reshape(n_exp, 1), sg, su, sd, cnt_in)


def _moe_body(seg_cnt_ref, seg_base_ref,
              wg_ref, wu_ref, wd_ref, g2_ref, b2_ref,
              tok_hbm, wts_hbm, xp_hbm, xs_hbm, rp_hbm, rs_hbm,
              op_hbm, os_hbm,
              x_s, y_s, xt_s, ot_s, tok_sm, wts_sm, sem_in, sem_out, sem_sm,
              *, tm, cp_rows, cs_rows, npan):
    chunk = cp_rows + cs_rows
    chunk_stride = chunk + SUBLANES
    half = tm // 2
    tile_stride = tm + SUBLANES
    n_exp = seg_cnt_ref.shape[0] // MOE_CHUNKS
    seg = pl.program_id(0)
    c = seg // n_exp
    chunk_first = seg % n_exp == 0
    chunk_last = seg % n_exp == n_exp - 1
    n_rows = seg_cnt_ref[seg]
    seg_row0 = seg_base_ref[seg]
    n_whole = jnp.maximum(n_rows - half + tm - 1, 0) // tm
    rows_after = n_rows - n_whole * tm

    def chunk_copies(hbm_p, hbm_s, vmem, sem, to_vmem):
        copies = []
        for p in range(npan):
            for hbm, nrows, row0 in ((hbm_p, cp_rows, 0), (hbm_s, cs_rows, cp_rows)):
                h = hbm.at[pl.ds(c * nrows, nrows), pl.ds(p * LANES, LANES)]
                v = vmem.at[pl.ds(p * chunk_stride + row0, nrows), :]
                copies.append(pltpu.make_async_copy(h, v, sem.at[0]) if to_vmem
                              else pltpu.make_async_copy(v, h, sem.at[0]))
        return copies

    @pl.when(chunk_first)
    def _load_chunk():
        copies = (chunk_copies(xp_hbm, xs_hbm, x_s, sem_in, True)
                  + chunk_copies(rp_hbm, rs_hbm, y_s, sem_in, True))
        for cp in copies:
            cp.start()
        zeros = jnp.zeros((SUBLANES, LANES), F32)
        for p in range(npan):
            x_s[pl.ds(p * chunk_stride + chunk, SUBLANES), :] = zeros
            y_s[pl.ds(p * chunk_stride + chunk, SUBLANES), :] = zeros
        list_rows = tok_sm.shape[0]
        lists = (pltpu.make_async_copy(tok_hbm.at[pl.ds(c * list_rows, list_rows)], tok_sm, sem_sm.at[0]),
                 pltpu.make_async_copy(wts_hbm.at[pl.ds(c * list_rows, list_rows)], wts_sm, sem_sm.at[1]))
        for cp in lists:
            cp.start()
        for cp in lists + tuple(copies):
            cp.wait()

    def process_rows(first, n):
        for i in range(n):
            t = tok_sm[first + i]
            xt_s[pl.ds(i, npan, stride=tile_stride), :] = x_s[pl.ds(t, npan, stride=chunk_stride), :]

        lhs = jnp.concatenate([xt_s[pl.ds(p * tile_stride, n), :] for p in range(npan)], axis=1).astype(BF16)
        g = jnp.dot(lhs, wg_ref[0, 0], preferred_element_type=F32)
        u = jnp.dot(lhs, wu_ref[0, 0], preferred_element_type=F32)
        hidden = (g * jax.nn.sigmoid(g) * u).astype(BF16)
        o = jnp.dot(hidden, wd_ref[0, 0], preferred_element_type=F32)
        for p in range(npan):
            ot_s[pl.ds(p * tile_stride, n), :] = o[:, p * LANES:(p + 1) * LANES]

        for i0 in range(0, n, SCATTER_UNROLL):
            toks, rows_new = [], []
            for i in range(i0, i0 + SCATTER_UNROLL):
                t = tok_sm[first + i]
                w = wts_sm[first + i]
                row = ot_s[pl.ds(i, npan, stride=tile_stride), :]
                toks.append(t)
                rows_new.append(y_s[pl.ds(t, npan, stride=chunk_stride), :] + w * row)
            for t, new in zip(toks, rows_new):
                y_s[pl.ds(t, npan, stride=chunk_stride), :] = new

    def whole_tile(i, carry):
        process_rows(seg_row0 + i * tm, tm)
        return carry
    lax.fori_loop(0, n_whole, whole_tile, 0)

    @pl.when(rows_after > 0)
    def _half_tile():
        process_rows(seg_row0 + n_whole * tm, half)

    @pl.when(chunk_last)
    def _finish_chunk():
        blk = MOE_TILE
        g2 = g2_ref[...]
        b2 = b2_ref[...]

        def ln_block(rb, carry):
            r0 = pl.multiple_of(rb * blk, blk)
            v = jnp.concatenate([y_s[pl.ds(p * chunk_stride + r0, blk), :] for p in range(npan)], axis=1)
            out = _layer_norm(v, g2, b2)
            for p in range(npan):
                y_s[pl.ds(p * chunk_stride + r0, blk), :] = out[:, p * LANES:(p + 1) * LANES]
            return carry
        lax.fori_loop(0, chunk // blk, ln_block, 0)
        copies = chunk_copies(op_hbm, os_hbm, y_s, sem_out, False)
        for cp in copies:
            cp.start()
        for cp in copies:
            cp.wait()


def _list_rows(chunk, top_k, n_exp):
    return (chunk * top_k) // MOE_TILE * MOE_TILE + n_exp * MOE_TILE


def _build_row_lists(dest, wsel, *, chunk, top_k, n_exp):
    n_pairs = top_k * chunk
    list_rows = _list_rows(chunk, top_k, n_exp)
    n_jobs = 2 * MOE_CHUNKS
    assert n_jobs <= SC_CORES * SC_SUBCORES and n_pairs % SC_LANES == 0 and list_rows % SC_LANES == 0
    tokens = jnp.tile(jnp.arange(chunk, dtype=jnp.int32), MOE_CHUNKS * top_k)
    vals = jnp.concatenate([tokens, lax.bitcast_convert_type(wsel, jnp.int32)])
    pad = jnp.concatenate([jnp.full((SC_LANES,), chunk, jnp.int32), jnp.zeros((SC_LANES,), jnp.int32)])
    mesh = plsc.VectorSubcoreMesh(core_axis_name="c", subcore_axis_name="s",
                                  num_cores=SC_CORES, num_subcores=SC_SUBCORES)

    def body(dest_hbm, vals_hbm, pad_hbm, out_hbm, dest_v, vals_v, list_v, pad_v):
        job = lax.axis_index("s") * SC_CORES + lax.axis_index("c")

        @pl.when(job < n_jobs)
        def _scatter_one_list():
            pltpu.sync_copy(dest_hbm.at[pl.ds((job % MOE_CHUNKS) * n_pairs, n_pairs)], dest_v)
            pltpu.sync_copy(vals_hbm.at[pl.ds(job * n_pairs, n_pairs)], vals_v)
            pltpu.sync_copy(pad_hbm.at[pl.ds((job // MOE_CHUNKS) * SC_LANES, SC_LANES)], pad_v)
            pad_vec = pad_v[...]

            @pl.loop(0, list_rows // SC_LANES)
            def _fill(i):
                list_v[pl.ds(i * SC_LANES, SC_LANES)] = pad_vec

            @pl.loop(0, n_pairs // SC_LANES)
            def _scatter(i):
                lanes = pl.ds(i * SC_LANES, SC_LANES)
                plsc.store_scatter(list_v, [dest_v[lanes]], vals_v[lanes])

            pltpu.sync_copy(list_v, out_hbm.at[pl.ds(job * list_rows, list_rows)])

    lists = pl.kernel(
        body, mesh=mesh,
        out_type=jax.ShapeDtypeStruct((n_jobs * list_rows,), jnp.int32),
        scratch_types=[pltpu.VMEM((n_pairs,), jnp.int32), pltpu.VMEM((n_pairs,), jnp.int32),
                       pltpu.VMEM((list_rows,), jnp.int32), pltpu.VMEM((SC_LANES,), jnp.int32)],
        compiler_params=pltpu.CompilerParams(needs_layout_passes=False),
        name="build_row_lists",
    )(dest, vals, pad)
    half = MOE_CHUNKS * list_rows
    return lists[:half], lax.bitcast_convert_type(lists[half:], F32)


def _moe_routed(seg_cnt, seg_base, tok_list, wts_list, wg, wu, wd, g2, b2, xp, xs, rp, rs, *, layer):
    _, n_exp, d_model, d_ff = wg.shape
    tm = MOE_TILE
    npan = d_model // LANES
    assert npan == SUBLANES, "one token row must fill exactly one (8, 128) register"
    cp_rows, cs_rows = xp.shape[0] // MOE_CHUNKS, xs.shape[0] // MOE_CHUNKS
    chunk = cp_rows + cs_rows
    assert chunk % MOE_TILE == 0 and cs_rows % SUBLANES == 0
    chunk_stride = chunk + SUBLANES
    tile_stride = tm + SUBLANES
    list_rows = tok_list.shape[0] // MOE_CHUNKS
    any_spec = pl.BlockSpec(memory_space=pl.ANY)
    expert_map = lambda s, *_: (layer, s % n_exp, 0, 0)
    const2 = lambda s, *_: (0, 0)
    expert_spec = lambda shape: pl.BlockSpec((1, 1) + shape, expert_map)
    vmem = (2 * npan * chunk_stride * LANES * 4 + 2 * npan * tile_stride * LANES * 4
            + WEIGHT_BUFFERS * 3 * d_model * d_ff * wg.dtype.itemsize + VMEM_HEADROOM_BYTES)
    return pl.pallas_call(
        functools.partial(_moe_body, tm=tm, cp_rows=cp_rows, cs_rows=cs_rows, npan=npan),
        grid_spec=pltpu.PrefetchScalarGridSpec(
            num_scalar_prefetch=2,
            grid=(MOE_CHUNKS * n_exp,),
            in_specs=[
                expert_spec((d_model, d_ff)),
                expert_spec((d_model, d_ff)),
                expert_spec((d_ff, d_model)),
                pl.BlockSpec((1, d_model), const2),
                pl.BlockSpec((1, d_model), const2),
                any_spec, any_spec, any_spec, any_spec, any_spec, any_spec,
            ],
            out_specs=[any_spec, any_spec],
            scratch_shapes=[
                pltpu.VMEM((npan * chunk_stride, LANES), F32),
                pltpu.VMEM((npan * chunk_stride, LANES), F32),
                pltpu.VMEM((npan * tile_stride, LANES), F32),
                pltpu.VMEM((npan * tile_stride, LANES), F32),
                pltpu.SMEM((list_rows,), jnp.int32),
                pltpu.SMEM((list_rows,), F32),
                pltpu.SemaphoreType.DMA((1,)),
                pltpu.SemaphoreType.DMA((1,)),
                pltpu.SemaphoreType.DMA((2,)),
            ]),
        out_shape=[jax.ShapeDtypeStruct(xp.shape, F32), jax.ShapeDtypeStruct(xs.shape, F32)],
        compiler_params=pltpu.CompilerParams(dimension_semantics=("arbitrary",),
                                             vmem_limit_bytes=min(vmem, VMEM_LIMIT_CAP_BYTES)),
        name="moe_routed",
    )(seg_cnt, seg_base, wg, wu, wd, g2.reshape(1, d_model), b2.reshape(1, d_model),
      tok_list, wts_list, xp, xs, rp, rs)


def _plan_tiles(cnt, idx_p, w_p, rank_p, idx_s, w_s, rank_s):
    k, tp = idx_p.shape
    ts = idx_s.shape[1]
    n_chunks, n_exp = cnt.shape
    tm = MOE_TILE
    cp, cs = tp // n_chunks, ts // n_chunks
    experts = jnp.arange(n_exp, dtype=jnp.int32)

    seg_rows = (cnt + tm - 1) // tm * tm
    seg_base = jnp.sum(jnp.where(experts[None, :] < experts[:, None], seg_rows[:, None, :], 0), axis=-1)

    def rows_of(idx, rank, per_chunk):
        idx3 = idx.reshape(k, n_chunks, per_chunk)
        base = jnp.sum(jnp.where(idx3[..., None] == experts, seg_base[None, :, None, :], 0), axis=-1)
        return jnp.transpose(base + rank.reshape(k, n_chunks, per_chunk), (1, 0, 2))
    by_chunk = lambda a, per_chunk: jnp.transpose(a.reshape(k, n_chunks, per_chunk), (1, 0, 2))
    dest = jnp.concatenate([rows_of(idx_p, rank_p, cp), rows_of(idx_s, rank_s, cs)], axis=2).reshape(-1)
    wsel = jnp.concatenate([by_chunk(w_p, cp), by_chunk(w_s, cs)], axis=2).reshape(-1)
    return cnt.reshape(-1), seg_base.reshape(-1), dest, wsel


def kernel(x_prompt, x_sample, state_lru_conv, state_lru_h, state_ccm_conv, lru_w_in, lru_b_in, lru_conv_w, lru_conv_b, lru_w_a, lru_b_a, lru_w_x, lru_b_x, lru_lambda, lru_w_out, lru_b_out, ccm_w_in, ccm_b_in, ccm_dw_w, ccm_dw_b, ccm_ln_g, ccm_ln_b, ccm_w_out, ccm_b_out, ln1_g, ln1_b, ln2_g, ln2_b, router_w, router_bias, exp_w_gate, exp_w_up, exp_w_down, sh_w_gate, sh_w_up, sh_w_down):
    depth = ln1_g.shape[0]
    alpha = (2 * depth) ** 0.25
    n_exp = router_w.shape[2]
    bp, sp, d_model = x_prompt.shape
    bs, ss, _ = x_sample.shape
    kw_lru = lru_conv_w.shape[1]
    kw_ccm = ccm_dw_w.shape[1]
    d_rnn = lru_conv_w.shape[2]
    d_conv = ccm_dw_w.shape[2]
    bf = lambda a: a.astype(BF16)

    zero_lru_conv = jnp.zeros((bp, kw_lru - 1, d_rnn), F32)
    zero_lru_h = jnp.zeros((bp, d_rnn), F32)
    zero_ccm_conv = jnp.zeros((bp, kw_ccm - 1, d_conv), F32)

    exp_gate_bf, exp_up_bf, exp_down_bf = bf(exp_w_gate), bf(exp_w_up), bf(exp_w_down)

    xp, xs = x_prompt, x_sample
    lru_conv_p, lru_h_p, ccm_conv_p = [], [], []
    lru_conv_s, lru_h_s, ccm_conv_s = [], [], []
    for layer in range(depth):
        j = layer // 2
        if layer % 2 == 0:
            weights = (bf(lru_w_in[j]), lru_b_in[j], lru_conv_w[j], lru_conv_b[j], bf(lru_w_a[j]), lru_b_a[j],
                       bf(lru_w_x[j]), lru_b_x[j], lru_lambda[j], bf(lru_w_out[j]), lru_b_out[j],
                       ln1_g[layer], ln1_b[layer])
            xp, cb, hl = _lru_mixer(xp, zero_lru_conv, zero_lru_h, *weights, seq_start=True, alpha=alpha)
            lru_conv_p.append(cb)
            lru_h_p.append(hl)
            xs, cb, hl = _lru_mixer(xs, state_lru_conv[j], state_lru_h[j], *weights, seq_start=False, alpha=alpha)
            lru_conv_s.append(cb)
            lru_h_s.append(hl)
        else:
            weights = (bf(ccm_w_in[j]), ccm_b_in[j], ccm_dw_w[j], ccm_dw_b[j], ccm_ln_g[j], ccm_ln_b[j],
                       bf(ccm_w_out[j]), ccm_b_out[j], ln1_g[layer], ln1_b[layer])
            xp, cb = _ccm_mixer(xp, zero_ccm_conv, *weights, alpha=alpha)
            ccm_conv_p.append(cb)
            xs, cb = _ccm_mixer(xs, state_ccm_conv[j], *weights, alpha=alpha)
            ccm_conv_s.append(cb)

        shared = (router_w[layer].T, router_bias[layer], bf(sh_w_gate[layer]), bf(sh_w_up[layer]),
                  bf(sh_w_down[layer]))
        xp2 = xp.reshape(bp * sp, d_model)
        xs2 = xs.reshape(bs * ss, d_model)
        no_tokens = jnp.zeros((MOE_CHUNKS, n_exp, LANES), F32)
        resid_p, idx_p, w_p, rank_p, cnt_p = _route_shared(xp2, *shared, no_tokens, alpha=alpha)
        resid_s, idx_s, w_s, rank_s, cnt = _route_shared(xs2, *shared, cnt_p, alpha=alpha)
        seg_cnt, seg_base, dest, wsel = _plan_tiles(cnt[:, :, 0].astype(jnp.int32), idx_p, w_p, rank_p,
                                                    idx_s, w_s, rank_s)
        tok_list, wts_list = _build_row_lists(dest, wsel, chunk=(xp2.shape[0] + xs2.shape[0]) // MOE_CHUNKS,
                                              top_k=idx_p.shape[0], n_exp=n_exp)
        xp2, xs2 = _moe_routed(seg_cnt, seg_base, tok_list, wts_list, exp_gate_bf, exp_up_bf, exp_down_bf,
                               ln2_g[layer], ln2_b[layer], xp2, xs2, resid_p, resid_s, layer=layer)
        xp = xp2.reshape(bp, sp, d_model)
        xs = xs2.reshape(bs, ss, d_model)

    return (xp, xs, jnp.stack(lru_conv_p), jnp.stack(lru_h_p), jnp.stack(ccm_conv_p),
            jnp.stack(lru_conv_s), jnp.stack(lru_h_s), jnp.stack(ccm_conv_s))
```

```python
import functools

import jax
import jax.numpy as jnp
from jax import lax
from jax.experimental import pallas as pl
from jax.experimental.pallas import tpu as pltpu
from jax.experimental.pallas import tpu_sc as plsc

LANES = 128
SUBLANES = 8
SC_CORES = 2
SC_SUBCORES = 16
SC_LANES = 16
VMEM_HEADROOM_BYTES = 8 << 20
VMEM_LIMIT_CAP_BYTES = 56 << 20
CONV_STATE_BLOCK_BYTES = 4 << 20

LN_EPS = 1e-5
LRU_C = 8.0
N_GROUPS = 8
TOPK_GROUPS = 4
TOP_K = 8
ROUTED_SCALE = 2.5

MIXER_ROWS = 512
ROUTE_ROWS = 512
MOE_TILE = 256
MOE_BIG_TILE = 512
MOE_UNIT = 64
MOE_CHUNKS = 4
SCATTER_UNROLL = 8
WEIGHT_BUFFERS = 2
ACC_VREGS = 16
SCAN_VREGS = 16

F32 = jnp.float32
BF16 = jnp.bfloat16


def _layer_norm(v, g, b):
    mu = jnp.mean(v, axis=-1, keepdims=True)
    d = v - mu
    var = jnp.mean(d * d, axis=-1, keepdims=True)
    return d * lax.rsqrt(var + LN_EPS) * g + b


def _round_up(n, m):
    return (n + m - 1) // m * m


def _lru_body(x_ref, cs_ref, h0_ref, win_ref, bin_ref, cw_ref, cb_ref, wa_ref, ba_ref, wx_ref, bx_ref,
              lam_ref, wout_ref, bout_ref, g1_ref, b1_ref,
              x1_ref, ncs_ref, hl_ref,
              ubuf, a_s, b_s, h_s, *, seq_start, alpha, scan_panels):
    nb, tc, d_model = x_ref.shape
    d_rnn = h0_ref.shape[1]
    heads = wa_ref.shape[0]
    hb = d_rnn // heads
    kw = cw_ref.shape[0]
    hist0 = SUBLANES - (kw - 1)
    rows = nb * tc
    ti = pl.program_id(1)

    @pl.when(ti == 0)
    def _init_state():
        ubuf[:, hist0:SUBLANES, :] = cs_ref[...]
        h_s[...] = h0_ref[...]

    x = x_ref[...].reshape(rows, d_model)
    proj = jnp.dot(x.astype(BF16), win_ref[...], preferred_element_type=F32) + bin_ref[...]
    gate = jax.nn.gelu(proj[:, :d_rnn])
    u = proj[:, d_rnn:]
    ubuf[:, SUBLANES:SUBLANES + tc, :] = u.reshape(nb, tc, d_rnn)
    xc3 = jnp.broadcast_to(cb_ref[...].reshape(1, 1, d_rnn), (nb, tc, d_rnn))
    for j in range(kw):
        xc3 = xc3 + cw_ref[j:j + 1, :].reshape(1, 1, d_rnn) * ubuf[:, hist0 + j:hist0 + j + tc, :]
    tail = ubuf[:, tc + hist0:tc + SUBLANES, :]
    ncs_ref[...] = tail
    ubuf[:, hist0:SUBLANES, :] = tail
    xc = xc3.reshape(rows, d_rnn)

    xcb = xc.astype(BF16)
    ra = jnp.concatenate([jnp.dot(xcb[:, h * hb:(h + 1) * hb], wa_ref[h], preferred_element_type=F32)
                          for h in range(heads)], axis=1)
    ia = jnp.concatenate([jnp.dot(xcb[:, h * hb:(h + 1) * hb], wx_ref[h], preferred_element_type=F32)
                          for h in range(heads)], axis=1)
    r = jax.nn.sigmoid(ra + ba_ref[...])
    gi = jax.nn.sigmoid(ia + bx_ref[...])
    lam = lam_ref[...]
    softplus_neg_lam = jnp.maximum(-lam, 0.0) + jnp.log1p(jnp.exp(-jnp.abs(lam)))
    log_a = (-LRU_C * softplus_neg_lam) * r
    a = jnp.exp(log_a)
    mult = jnp.sqrt(-jnp.tanh(log_a) * (a * a + 1.0))
    if seq_start:
        t_in_seq = lax.broadcasted_iota(jnp.int32, (rows, 1), 0) % tc
        mult = jnp.where((t_in_seq == 0) & (ti == 0), 1.0, mult)
    bterm = xc * gi * mult
    npan = d_rnn // LANES
    seq_stride = a_s.shape[1] // nb
    for p in range(npan):
        for n in range(nb):
            dst = pl.ds(n * seq_stride, tc)
            a_s[p, dst, :] = a[n * tc:(n + 1) * tc, p * LANES:(p + 1) * LANES]
            b_s[p, dst, :] = bterm[n * tc:(n + 1) * tc, p * LANES:(p + 1) * LANES]

    for p0 in range(0, npan, scan_panels):
        group = range(p0, min(p0 + scan_panels, npan))
        hs = [h_s[:, p * LANES:(p + 1) * LANES] for p in group]
        for t in range(tc):
            rs = pl.ds(t, nb, stride=seq_stride)
            for k, p in enumerate(group):
                hs[k] = a_s[p, rs, :] * hs[k] + b_s[p, rs, :]
                b_s[p, rs, :] = hs[k]
        for k, p in enumerate(group):
            h_s[:, p * LANES:(p + 1) * LANES] = hs[k]
    hl_ref[...] = h_s[...]

    h_all = jnp.concatenate(
        [jnp.concatenate([b_s[p, pl.ds(n * seq_stride, tc), :] for n in range(nb)], axis=0) for p in range(npan)],
        axis=1)
    y = jnp.dot((h_all * gate).astype(BF16), wout_ref[...], preferred_element_type=F32) + bout_ref[...]
    x1 = _layer_norm(alpha * x + y, g1_ref[...], b1_ref[...])
    x1_ref[...] = x1.reshape(nb, tc, d_model)


def _lru_mixer(x, conv_state, h0, win, b_in, conv_w, conv_b, wa, ba, wx, bx, lam, wout, bout, g1, b1, *,
               seq_start, alpha):
    n, s, d_model = x.shape
    d_rnn = h0.shape[1]
    kw = conv_w.shape[0]
    tc = min(s, MIXER_ROWS // SUBLANES)
    nb = min(n, MIXER_ROWS // tc)
    rows = nb * tc
    scan_panels = max(1, SCAN_VREGS * SUBLANES // nb)
    seq_stride = tc + SUBLANES if tc % (2 * SUBLANES) == 0 else tc
    row = lambda v: v.reshape(1, -1)
    const2 = lambda i, t: (0, 0)
    const3 = lambda i, t: (0, 0, 0)
    vmem = (2 * 2 * rows * d_model * 4 + 2 * (win.size + wout.size + wa.size + wx.size) * 2
            + (nb * (tc + SUBLANES) + 2 * rows + nb) * d_rnn * 4 + 4 * rows * 2 * d_rnn * 4 + VMEM_HEADROOM_BYTES)
    return pl.pallas_call(
        functools.partial(_lru_body, seq_start=seq_start, alpha=alpha, scan_panels=scan_panels),
        grid=(n // nb, s // tc),
        in_specs=[
            pl.BlockSpec((nb, tc, d_model), lambda i, t: (i, t, 0)),
            pl.BlockSpec((nb, kw - 1, d_rnn), lambda i, t: (i, 0, 0)),
            pl.BlockSpec((nb, d_rnn), lambda i, t: (i, 0)),
            pl.BlockSpec(win.shape, const2), pl.BlockSpec((1, 2 * d_rnn), const2),
            pl.BlockSpec(conv_w.shape, const2), pl.BlockSpec((1, d_rnn), const2),
            pl.BlockSpec(wa.shape, const3), pl.BlockSpec((1, d_rnn), const2),
            pl.BlockSpec(wx.shape, const3), pl.BlockSpec((1, d_rnn), const2),
            pl.BlockSpec((1, d_rnn), const2),
            pl.BlockSpec(wout.shape, const2), pl.BlockSpec((1, d_model), const2),
            pl.BlockSpec((1, d_model), const2), pl.BlockSpec((1, d_model), const2),
        ],
        out_specs=[
            pl.BlockSpec((nb, tc, d_model), lambda i, t: (i, t, 0)),
            pl.BlockSpec((nb, kw - 1, d_rnn), lambda i, t: (i, 0, 0)),
            pl.BlockSpec((nb, d_rnn), lambda i, t: (i, 0)),
        ],
        out_shape=[jax.ShapeDtypeStruct(x.shape, F32),
                   jax.ShapeDtypeStruct((n, kw - 1, d_rnn), F32),
                   jax.ShapeDtypeStruct((n, d_rnn), F32)],
        scratch_shapes=[pltpu.VMEM((nb, tc + SUBLANES, d_rnn), F32),
                        pltpu.VMEM((d_rnn // LANES, nb * seq_stride, LANES), F32),
                        pltpu.VMEM((d_rnn // LANES, nb * seq_stride, LANES), F32),
                        pltpu.VMEM((nb, d_rnn), F32)],
        compiler_params=pltpu.CompilerParams(dimension_semantics=("arbitrary", "arbitrary"),
                                             vmem_limit_bytes=min(vmem, VMEM_LIMIT_CAP_BYTES)),
        name="lru_mixer",
    )(x, conv_state, h0, win, row(b_in), conv_w, row(conv_b), wa, row(ba), wx, row(bx), row(lam),
      wout, row(bout), row(g1), row(b1))


def _ccm_body(x_ref, cs_ref, win_ref, bin_ref, dw_ref, dwb_ref, lng_ref, lnb_ref, wout_ref, bout_ref,
              g1_ref, b1_ref, x1_ref, ncs_ref, gbuf, cbuf, *, alpha, conv_cols):
    nb, tc, d_model = x_ref.shape
    kw, d_conv = dw_ref.shape
    hist_rows = _round_up(kw - 1, SUBLANES)
    hist0 = hist_rows - (kw - 1)
    rows = nb * tc
    ti = pl.program_id(1)

    @pl.when(ti == 0)
    def _init_state():
        gbuf[:, hist0:hist_rows, :] = cs_ref[...]

    x = x_ref[...].reshape(rows, d_model)
    p = jnp.dot(x.astype(BF16), win_ref[...], preferred_element_type=F32) + bin_ref[...]
    glu = p[:, :d_conv] * jax.nn.sigmoid(p[:, d_conv:])
    gbuf[:, hist_rows:hist_rows + tc, :] = glu.reshape(nb, tc, d_conv)

    def conv_seq(n, carry):
        for c0 in range(0, d_conv, conv_cols):
            cols = slice(c0, c0 + conv_cols)
            acc = jnp.broadcast_to(dwb_ref[:, cols], (tc, conv_cols))
            for shift in range(SUBLANES):
                taps = [j for j in range(kw) if (hist0 + j) % SUBLANES == shift]
                if not taps:
                    continue
                rows = tc if shift == 0 else tc + SUBLANES
                part = jnp.zeros((rows, conv_cols), F32)
                for j in taps:
                    start = hist0 + j - shift
                    part = part + dw_ref[j:j + 1, cols] * gbuf[n, start:start + rows, cols]
                acc = acc + part[shift:shift + tc]
            cbuf[n, :, cols] = acc
        return carry
    lax.fori_loop(0, nb, conv_seq, 0)

    tail = gbuf[:, tc + hist0:tc + hist_rows, :]
    ncs_ref[...] = tail
    gbuf[:, hist0:hist_rows, :] = tail

    c = cbuf[...].reshape(rows, d_conv)
    hn = _layer_norm(c, lng_ref[...], lnb_ref[...])
    hdn = hn * jax.nn.sigmoid(hn)
    y = jnp.dot(hdn.astype(BF16), wout_ref[...], preferred_element_type=F32) + bout_ref[...]
    x1 = _layer_norm(alpha * x + y, g1_ref[...], b1_ref[...])
    x1_ref[...] = x1.reshape(nb, tc, d_model)


def _ccm_mixer(x, conv_state, win, b_in, dw_w, dw_b, ln_g, ln_b, wout, bout, g1, b1, *, alpha):
    n, s, d_model = x.shape
    kw, d_conv = dw_w.shape
    tc = min(s, MIXER_ROWS // SUBLANES)
    nb = min(n, MIXER_ROWS // tc)
    while nb > SUBLANES and nb * (kw - 1) * d_conv * 4 > CONV_STATE_BLOCK_BYTES:
        nb //= 2
    rows = nb * tc
    hist_rows = _round_up(kw - 1, SUBLANES)
    conv_cols = min(d_conv, max(LANES, (ACC_VREGS * SUBLANES * LANES // tc) // LANES * LANES))
    while d_conv % conv_cols:
        conv_cols -= LANES
    row = lambda v: v.reshape(1, -1)
    const2 = lambda i, t: (0, 0)
    vmem = (2 * 2 * rows * d_model * 4 + 2 * (win.size + wout.size) * 2
            + (nb * (tc + hist_rows) + rows) * d_conv * 4 + 4 * nb * (kw - 1) * d_conv * 4
            + 4 * rows * 2 * d_conv * 4 + VMEM_HEADROOM_BYTES)
    return pl.pallas_call(
        functools.partial(_ccm_body, alpha=alpha, conv_cols=conv_cols),
        grid=(n // nb, s // tc),
        in_specs=[
            pl.BlockSpec((nb, tc, d_model), lambda i, t: (i, t, 0)),
            pl.BlockSpec((nb, kw - 1, d_conv), lambda i, t: (i, 0, 0)),
            pl.BlockSpec(win.shape, const2), pl.BlockSpec((1, 2 * d_conv), const2),
            pl.BlockSpec(dw_w.shape, const2), pl.BlockSpec((1, d_conv), const2),
            pl.BlockSpec((1, d_conv), const2), pl.BlockSpec((1, d_conv), const2),
            pl.BlockSpec(wout.shape, const2), pl.BlockSpec((1, d_model), const2),
            pl.BlockSpec((1, d_model), const2), pl.BlockSpec((1, d_model), const2),
        ],
        out_specs=[
            pl.BlockSpec((nb, tc, d_model), lambda i, t: (i, t, 0)),
            pl.BlockSpec((nb, kw - 1, d_conv), lambda i, t: (i, 0, 0)),
        ],
        out_shape=[jax.ShapeDtypeStruct(x.shape, F32),
                   jax.ShapeDtypeStruct((n, kw - 1, d_conv), F32)],
        scratch_shapes=[pltpu.VMEM((nb, tc + hist_rows, d_conv), F32),
                        pltpu.VMEM((nb, tc, d_conv), F32)],
        compiler_params=pltpu.CompilerParams(dimension_semantics=("arbitrary", "arbitrary"),
                                             vmem_limit_bytes=min(vmem, VMEM_LIMIT_CAP_BYTES)),
        name="ccm_mixer",
    )(x, conv_state, win, row(b_in), dw_w, row(dw_b), row(ln_g), row(ln_b), wout, row(bout), row(g1), row(b1))


def _first_argmax(v, axis):
    m = jnp.max(v, axis=axis, keepdims=True)
    iota = lax.broadcasted_iota(jnp.int32, v.shape, axis)
    first = jnp.min(jnp.where(v == m, iota, v.shape[axis]), axis=axis, keepdims=True)
    return m, iota == first, first


def _route_body(x_ref, rwt_ref, rb_ref, sg_ref, su_ref, sd_ref, cnt_in_ref,
                resid_ref, idx_ref, w_ref, rank_ref, cnt_ref, tri_s, run_s, *, alpha, steps_per_chunk):
    x = x_ref[...]
    n_exp = rwt_ref.shape[0]
    tr = x.shape[0]
    step = pl.program_id(0)

    @pl.when(step == 0)
    def _build_prefix_matrix():
        r = lax.broadcasted_iota(jnp.int32, (tr, tr), 0)
        c = lax.broadcasted_iota(jnp.int32, (tr, tr), 1)
        tri_s[...] = jnp.where(r <= c, 1.0, 0.0).astype(BF16)

    @pl.when(step % steps_per_chunk == 0)
    def _start_chunk():
        run_s[...] = cnt_in_ref[0]

    logits = lax.dot_general(rwt_ref[...], x, (((1,), (1,)), ((), ())),
                             precision=lax.Precision.HIGHEST, preferred_element_type=F32)
    scores = jax.nn.sigmoid(logits)
    biased = scores + rb_ref[...]
    per_group = n_exp // N_GROUPS
    grp = biased.reshape(N_GROUPS, per_group, tr)
    m1, hit1, _ = _first_argmax(grp, 1)
    m2 = jnp.max(jnp.where(hit1, -jnp.inf, grp), axis=1, keepdims=True)
    gscore = (m1 + m2).reshape(N_GROUPS, tr)
    gsel = jnp.zeros((N_GROUPS, tr), F32)
    for _ in range(TOPK_GROUPS):
        _, hit, _ = _first_argmax(gscore, 0)
        gsel = jnp.where(hit, 1.0, gsel)
        gscore = jnp.where(hit, -jnp.inf, gscore)
    emask = jnp.broadcast_to(gsel.reshape(N_GROUPS, 1, tr), (N_GROUPS, per_group, tr)).reshape(n_exp, tr)
    masked = jnp.where(emask > 0.0, biased, -jnp.inf)
    idx_rows, w_rows, hits = [], [], []
    for _ in range(TOP_K):
        _, hit, first = _first_argmax(masked, 0)
        idx_rows.append(first)
        hits.append(hit)
        w_rows.append(jnp.sum(jnp.where(hit, scores, 0.0), axis=0, keepdims=True))
        masked = jnp.where(hit, -jnp.inf, masked)
    w = jnp.concatenate(w_rows, axis=0)
    idx_ref[...] = jnp.concatenate(idx_rows, axis=0)
    w_ref[...] = w / jnp.sum(w, axis=0, keepdims=True) * ROUTED_SCALE

    sel = jnp.zeros((n_exp, tr), F32)
    for hit in hits:
        sel = jnp.where(hit, 1.0, sel)
    before = jnp.dot(sel.astype(BF16), tri_s[...], preferred_element_type=F32) - sel + run_s[:, 0:1]
    rank_ref[...] = jnp.concatenate(
        [jnp.sum(jnp.where(hit, before, 0.0), axis=0, keepdims=True) for hit in hits], axis=0).astype(jnp.int32)
    run_s[...] = run_s[...] + jnp.sum(sel, axis=1, keepdims=True)
    cnt_ref[0] = run_s[...]

    xb = x.astype(BF16)
    sgate = jnp.dot(xb, sg_ref[...], preferred_element_type=F32)
    sup = jnp.dot(xb, su_ref[...], preferred_element_type=F32)
    hs = (sgate * jax.nn.sigmoid(sgate) * sup).astype(BF16)
    resid_ref[...] = alpha * x + jnp.dot(hs, sd_ref[...], preferred_element_type=F32)


def _route_shared(x, router_wt, router_bias, sg, su, sd, cnt_in, *, alpha):
    t, d_model = x.shape
    n_exp = router_wt.shape[0]
    rows_per_chunk = t // MOE_CHUNKS
    tr = min(rows_per_chunk, ROUTE_ROWS)
    steps_per_chunk = rows_per_chunk // tr
    const2 = lambda i: (0, 0)
    chunk_map = lambda i: (i // steps_per_chunk, 0, 0)
    vmem = 2 * 2 * tr * d_model * 4 + 2 * (router_wt.size * 4 + 3 * sg.size * 2) + VMEM_HEADROOM_BYTES
    return pl.pallas_call(
        functools.partial(_route_body, alpha=alpha, steps_per_chunk=steps_per_chunk),
        grid=(t // tr,),
        in_specs=[pl.BlockSpec((tr, d_model), lambda i: (i, 0)),
                  pl.BlockSpec(router_wt.shape, const2), pl.BlockSpec((n_exp, 1), const2),
                  pl.BlockSpec(sg.shape, const2), pl.BlockSpec(su.shape, const2), pl.BlockSpec(sd.shape, const2),
                  pl.BlockSpec((1, n_exp, LANES), chunk_map)],
        out_specs=[pl.BlockSpec((tr, d_model), lambda i: (i, 0)),
                   pl.BlockSpec((TOP_K, tr), lambda i: (0, i)),
                   pl.BlockSpec((TOP_K, tr), lambda i: (0, i)),
                   pl.BlockSpec((TOP_K, tr), lambda i: (0, i)),
                   pl.BlockSpec((1, n_exp, LANES), chunk_map)],
        out_shape=[jax.ShapeDtypeStruct((t, d_model), F32),
                   jax.ShapeDtypeStruct((TOP_K, t), jnp.int32),
                   jax.ShapeDtypeStruct((TOP_K, t), F32),
                   jax.ShapeDtypeStruct((TOP_K, t), jnp.int32),
                   jax.ShapeDtypeStruct((MOE_CHUNKS, n_exp, LANES), F32)],
        scratch_shapes=[pltpu.VMEM((tr, tr), BF16), pltpu.VMEM((n_exp, LANES), F32)],
        compiler_params=pltpu.CompilerParams(dimension_semantics=("arbitrary",),
                                             vmem_limit_bytes=min(vmem, VMEM_LIMIT_CAP_BYTES)),
        name="route_shared",
    )(x, router_wt, router_bias.reshape(n_exp, 1), sg, su, sd, cnt_in)


def _moe_body(seg_cnt_ref, seg_base_ref,
              wg_ref, wu_ref, wd_ref, g2_ref, b2_ref,
              tok_hbm, wts_hbm, xp_hbm, xs_hbm, rp_hbm, rs_hbm,
              op_hbm, os_hbm,
              x_s, y_s, xt_s, ot_s, tok_sm, wts_sm, sem_in, sem_out, sem_sm,
              *, cp_rows, cs_rows, npan):
    chunk = cp_rows + cs_rows
    chunk_stride = chunk + SUBLANES
    big = MOE_BIG_TILE
    tile_stride = big + SUBLANES
    n_exp = seg_cnt_ref.shape[0] // MOE_CHUNKS
    seg = pl.program_id(0)
    c = seg // n_exp
    chunk_first = seg % n_exp == 0
    chunk_last = seg % n_exp == n_exp - 1
    n_rows = seg_cnt_ref[seg]
    seg_row0 = seg_base_ref[seg]
    units = (n_rows + MOE_UNIT - 1) // MOE_UNIT
    n_big = units // (big // MOE_UNIT)
    units_left = units % (big // MOE_UNIT)

    def chunk_copies(hbm_p, hbm_s, vmem, sem, to_vmem):
        copies = []
        for p in range(npan):
            for hbm, nrows, row0 in ((hbm_p, cp_rows, 0), (hbm_s, cs_rows, cp_rows)):
                h = hbm.at[pl.ds(c * nrows, nrows), pl.ds(p * LANES, LANES)]
                v = vmem.at[pl.ds(p * chunk_stride + row0, nrows), :]
                copies.append(pltpu.make_async_copy(h, v, sem.at[0]) if to_vmem
                              else pltpu.make_async_copy(v, h, sem.at[0]))
        return copies

    @pl.when(chunk_first)
    def _load_chunk():
        copies = (chunk_copies(xp_hbm, xs_hbm, x_s, sem_in, True)
                  + chunk_copies(rp_hbm, rs_hbm, y_s, sem_in, True))
        for cp in copies:
            cp.start()
        zeros = jnp.zeros((SUBLANES, LANES), F32)
        for p in range(npan):
            x_s[pl.ds(p * chunk_stride + chunk, SUBLANES), :] = zeros
            y_s[pl.ds(p * chunk_stride + chunk, SUBLANES), :] = zeros
        list_rows = tok_sm.shape[0]
        lists = (pltpu.make_async_copy(tok_hbm.at[pl.ds(c * list_rows, list_rows)], tok_sm, sem_sm.at[0]),
                 pltpu.make_async_copy(wts_hbm.at[pl.ds(c * list_rows, list_rows)], wts_sm, sem_sm.at[1]))
        for cp in lists:
            cp.start()
        for cp in lists + tuple(copies):
            cp.wait()

    def process_rows(first, n):
        for i in range(n):
            t = tok_sm[first + i]
            xt_s[pl.ds(i, npan, stride=tile_stride), :] = x_s[pl.ds(t, npan, stride=chunk_stride), :]

        lhs = jnp.concatenate([xt_s[pl.ds(p * tile_stride, n), :] for p in range(npan)], axis=1).astype(BF16)
        g = jnp.dot(lhs, wg_ref[0, 0], preferred_element_type=F32)
        u = jnp.dot(lhs, wu_ref[0, 0], preferred_element_type=F32)
        hidden = (g * jax.nn.sigmoid(g) * u).astype(BF16)
        o = jnp.dot(hidden, wd_ref[0, 0], preferred_element_type=F32)
        for p in range(npan):
            ot_s[pl.ds(p * tile_stride, n), :] = o[:, p * LANES:(p + 1) * LANES]

        for i0 in range(0, n, SCATTER_UNROLL):
            toks, rows_new = [], []
            for i in range(i0, i0 + SCATTER_UNROLL):
                t = tok_sm[first + i]
                w = wts_sm[first + i]
                row = ot_s[pl.ds(i, npan, stride=tile_stride), :]
                toks.append(t)
                rows_new.append(y_s[pl.ds(t, npan, stride=chunk_stride), :] + w * row)
            for t, new in zip(toks, rows_new):
                y_s[pl.ds(t, npan, stride=chunk_stride), :] = new

    def big_tile(i, carry):
        process_rows(seg_row0 + i * big, big)
        return carry
    lax.fori_loop(0, n_big, big_tile, 0)

    first = seg_row0 + n_big * big
    size = big // 2
    while size >= MOE_UNIT:
        needed = (units_left & (size // MOE_UNIT)) != 0
        pl.when(needed)(functools.partial(process_rows, first, size))
        first = first + jnp.where(needed, size, 0)
        size //= 2

    @pl.when(chunk_last)
    def _finish_chunk():
        blk = MOE_TILE
        g2 = g2_ref[...]
        b2 = b2_ref[...]

        def ln_block(rb, carry):
            r0 = pl.multiple_of(rb * blk, blk)
            v = jnp.concatenate([y_s[pl.ds(p * chunk_stride + r0, blk), :] for p in range(npan)], axis=1)
            out = _layer_norm(v, g2, b2)
            for p in range(npan):
                y_s[pl.ds(p * chunk_stride + r0, blk), :] = out[:, p * LANES:(p + 1) * LANES]
            return carry
        lax.fori_loop(0, chunk // blk, ln_block, 0)
        copies = chunk_copies(op_hbm, os_hbm, y_s, sem_out, False)
        for cp in copies:
            cp.start()
        for cp in copies:
            cp.wait()


def _list_rows(chunk, top_k, n_exp):
    return (chunk * top_k) // MOE_TILE * MOE_TILE + n_exp * MOE_TILE


def _build_row_lists(dest, wsel, *, chunk, top_k, n_exp):
    n_pairs = top_k * chunk
    list_rows = _list_rows(chunk, top_k, n_exp)
    n_jobs = 2 * MOE_CHUNKS
    assert n_jobs <= SC_CORES * SC_SUBCORES and n_pairs % SC_LANES == 0 and list_rows % SC_LANES == 0
    tokens = jnp.tile(jnp.arange(chunk, dtype=jnp.int32), MOE_CHUNKS * top_k)
    vals = jnp.concatenate([tokens, lax.bitcast_convert_type(wsel, jnp.int32)])
    pad = jnp.concatenate([jnp.full((SC_LANES,), chunk, jnp.int32), jnp.zeros((SC_LANES,), jnp.int32)])
    mesh = plsc.VectorSubcoreMesh(core_axis_name="c", subcore_axis_name="s",
                                  num_cores=SC_CORES, num_subcores=SC_SUBCORES)

    def body(dest_hbm, vals_hbm, pad_hbm, out_hbm, dest_v, vals_v, list_v, pad_v):
        job = lax.axis_index("s") * SC_CORES + lax.axis_index("c")

        @pl.when(job < n_jobs)
        def _scatter_one_list():
            pltpu.sync_copy(dest_hbm.at[pl.ds((job % MOE_CHUNKS) * n_pairs, n_pairs)], dest_v)
            pltpu.sync_copy(vals_hbm.at[pl.ds(job * n_pairs, n_pairs)], vals_v)
            pltpu.sync_copy(pad_hbm.at[pl.ds((job // MOE_CHUNKS) * SC_LANES, SC_LANES)], pad_v)
            pad_vec = pad_v[...]

            @pl.loop(0, list_rows // SC_LANES)
            def _fill(i):
                list_v[pl.ds(i * SC_LANES, SC_LANES)] = pad_vec

            @pl.loop(0, n_pairs // SC_LANES)
            def _scatter(i):
                lanes = pl.ds(i * SC_LANES, SC_LANES)
                plsc.store_scatter(list_v, [dest_v[lanes]], vals_v[lanes])

            pltpu.sync_copy(list_v, out_hbm.at[pl.ds(job * list_rows, list_rows)])

    lists = pl.kernel(
        body, mesh=mesh,
        out_type=jax.ShapeDtypeStruct((n_jobs * list_rows,), jnp.int32),
        scratch_types=[pltpu.VMEM((n_pairs,), jnp.int32), pltpu.VMEM((n_pairs,), jnp.int32),
                       pltpu.VMEM((list_rows,), jnp.int32), pltpu.VMEM((SC_LANES,), jnp.int32)],
        compiler_params=pltpu.CompilerParams(needs_layout_passes=False),
        name="build_row_lists",
    )(dest, vals, pad)
    half = MOE_CHUNKS * list_rows
    return lists[:half], lax.bitcast_convert_type(lists[half:], F32)


def _moe_routed(seg_cnt, seg_base, tok_list, wts_list, wg, wu, wd, g2, b2, xp, xs, rp, rs, *, layer):
    _, n_exp, d_model, d_ff = wg.shape
    npan = d_model // LANES
    assert npan == SUBLANES, "one token row must fill exactly one (8, 128) register"
    assert MOE_TILE % MOE_UNIT == 0 and MOE_BIG_TILE % MOE_TILE == 0
    cp_rows, cs_rows = xp.shape[0] // MOE_CHUNKS, xs.shape[0] // MOE_CHUNKS
    chunk = cp_rows + cs_rows
    assert chunk % MOE_TILE == 0 and cs_rows % SUBLANES == 0
    chunk_stride = chunk + SUBLANES
    tile_stride = MOE_BIG_TILE + SUBLANES
    list_rows = tok_list.shape[0] // MOE_CHUNKS
    any_spec = pl.BlockSpec(memory_space=pl.ANY)
    expert_map = lambda s, *_: (layer, s % n_exp, 0, 0)
    const2 = lambda s, *_: (0, 0)
    expert_spec = lambda shape: pl.BlockSpec((1, 1) + shape, expert_map)
    vmem = (2 * npan * chunk_stride * LANES * 4 + 2 * npan * tile_stride * LANES * 4
            + WEIGHT_BUFFERS * 3 * d_model * d_ff * wg.dtype.itemsize + VMEM_HEADROOM_BYTES)
    return pl.pallas_call(
        functools.partial(_moe_body, cp_rows=cp_rows, cs_rows=cs_rows, npan=npan),
        grid_spec=pltpu.PrefetchScalarGridSpec(
            num_scalar_prefetch=2,
            grid=(MOE_CHUNKS * n_exp,),
            in_specs=[
                expert_spec((d_model, d_ff)),
                expert_spec((d_model, d_ff)),
                expert_spec((d_ff, d_model)),
                pl.BlockSpec((1, d_model), const2),
                pl.BlockSpec((1, d_model), const2),
                any_spec, any_spec, any_spec, any_spec, any_spec, any_spec,
            ],
            out_specs=[any_spec, any_spec],
            scratch_shapes=[
                pltpu.VMEM((npan * chunk_stride, LANES), F32),
                pltpu.VMEM((npan * chunk_stride, LANES), F32),
                pltpu.VMEM((npan * tile_stride, LANES), F32),
                pltpu.VMEM((npan * tile_stride, LANES), F32),
                pltpu.SMEM((list_rows,), jnp.int32),
                pltpu.SMEM((list_rows,), F32),
                pltpu.SemaphoreType.DMA((1,)),
                pltpu.SemaphoreType.DMA((1,)),
                pltpu.SemaphoreType.DMA((2,)),
            ]),
        out_shape=[jax.ShapeDtypeStruct(xp.shape, F32), jax.ShapeDtypeStruct(xs.shape, F32)],
        compiler_params=pltpu.CompilerParams(dimension_semantics=("arbitrary",),
                                             vmem_limit_bytes=min(vmem, VMEM_LIMIT_CAP_BYTES)),
        name="moe_routed",
    )(seg_cnt, seg_base, wg, wu, wd, g2.reshape(1, d_model), b2.reshape(1, d_model),
      tok_list, wts_list, xp, xs, rp, rs)


def _plan_tiles(cnt, idx_p, w_p, rank_p, idx_s, w_s, rank_s):
    k, tp = idx_p.shape
    ts = idx_s.shape[1]
    n_chunks, n_exp = cnt.shape
    tm = MOE_TILE
    cp, cs = tp // n_chunks, ts // n_chunks
    experts = jnp.arange(n_exp, dtype=jnp.int32)

    seg_rows = (cnt + tm - 1) // tm * tm
    seg_base = jnp.sum(jnp.where(experts[None, :] < experts[:, None], seg_rows[:, None, :], 0), axis=-1)

    def rows_of(idx, rank, per_chunk):
        idx3 = idx.reshape(k, n_chunks, per_chunk)
        base = jnp.sum(jnp.where(idx3[..., None] == experts, seg_base[None, :, None, :], 0), axis=-1)
        return jnp.transpose(base + rank.reshape(k, n_chunks, per_chunk), (1, 0, 2))
    by_chunk = lambda a, per_chunk: jnp.transpose(a.reshape(k, n_chunks, per_chunk), (1, 0, 2))
    dest = jnp.concatenate([rows_of(idx_p, rank_p, cp), rows_of(idx_s, rank_s, cs)], axis=2).reshape(-1)
    wsel = jnp.concatenate([by_chunk(w_p, cp), by_chunk(w_s, cs)], axis=2).reshape(-1)
    return cnt.reshape(-1), seg_base.reshape(-1), dest, wsel


def kernel(x_prompt, x_sample, state_lru_conv, state_lru_h, state_ccm_conv, lru_w_in, lru_b_in, lru_conv_w, lru_conv_b, lru_w_a, lru_b_a, lru_w_x, lru_b_x, lru_lambda, lru_w_out, lru_b_out, ccm_w_in, ccm_b_in, ccm_dw_w, ccm_dw_b, ccm_ln_g, ccm_ln_b, ccm_w_out, ccm_b_out, ln1_g, ln1_b, ln2_g, ln2_b, router_w, router_bias, exp_w_gate, exp_w_up, exp_w_down, sh_w_gate, sh_w_up, sh_w_down):
    depth = ln1_g.shape[0]
    alpha = (2 * depth) ** 0.25
    n_exp = router_w.shape[2]
    bp, sp, d_model = x_prompt.shape
    bs, ss, _ = x_sample.shape
    kw_lru = lru_conv_w.shape[1]
    kw_ccm = ccm_dw_w.shape[1]
    d_rnn = lru_conv_w.shape[2]
    d_conv = ccm_dw_w.shape[2]
    bf = lambda a: a.astype(BF16)

    zero_lru_conv = jnp.zeros((bp, kw_lru - 1, d_rnn), F32)
    zero_lru_h = jnp.zeros((bp, d_rnn), F32)
    zero_ccm_conv = jnp.zeros((bp, kw_ccm - 1, d_conv), F32)

    exp_gate_bf, exp_up_bf, exp_down_bf = bf(exp_w_gate), bf(exp_w_up), bf(exp_w_down)

    xp, xs = x_prompt, x_sample
    lru_conv_p, lru_h_p, ccm_conv_p = [], [], []
    lru_conv_s, lru_h_s, ccm_conv_s = [], [], []
    for layer in range(depth):
        j = layer // 2
        if layer % 2 == 0:
            weights = (bf(lru_w_in[j]), lru_b_in[j], lru_conv_w[j], lru_conv_b[j], bf(lru_w_a[j]), lru_b_a[j],
                       bf(lru_w_x[j]), lru_b_x[j], lru_lambda[j], bf(lru_w_out[j]), lru_b_out[j],
                       ln1_g[layer], ln1_b[layer])
            xp, cb, hl = _lru_mixer(xp, zero_lru_conv, zero_lru_h, *weights, seq_start=True, alpha=alpha)
            lru_conv_p.append(cb)
            lru_h_p.append(hl)
            xs, cb, hl = _lru_mixer(xs, state_lru_conv[j], state_lru_h[j], *weights, seq_start=False, alpha=alpha)
            lru_conv_s.append(cb)
            lru_h_s.append(hl)
        else:
            weights = (bf(ccm_w_in[j]), ccm_b_in[j], ccm_dw_w[j], ccm_dw_b[j], ccm_ln_g[j], ccm_ln_b[j],
                       bf(ccm_w_out[j]), ccm_b_out[j], ln1_g[layer], ln1_b[layer])
            xp, cb = _ccm_mixer(xp, zero_ccm_conv, *weights, alpha=alpha)
            ccm_conv_p.append(cb)
            xs, cb = _ccm_mixer(xs, state_ccm_conv[j], *weights, alpha=alpha)
            ccm_conv_s.append(cb)

        shared = (router_w[layer].T, router_bias[layer], bf(sh_w_gate[layer]), bf(sh_w_up[layer]),
                  bf(sh_w_down[layer]))
        xp2 = xp.reshape(bp * sp, d_model)
        xs2 = xs.reshape(bs * ss, d_model)
        no_tokens = jnp.zeros((MOE_CHUNKS, n_exp, LANES), F32)
        resid_p, idx_p, w_p, rank_p, cnt_p = _route_shared(xp2, *shared, no_tokens, alpha=alpha)
        resid_s, idx_s, w_s, rank_s, cnt = _route_shared(xs2, *shared, cnt_p, alpha=alpha)
        seg_cnt, seg_base, dest, wsel = _plan_tiles(cnt[:, :, 0].astype(jnp.int32), idx_p, w_p, rank_p,
                                                    idx_s, w_s, rank_s)
        tok_list, wts_list = _build_row_lists(dest, wsel, chunk=(xp2.shape[0] + xs2.shape[0]) // MOE_CHUNKS,
                                              top_k=idx_p.shape[0], n_exp=n_exp)
        xp2, xs2 = _moe_routed(seg_cnt, seg_base, tok_list, wts_list, exp_gate_bf, exp_up_bf, exp_down_bf,
                               ln2_g[layer], ln2_b[layer], xp2, xs2, resid_p, resid_s, layer=layer)
        xp = xp2.reshape(bp, sp, d_model)
        xs = xs2.reshape(bs, ss, d_model)

    return (xp, xs, jnp.stack(lru_conv_p), jnp.stack(lru_h_p), jnp.stack(ccm_conv_p),
            jnp.stack(lru_conv_s), jnp.stack(lru_h_s), jnp.stack(ccm_conv_s))
```

```python
import functools

import jax
import jax.numpy as jnp
from jax import lax
from jax.experimental import pallas as pl
from jax.experimental.pallas import tpu as pltpu
from jax.experimental.pallas import tpu_sc as plsc

LANES = 128
SUBLANES = 8
SC_CORES = 2
SC_SUBCORES = 16
SC_LANES = 16
VMEM_HEADROOM_BYTES = 8 << 20
VMEM_LIMIT_CAP_BYTES = 56 << 20
CONV_STATE_BLOCK_BYTES = 4 << 20

LN_EPS = 1e-5
LRU_C = 8.0
N_GROUPS = 8
TOPK_GROUPS = 4
TOP_K = 8
ROUTED_SCALE = 2.5

MIXER_ROWS = 512
ROUTE_ROWS = 512
MOE_TILE = 256
MOE_BIG_TILE = 512
MOE_UNIT = 64
MOE_CHUNKS = 4
SCATTER_UNROLL = 8
WEIGHT_BUFFERS = 2
ACC_VREGS = 16
SCAN_VREGS = 16

F32 = jnp.float32
BF16 = jnp.bfloat16


def _layer_norm(v, g, b):
    mu = jnp.mean(v, axis=-1, keepdims=True)
    d = v - mu
    var = jnp.mean(d * d, axis=-1, keepdims=True)
    return d * lax.rsqrt(var + LN_EPS) * g + b


def _round_up(n, m):
    return (n + m - 1) // m * m


def _lru_body(x_ref, cs_ref, h0_ref, win_ref, bin_ref, cw_ref, cb_ref, wa_ref, ba_ref, wx_ref, bx_ref,
              lam_ref, wout_ref, bout_ref, g1_ref, b1_ref,
              x1_ref, ncs_ref, hl_ref,
              ubuf, a_s, b_s, h_s, *, seq_start, alpha, scan_panels):
    nb, tc, d_model = x_ref.shape
    d_rnn = h0_ref.shape[1]
    heads = wa_ref.shape[0]
    hb = d_rnn // heads
    kw = cw_ref.shape[0]
    hist0 = SUBLANES - (kw - 1)
    rows = nb * tc
    ti = pl.program_id(1)

    @pl.when(ti == 0)
    def _init_state():
        ubuf[:, hist0:SUBLANES, :] = cs_ref[...]
        h_s[...] = h0_ref[...]

    x = x_ref[...].reshape(rows, d_model)
    proj = jnp.dot(x.astype(BF16), win_ref[...], preferred_element_type=F32) + bin_ref[...]
    gate = jax.nn.gelu(proj[:, :d_rnn])
    u = proj[:, d_rnn:]
    ubuf[:, SUBLANES:SUBLANES + tc, :] = u.reshape(nb, tc, d_rnn)
    xc3 = jnp.broadcast_to(cb_ref[...].reshape(1, 1, d_rnn), (nb, tc, d_rnn))
    for j in range(kw):
        xc3 = xc3 + cw_ref[j:j + 1, :].reshape(1, 1, d_rnn) * ubuf[:, hist0 + j:hist0 + j + tc, :]
    tail = ubuf[:, tc + hist0:tc + SUBLANES, :]
    ncs_ref[...] = tail
    ubuf[:, hist0:SUBLANES, :] = tail
    xc = xc3.reshape(rows, d_rnn)

    xcb = xc.astype(BF16)
    ra = jnp.concatenate([jnp.dot(xcb[:, h * hb:(h + 1) * hb], wa_ref[h], preferred_element_type=F32)
                          for h in range(heads)], axis=1)
    ia = jnp.concatenate([jnp.dot(xcb[:, h * hb:(h + 1) * hb], wx_ref[h], preferred_element_type=F32)
                          for h in range(heads)], axis=1)
    r = jax.nn.sigmoid(ra + ba_ref[...])
    gi = jax.nn.sigmoid(ia + bx_ref[...])
    lam = lam_ref[...]
    softplus_neg_lam = jnp.maximum(-lam, 0.0) + jnp.log1p(jnp.exp(-jnp.abs(lam)))
    log_a = (-LRU_C * softplus_neg_lam) * r
    a = jnp.exp(log_a)
    mult = jnp.sqrt(-jnp.tanh(log_a) * (a * a + 1.0))
    if seq_start:
        t_in_seq = lax.broadcasted_iota(jnp.int32, (rows, 1), 0) % tc
        mult = jnp.where((t_in_seq == 0) & (ti == 0), 1.0, mult)
    bterm = xc * gi * mult
    npan = d_rnn // LANES
    seq_stride = a_s.shape[1] // nb
    for p in range(npan):
        for n in range(nb):
            dst = pl.ds(n * seq_stride, tc)
            a_s[p, dst, :] = a[n * tc:(n + 1) * tc, p * LANES:(p + 1) * LANES]
            b_s[p, dst, :] = bterm[n * tc:(n + 1) * tc, p * LANES:(p + 1) * LANES]

    for p0 in range(0, npan, scan_panels):
        group = range(p0, min(p0 + scan_panels, npan))
        hs = [h_s[:, p * LANES:(p + 1) * LANES] for p in group]
        for t in range(tc):
            rs = pl.ds(t, nb, stride=seq_stride)
            for k, p in enumerate(group):
                hs[k] = a_s[p, rs, :] * hs[k] + b_s[p, rs, :]
                b_s[p, rs, :] = hs[k]
        for k, p in enumerate(group):
            h_s[:, p * LANES:(p + 1) * LANES] = hs[k]
    hl_ref[...] = h_s[...]

    h_all = jnp.concatenate(
        [jnp.concatenate([b_s[p, pl.ds(n * seq_stride, tc), :] for n in range(nb)], axis=0) for p in range(npan)],
        axis=1)
    y = jnp.dot((h_all * gate).astype(BF16), wout_ref[...], preferred_element_type=F32) + bout_ref[...]
    x1 = _layer_norm(alpha * x + y, g1_ref[...], b1_ref[...])
    x1_ref[...] = x1.reshape(nb, tc, d_model)


def _lru_mixer(x, conv_state, h0, win, b_in, conv_w, conv_b, wa, ba, wx, bx, lam, wout, bout, g1, b1, *,
               seq_start, alpha):
    n, s, d_model = x.shape
    d_rnn = h0.shape[1]
    kw = conv_w.shape[0]
    tc = min(s, MIXER_ROWS // SUBLANES)
    nb = min(n, MIXER_ROWS // tc)
    rows = nb * tc
    scan_panels = max(1, SCAN_VREGS * SUBLANES // nb)
    seq_stride = tc + SUBLANES if tc % (2 * SUBLANES) == 0 else tc
    row = lambda v: v.reshape(1, -1)
    const2 = lambda i, t: (0, 0)
    const3 = lambda i, t: (0, 0, 0)
    vmem = (2 * 2 * rows * d_model * 4 + 2 * (win.size + wout.size + wa.size + wx.size) * 2
            + (nb * (tc + SUBLANES) + 2 * rows + nb) * d_rnn * 4 + 4 * rows * 2 * d_rnn * 4 + VMEM_HEADROOM_BYTES)
    return pl.pallas_call(
        functools.partial(_lru_body, seq_start=seq_start, alpha=alpha, scan_panels=scan_panels),
        grid=(n // nb, s // tc),
        in_specs=[
            pl.BlockSpec((nb, tc, d_model), lambda i, t: (i, t, 0)),
            pl.BlockSpec((nb, kw - 1, d_rnn), lambda i, t: (i, 0, 0)),
            pl.BlockSpec((nb, d_rnn), lambda i, t: (i, 0)),
            pl.BlockSpec(win.shape, const2), pl.BlockSpec((1, 2 * d_rnn), const2),
            pl.BlockSpec(conv_w.shape, const2), pl.BlockSpec((1, d_rnn), const2),
            pl.BlockSpec(wa.shape, const3), pl.BlockSpec((1, d_rnn), const2),
            pl.BlockSpec(wx.shape, const3), pl.BlockSpec((1, d_rnn), const2),
            pl.BlockSpec((1, d_rnn), const2),
            pl.BlockSpec(wout.shape, const2), pl.BlockSpec((1, d_model), const2),
            pl.BlockSpec((1, d_model), const2), pl.BlockSpec((1, d_model), const2),
        ],
        out_specs=[
            pl.BlockSpec((nb, tc, d_model), lambda i, t: (i, t, 0)),
            pl.BlockSpec((nb, kw - 1, d_rnn), lambda i, t: (i, 0, 0)),
            pl.BlockSpec((nb, d_rnn), lambda i, t: (i, 0)),
        ],
        out_shape=[jax.ShapeDtypeStruct(x.shape, F32),
                   jax.ShapeDtypeStruct((n, kw - 1, d_rnn), F32),
                   jax.ShapeDtypeStruct((n, d_rnn), F32)],
        scratch_shapes=[pltpu.VMEM((nb, tc + SUBLANES, d_rnn), F32),
                        pltpu.VMEM((d_rnn // LANES, nb * seq_stride, LANES), F32),
                        pltpu.VMEM((d_rnn // LANES, nb * seq_stride, LANES), F32),
                        pltpu.VMEM((nb, d_rnn), F32)],
        compiler_params=pltpu.CompilerParams(dimension_semantics=("arbitrary", "arbitrary"),
                                             vmem_limit_bytes=min(vmem, VMEM_LIMIT_CAP_BYTES)),
        name="lru_mixer",
    )(x, conv_state, h0, win, row(b_in), conv_w, row(conv_b), wa, row(ba), wx, row(bx), row(lam),
      wout, row(bout), row(g1), row(b1))


def _ccm_body(x_ref, cs_ref, win_ref, bin_ref, dw_ref, dwb_ref, lng_ref, lnb_ref, wout_ref, bout_ref,
              g1_ref, b1_ref, x1_ref, ncs_ref, gbuf, cbuf, *, alpha, conv_cols):
    nb, tc, d_model = x_ref.shape
    kw, d_conv = dw_ref.shape
    hist_rows = _round_up(kw - 1, SUBLANES)
    hist0 = hist_rows - (kw - 1)
    rows = nb * tc
    ti = pl.program_id(1)

    @pl.when(ti == 0)
    def _init_state():
        gbuf[:, hist0:hist_rows, :] = cs_ref[...]

    x = x_ref[...].reshape(rows, d_model)
    p = jnp.dot(x.astype(BF16), win_ref[...], preferred_element_type=F32) + bin_ref[...]
    glu = p[:, :d_conv] * jax.nn.sigmoid(p[:, d_conv:])
    gbuf[:, hist_rows:hist_rows + tc, :] = glu.reshape(nb, tc, d_conv)

    def conv_seq(n, carry):
        for c0 in range(0, d_conv, conv_cols):
            cols = slice(c0, c0 + conv_cols)
            acc = jnp.broadcast_to(dwb_ref[:, cols], (tc, conv_cols))
            for shift in range(SUBLANES):
                taps = [j for j in range(kw) if (hist0 + j) % SUBLANES == shift]
                if not taps:
                    continue
                rows = tc if shift == 0 else tc + SUBLANES
                part = jnp.zeros((rows, conv_cols), F32)
                for j in taps:
                    start = hist0 + j - shift
                    part = part + dw_ref[j:j + 1, cols] * gbuf[n, start:start + rows, cols]
                acc = acc + part[shift:shift + tc]
            cbuf[n, :, cols] = acc
        return carry
    lax.fori_loop(0, nb, conv_seq, 0)

    tail = gbuf[:, tc + hist0:tc + hist_rows, :]
    ncs_ref[...] = tail
    gbuf[:, hist0:hist_rows, :] = tail

    c = cbuf[...].reshape(rows, d_conv)
    hn = _layer_norm(c, lng_ref[...], lnb_ref[...])
    hdn = hn * jax.nn.sigmoid(hn)
    y = jnp.dot(hdn.astype(BF16), wout_ref[...], preferred_element_type=F32) + bout_ref[...]
    x1 = _layer_norm(alpha * x + y, g1_ref[...], b1_ref[...])
    x1_ref[...] = x1.reshape(nb, tc, d_model)


def _ccm_mixer(x, conv_state, win, b_in, dw_w, dw_b, ln_g, ln_b, wout, bout, g1, b1, *, alpha):
    n, s, d_model = x.shape
    kw, d_conv = dw_w.shape
    tc = min(s, MIXER_ROWS // SUBLANES)
    nb = min(n, MIXER_ROWS // tc)
    while nb > SUBLANES and nb * (kw - 1) * d_conv * 4 > CONV_STATE_BLOCK_BYTES:
        nb //= 2
    rows = nb * tc
    hist_rows = _round_up(kw - 1, SUBLANES)
    conv_cols = min(d_conv, max(LANES, (ACC_VREGS * SUBLANES * LANES // tc) // LANES * LANES))
    while d_conv % conv_cols:
        conv_cols -= LANES
    row = lambda v: v.reshape(1, -1)
    const2 = lambda i, t: (0, 0)
    vmem = (2 * 2 * rows * d_model * 4 + 2 * (win.size + wout.size) * 2
            + (nb * (tc + hist_rows) + rows) * d_conv * 4 + 4 * nb * (kw - 1) * d_conv * 4
            + 4 * rows * 2 * d_conv * 4 + VMEM_HEADROOM_BYTES)
    return pl.pallas_call(
        functools.partial(_ccm_body, alpha=alpha, conv_cols=conv_cols),
        grid=(n // nb, s // tc),
        in_specs=[
            pl.BlockSpec((nb, tc, d_model), lambda i, t: (i, t, 0)),
            pl.BlockSpec((nb, kw - 1, d_conv), lambda i, t: (i, 0, 0)),
            pl.BlockSpec(win.shape, const2), pl.BlockSpec((1, 2 * d_conv), const2),
            pl.BlockSpec(dw_w.shape, const2), pl.BlockSpec((1, d_conv), const2),
            pl.BlockSpec((1, d_conv), const2), pl.BlockSpec((1, d_conv), const2),
            pl.BlockSpec(wout.shape, const2), pl.BlockSpec((1, d_model), const2),
            pl.BlockSpec((1, d_model), const2), pl.BlockSpec((1, d_model), const2),
        ],
        out_specs=[
            pl.BlockSpec((nb, tc, d_model), lambda i, t: (i, t, 0)),
            pl.BlockSpec((nb, kw - 1, d_conv), lambda i, t: (i, 0, 0)),
        ],
        out_shape=[jax.ShapeDtypeStruct(x.shape, F32),
                   jax.ShapeDtypeStruct((n, kw - 1, d_conv), F32)],
        scratch_shapes=[pltpu.VMEM((nb, tc + hist_rows, d_conv), F32),
                        pltpu.VMEM((nb, tc, d_conv), F32)],
        compiler_params=pltpu.CompilerParams(dimension_semantics=("arbitrary", "arbitrary"),
                                             vmem_limit_bytes=min(vmem, VMEM_LIMIT_CAP_BYTES)),
        name="ccm_mixer",
    )(x, conv_state, win, row(b_in), dw_w, row(dw_b), row(ln_g), row(ln_b), wout, row(bout), row(g1), row(b1))


def _first_argmax(v, axis):
    m = jnp.max(v, axis=axis, keepdims=True)
    iota = lax.broadcasted_iota(jnp.int32, v.shape, axis)
    first = jnp.min(jnp.where(v == m, iota, v.shape[axis]), axis=axis, keepdims=True)
    return m, iota == first, first


def _route_body(x_ref, rwt_ref, rb_ref, sg_ref, su_ref, sd_ref, cnt_in_ref,
                resid_ref, idx_ref, w_ref, rank_ref, cnt_ref, tri_s, run_s, *, alpha, steps_per_chunk):
    x = x_ref[...]
    n_exp = rwt_ref.shape[0]
    tr = x.shape[0]
    step = pl.program_id(0)

    @pl.when(step == 0)
    def _build_prefix_matrix():
        r = lax.broadcasted_iota(jnp.int32, (tr, tr), 0)
        c = lax.broadcasted_iota(jnp.int32, (tr, tr), 1)
        tri_s[...] = jnp.where(r <= c, 1.0, 0.0).astype(BF16)

    @pl.when(step % steps_per_chunk == 0)
    def _start_chunk():
        run_s[...] = cnt_in_ref[0]

    logits = lax.dot_general(rwt_ref[...], x, (((1,), (1,)), ((), ())),
                             precision=lax.Precision.HIGHEST, preferred_element_type=F32)
    scores = jax.nn.sigmoid(logits)
    biased = scores + rb_ref[...]
    per_group = n_exp // N_GROUPS
    grp = biased.reshape(N_GROUPS, per_group, tr)
    m1, hit1, _ = _first_argmax(grp, 1)
    m2 = jnp.max(jnp.where(hit1, -jnp.inf, grp), axis=1, keepdims=True)
    gscore = (m1 + m2).reshape(N_GROUPS, tr)
    gsel = jnp.zeros((N_GROUPS, tr), F32)
    for _ in range(TOPK_GROUPS):
        _, hit, _ = _first_argmax(gscore, 0)
        gsel = jnp.where(hit, 1.0, gsel)
        gscore = jnp.where(hit, -jnp.inf, gscore)
    emask = jnp.broadcast_to(gsel.reshape(N_GROUPS, 1, tr), (N_GROUPS, per_group, tr)).reshape(n_exp, tr)
    masked = jnp.where(emask > 0.0, biased, -jnp.inf)
    idx_rows, w_rows, hits = [], [], []
    for _ in range(TOP_K):
        _, hit, first = _first_argmax(masked, 0)
        idx_rows.append(first)
        hits.append(hit)
        w_rows.append(jnp.sum(jnp.where(hit, scores, 0.0), axis=0, keepdims=True))
        masked = jnp.where(hit, -jnp.inf, masked)
    w = jnp.concatenate(w_rows, axis=0)
    idx_ref[...] = jnp.concatenate(idx_rows, axis=0)
    w_ref[...] = w / jnp.sum(w, axis=0, keepdims=True) * ROUTED_SCALE

    sel = jnp.zeros((n_exp, tr), F32)
    for hit in hits:
        sel = jnp.where(hit, 1.0, sel)
    before = jnp.dot(sel.astype(BF16), tri_s[...], preferred_element_type=F32) - sel + run_s[:, 0:1]
    rank_ref[...] = jnp.concatenate(
        [jnp.sum(jnp.where(hit, before, 0.0), axis=0, keepdims=True) for hit in hits], axis=0).astype(jnp.int32)
    run_s[...] = run_s[...] + jnp.sum(sel, axis=1, keepdims=True)
    cnt_ref[0] = run_s[...]

    xb = x.astype(BF16)
    sgate = jnp.dot(xb, sg_ref[...], preferred_element_type=F32)
    sup = jnp.dot(xb, su_ref[...], preferred_element_type=F32)
    hs = (sgate * jax.nn.sigmoid(sgate) * sup).astype(BF16)
    resid_ref[...] = alpha * x + jnp.dot(hs, sd_ref[...], preferred_element_type=F32)


def _route_shared(x, router_wt, router_bias, sg, su, sd, cnt_in, *, alpha):
    t, d_model = x.shape
    n_exp = router_wt.shape[0]
    rows_per_chunk = t // MOE_CHUNKS
    tr = min(rows_per_chunk, ROUTE_ROWS)
    steps_per_chunk = rows_per_chunk // tr
    const2 = lambda i: (0, 0)
    chunk_map = lambda i: (i // steps_per_chunk, 0, 0)
    vmem = 2 * 2 * tr * d_model * 4 + 2 * (router_wt.size * 4 + 3 * sg.size * 2) + VMEM_HEADROOM_BYTES
    return pl.pallas_call(
        functools.partial(_route_body, alpha=alpha, steps_per_chunk=steps_per_chunk),
        grid=(t // tr,),
        in_specs=[pl.BlockSpec((tr, d_model), lambda i: (i, 0)),
                  pl.BlockSpec(router_wt.shape, const2), pl.BlockSpec((n_exp, 1), const2),
                  pl.BlockSpec(sg.shape, const2), pl.BlockSpec(su.shape, const2), pl.BlockSpec(sd.shape, const2),
                  pl.BlockSpec((1, n_exp, LANES), chunk_map)],
        out_specs=[pl.BlockSpec((tr, d_model), lambda i: (i, 0)),
                   pl.BlockSpec((TOP_K, tr), lambda i: (0, i)),
                   pl.BlockSpec((TOP_K, tr), lambda i: (0, i)),
                   pl.BlockSpec((TOP_K, tr), lambda i: (0, i)),
                   pl.BlockSpec((1, n_exp, LANES), chunk_map)],
        out_shape=[jax.ShapeDtypeStruct((t, d_model), F32),
                   jax.ShapeDtypeStruct((TOP_K, t), jnp.int32),
                   jax.ShapeDtypeStruct((TOP_K, t), F32),
                   jax.ShapeDtypeStruct((TOP_K, t), jnp.int32),
                   jax.ShapeDtypeStruct((MOE_CHUNKS, n_exp, LANES), F32)],
        scratch_shapes=[pltpu.VMEM((tr, tr), BF16), pltpu.VMEM((n_exp, LANES), F32)],
        compiler_params=pltpu.CompilerParams(dimension_semantics=("arbitrary",),
                                             vmem_limit_bytes=min(vmem, VMEM_LIMIT_CAP_BYTES)),
        name="route_shared",
    )(x, router_wt, router_bias.reshape(n_exp, 1), sg, su, sd, cnt_in)


def _moe_body(seg_cnt_ref, seg_base_ref,
              wg_ref, wu_ref, wd_ref, g2_ref, b2_ref,
              tok_hbm, wts_hbm, xp_hbm, xs_hbm, rp_hbm, rs_hbm,
              op_hbm, os_hbm,
              x_s, y_s, xt_s, ot_s, tok_sm, wts_sm, sem_in, sem_out, sem_sm,
              *, cp_rows, cs_rows, npan):
    chunk = cp_rows + cs_rows
    chunk_stride = chunk + SUBLANES
    big = MOE_BIG_TILE
    tile_stride = big + MOE_UNIT + SUBLANES
    n_exp = seg_cnt_ref.shape[0] // MOE_CHUNKS
    seg = pl.program_id(0)
    c = seg // n_exp
    chunk_first = seg % n_exp == 0
    chunk_last = seg % n_exp == n_exp - 1
    n_rows = seg_cnt_ref[seg]
    seg_row0 = seg_base_ref[seg]
    units = (n_rows + MOE_UNIT - 1) // MOE_UNIT
    n_big = units // (big // MOE_UNIT)
    units_left = units % (big // MOE_UNIT)

    def chunk_copies(hbm_p, hbm_s, vmem, sem, to_vmem):
        copies = []
        for p in range(npan):
            for hbm, nrows, row0 in ((hbm_p, cp_rows, 0), (hbm_s, cs_rows, cp_rows)):
                h = hbm.at[pl.ds(c * nrows, nrows), pl.ds(p * LANES, LANES)]
                v = vmem.at[pl.ds(p * chunk_stride + row0, nrows), :]
                copies.append(pltpu.make_async_copy(h, v, sem.at[0]) if to_vmem
                              else pltpu.make_async_copy(v, h, sem.at[0]))
        return copies

    @pl.when(chunk_first)
    def _load_chunk():
        copies = (chunk_copies(xp_hbm, xs_hbm, x_s, sem_in, True)
                  + chunk_copies(rp_hbm, rs_hbm, y_s, sem_in, True))
        for cp in copies:
            cp.start()
        zeros = jnp.zeros((SUBLANES, LANES), F32)
        for p in range(npan):
            x_s[pl.ds(p * chunk_stride + chunk, SUBLANES), :] = zeros
            y_s[pl.ds(p * chunk_stride + chunk, SUBLANES), :] = zeros
        list_rows = tok_sm.shape[0]
        lists = (pltpu.make_async_copy(tok_hbm.at[pl.ds(c * list_rows, list_rows)], tok_sm, sem_sm.at[0]),
                 pltpu.make_async_copy(wts_hbm.at[pl.ds(c * list_rows, list_rows)], wts_sm, sem_sm.at[1]))
        for cp in lists:
            cp.start()
        for cp in lists + tuple(copies):
            cp.wait()

    def process_rows(first, n):
        for i in range(n):
            t = tok_sm[first + i]
            xt_s[pl.ds(i, npan, stride=tile_stride), :] = x_s[pl.ds(t, npan, stride=chunk_stride), :]

        lhs = jnp.concatenate([xt_s[pl.ds(p * tile_stride, n), :] for p in range(npan)], axis=1).astype(BF16)
        g = jnp.dot(lhs, wg_ref[0, 0], preferred_element_type=F32)
        u = jnp.dot(lhs, wu_ref[0, 0], preferred_element_type=F32)
        hidden = (g * jax.nn.sigmoid(g) * u).astype(BF16)
        o = jnp.dot(hidden, wd_ref[0, 0], preferred_element_type=F32)
        for p in range(npan):
            ot_s[pl.ds(p * tile_stride, n), :] = o[:, p * LANES:(p + 1) * LANES]

        for i0 in range(0, n, SCATTER_UNROLL):
            toks, rows_new = [], []
            for i in range(i0, i0 + SCATTER_UNROLL):
                t = tok_sm[first + i]
                w = wts_sm[first + i]
                row = ot_s[pl.ds(i, npan, stride=tile_stride), :]
                toks.append(t)
                rows_new.append(y_s[pl.ds(t, npan, stride=chunk_stride), :] + w * row)
            for t, new in zip(toks, rows_new):
                y_s[pl.ds(t, npan, stride=chunk_stride), :] = new

    extend_last = (n_big >= 1) & (units_left == 1)
    n_plain = n_big - extend_last.astype(jnp.int32)

    def big_tile(i, carry):
        process_rows(seg_row0 + i * big, big)
        return carry
    lax.fori_loop(0, n_plain, big_tile, 0)
    first = seg_row0 + n_plain * big
    pl.when(extend_last)(functools.partial(process_rows, first, big + MOE_UNIT))

    size = big // 2
    while size >= MOE_UNIT:
        needed = ((units_left & (size // MOE_UNIT)) != 0) & jnp.logical_not(extend_last)
        pl.when(needed)(functools.partial(process_rows, first, size))
        first = first + jnp.where(needed, size, 0)
        size //= 2

    @pl.when(chunk_last)
    def _finish_chunk():
        blk = MOE_TILE
        g2 = g2_ref[...]
        b2 = b2_ref[...]

        def ln_block(rb, carry):
            r0 = pl.multiple_of(rb * blk, blk)
            v = jnp.concatenate([y_s[pl.ds(p * chunk_stride + r0, blk), :] for p in range(npan)], axis=1)
            out = _layer_norm(v, g2, b2)
            for p in range(npan):
                y_s[pl.ds(p * chunk_stride + r0, blk), :] = out[:, p * LANES:(p + 1) * LANES]
            return carry
        lax.fori_loop(0, chunk // blk, ln_block, 0)
        copies = chunk_copies(op_hbm, os_hbm, y_s, sem_out, False)
        for cp in copies:
            cp.start()
        for cp in copies:
            cp.wait()


def _list_rows(chunk, top_k, n_exp):
    return (chunk * top_k) // MOE_TILE * MOE_TILE + n_exp * MOE_TILE


def _build_row_lists(dest, wsel, *, chunk, top_k, n_exp):
    n_pairs = top_k * chunk
    list_rows = _list_rows(chunk, top_k, n_exp)
    n_jobs = 2 * MOE_CHUNKS
    assert n_jobs <= SC_CORES * SC_SUBCORES and n_pairs % SC_LANES == 0 and list_rows % SC_LANES == 0
    tokens = jnp.tile(jnp.arange(chunk, dtype=jnp.int32), MOE_CHUNKS * top_k)
    vals = jnp.concatenate([tokens, lax.bitcast_convert_type(wsel, jnp.int32)])
    pad = jnp.concatenate([jnp.full((SC_LANES,), chunk, jnp.int32), jnp.zeros((SC_LANES,), jnp.int32)])
    mesh = plsc.VectorSubcoreMesh(core_axis_name="c", subcore_axis_name="s",
                                  num_cores=SC_CORES, num_subcores=SC_SUBCORES)

    def body(dest_hbm, vals_hbm, pad_hbm, out_hbm, dest_v, vals_v, list_v, pad_v):
        job = lax.axis_index("s") * SC_CORES + lax.axis_index("c")

        @pl.when(job < n_jobs)
        def _scatter_one_list():
            pltpu.sync_copy(dest_hbm.at[pl.ds((job % MOE_CHUNKS) * n_pairs, n_pairs)], dest_v)
            pltpu.sync_copy(vals_hbm.at[pl.ds(job * n_pairs, n_pairs)], vals_v)
            pltpu.sync_copy(pad_hbm.at[pl.ds((job // MOE_CHUNKS) * SC_LANES, SC_LANES)], pad_v)
            pad_vec = pad_v[...]

            @pl.loop(0, list_rows // SC_LANES)
            def _fill(i):
                list_v[pl.ds(i * SC_LANES, SC_LANES)] = pad_vec

            @pl.loop(0, n_pairs // SC_LANES)
            def _scatter(i):
                lanes = pl.ds(i * SC_LANES, SC_LANES)
                plsc.store_scatter(list_v, [dest_v[lanes]], vals_v[lanes])

            pltpu.sync_copy(list_v, out_hbm.at[pl.ds(job * list_rows, list_rows)])

    lists = pl.kernel(
        body, mesh=mesh,
        out_type=jax.ShapeDtypeStruct((n_jobs * list_rows,), jnp.int32),
        scratch_types=[pltpu.VMEM((n_pairs,), jnp.int32), pltpu.VMEM((n_pairs,), jnp.int32),
                       pltpu.VMEM((list_rows,), jnp.int32), pltpu.VMEM((SC_LANES,), jnp.int32)],
        compiler_params=pltpu.CompilerParams(needs_layout_passes=False),
        name="build_row_lists",
    )(dest, vals, pad)
    half = MOE_CHUNKS * list_rows
    return lists[:half], lax.bitcast_convert_type(lists[half:], F32)


def _moe_routed(seg_cnt, seg_base, tok_list, wts_list, wg, wu, wd, g2, b2, xp, xs, rp, rs, *, layer):
    _, n_exp, d_model, d_ff = wg.shape
    npan = d_model // LANES
    assert npan == SUBLANES, "one token row must fill exactly one (8, 128) register"
    assert MOE_TILE % MOE_UNIT == 0 and MOE_BIG_TILE % MOE_TILE == 0
    cp_rows, cs_rows = xp.shape[0] // MOE_CHUNKS, xs.shape[0] // MOE_CHUNKS
    chunk = cp_rows + cs_rows
    assert chunk % MOE_TILE == 0 and cs_rows % SUBLANES == 0
    chunk_stride = chunk + SUBLANES
    tile_stride = MOE_BIG_TILE + MOE_UNIT + SUBLANES
    list_rows = tok_list.shape[0] // MOE_CHUNKS
    any_spec = pl.BlockSpec(memory_space=pl.ANY)
    expert_map = lambda s, *_: (layer, s % n_exp, 0, 0)
    const2 = lambda s, *_: (0, 0)
    expert_spec = lambda shape: pl.BlockSpec((1, 1) + shape, expert_map)
    vmem = (2 * npan * chunk_stride * LANES * 4 + 2 * npan * tile_stride * LANES * 4
            + WEIGHT_BUFFERS * 3 * d_model * d_ff * wg.dtype.itemsize + VMEM_HEADROOM_BYTES)
    return pl.pallas_call(
        functools.partial(_moe_body, cp_rows=cp_rows, cs_rows=cs_rows, npan=npan),
        grid_spec=pltpu.PrefetchScalarGridSpec(
            num_scalar_prefetch=2,
            grid=(MOE_CHUNKS * n_exp,),
            in_specs=[
                expert_spec((d_model, d_ff)),
                expert_spec((d_model, d_ff)),
                expert_spec((d_ff, d_model)),
                pl.BlockSpec((1, d_model), const2),
                pl.BlockSpec((1, d_model), const2),
                any_spec, any_spec, any_spec, any_spec, any_spec, any_spec,
            ],
            out_specs=[any_spec, any_spec],
            scratch_shapes=[
                pltpu.VMEM((npan * chunk_stride, LANES), F32),
                pltpu.VMEM((npan * chunk_stride, LANES), F32),
                pltpu.VMEM((npan * tile_stride, LANES), F32),
                pltpu.VMEM((npan * tile_stride, LANES), F32),
                pltpu.SMEM((list_rows,), jnp.int32),
                pltpu.SMEM((list_rows,), F32),
                pltpu.SemaphoreType.DMA((1,)),
                pltpu.SemaphoreType.DMA((1,)),
                pltpu.SemaphoreType.DMA((2,)),
            ]),
        out_shape=[jax.ShapeDtypeStruct(xp.shape, F32), jax.ShapeDtypeStruct(xs.shape, F32)],
        compiler_params=pltpu.CompilerParams(dimension_semantics=("arbitrary",),
                                             vmem_limit_bytes=min(vmem, VMEM_LIMIT_CAP_BYTES)),
        name="moe_routed",
    )(seg_cnt, seg_base, wg, wu, wd, g2.reshape(1, d_model), b2.reshape(1, d_model),
      tok_list, wts_list, xp, xs, rp, rs)


def _plan_tiles(cnt, idx_p, w_p, rank_p, idx_s, w_s, rank_s):
    k, tp = idx_p.shape
    ts = idx_s.shape[1]
    n_chunks, n_exp = cnt.shape
    tm = MOE_TILE
    cp, cs = tp // n_chunks, ts // n_chunks
    experts = jnp.arange(n_exp, dtype=jnp.int32)

    seg_rows = (cnt + tm - 1) // tm * tm
    seg_base = jnp.sum(jnp.where(experts[None, :] < experts[:, None], seg_rows[:, None, :], 0), axis=-1)

    def rows_of(idx, rank, per_chunk):
        idx3 = idx.reshape(k, n_chunks, per_chunk)
        base = jnp.sum(jnp.where(idx3[..., None] == experts, seg_base[None, :, None, :], 0), axis=-1)
        return jnp.transpose(base + rank.reshape(k, n_chunks, per_chunk), (1, 0, 2))
    by_chunk = lambda a, per_chunk: jnp.transpose(a.reshape(k, n_chunks, per_chunk), (1, 0, 2))
    dest = jnp.concatenate([rows_of(idx_p, rank_p, cp), rows_of(idx_s, rank_s, cs)], axis=2).reshape(-1)
    wsel = jnp.concatenate([by_chunk(w_p, cp), by_chunk(w_s, cs)], axis=2).reshape(-1)
    return cnt.reshape(-1), seg_base.reshape(-1), dest, wsel


def kernel(x_prompt, x_sample, state_lru_conv, state_lru_h, state_ccm_conv, lru_w_in, lru_b_in, lru_conv_w, lru_conv_b, lru_w_a, lru_b_a, lru_w_x, lru_b_x, lru_lambda, lru_w_out, lru_b_out, ccm_w_in, ccm_b_in, ccm_dw_w, ccm_dw_b, ccm_ln_g, ccm_ln_b, ccm_w_out, ccm_b_out, ln1_g, ln1_b, ln2_g, ln2_b, router_w, router_bias, exp_w_gate, exp_w_up, exp_w_down, sh_w_gate, sh_w_up, sh_w_down):
    depth = ln1_g.shape[0]
    alpha = (2 * depth) ** 0.25
    n_exp = router_w.shape[2]
    bp, sp, d_model = x_prompt.shape
    bs, ss, _ = x_sample.shape
    kw_lru = lru_conv_w.shape[1]
    kw_ccm = ccm_dw_w.shape[1]
    d_rnn = lru_conv_w.shape[2]
    d_conv = ccm_dw_w.shape[2]
    bf = lambda a: a.astype(BF16)

    zero_lru_conv = jnp.zeros((bp, kw_lru - 1, d_rnn), F32)
    zero_lru_h = jnp.zeros((bp, d_rnn), F32)
    zero_ccm_conv = jnp.zeros((bp, kw_ccm - 1, d_conv), F32)

    exp_gate_bf, exp_up_bf, exp_down_bf = bf(exp_w_gate), bf(exp_w_up), bf(exp_w_down)

    xp, xs = x_prompt, x_sample
    lru_conv_p, lru_h_p, ccm_conv_p = [], [], []
    lru_conv_s, lru_h_s, ccm_conv_s = [], [], []
    for layer in range(depth):
        j = layer // 2
        if layer % 2 == 0:
            weights = (bf(lru_w_in[j]), lru_b_in[j], lru_conv_w[j], lru_conv_b[j], bf(lru_w_a[j]), lru_b_a[j],
                       bf(lru_w_x[j]), lru_b_x[j], lru_lambda[j], bf(lru_w_out[j]), lru_b_out[j],
                       ln1_g[layer], ln1_b[layer])
            xp, cb, hl = _lru_mixer(xp, zero_lru_conv, zero_lru_h, *weights, seq_start=True, alpha=alpha)
            lru_conv_p.append(cb)
            lru_h_p.append(hl)
            xs, cb, hl = _lru_mixer(xs, state_lru_conv[j], state_lru_h[j], *weights, seq_start=False, alpha=alpha)
            lru_conv_s.append(cb)
            lru_h_s.append(hl)
        else:
            weights = (bf(ccm_w_in[j]), ccm_b_in[j], ccm_dw_w[j], ccm_dw_b[j], ccm_ln_g[j], ccm_ln_b[j],
                       bf(ccm_w_out[j]), ccm_b_out[j], ln1_g[layer], ln1_b[layer])
            xp, cb = _ccm_mixer(xp, zero_ccm_conv, *weights, alpha=alpha)
            ccm_conv_p.append(cb)
            xs, cb = _ccm_mixer(xs, state_ccm_conv[j], *weights, alpha=alpha)
            ccm_conv_s.append(cb)

        shared = (router_w[layer].T, router_bias[layer], bf(sh_w_gate[layer]), bf(sh_w_up[layer]),
                  bf(sh_w_down[layer]))
        xp2 = xp.reshape(bp * sp, d_model)
        xs2 = xs.reshape(bs * ss, d_model)
        no_tokens = jnp.zeros((MOE_CHUNKS, n_exp, LANES), F32)
        resid_p, idx_p, w_p, rank_p, cnt_p = _route_shared(xp2, *shared, no_tokens, alpha=alpha)
        resid_s, idx_s, w_s, rank_s, cnt = _route_shared(xs2, *shared, cnt_p, alpha=alpha)
        seg_cnt, seg_base, dest, wsel = _plan_tiles(cnt[:, :, 0].astype(jnp.int32), idx_p, w_p, rank_p,
                                                    idx_s, w_s, rank_s)
        tok_list, wts_list = _build_row_lists(dest, wsel, chunk=(xp2.shape[0] + xs2.shape[0]) // MOE_CHUNKS,
                                              top_k=idx_p.shape[0], n_exp=n_exp)
        xp2, xs2 = _moe_routed(seg_cnt, seg_base, tok_list, wts_list, exp_gate_bf, exp_up_bf, exp_down_bf,
                               ln2_g[layer], ln2_b[layer], xp2, xs2, resid_p, resid_s, layer=layer)
        xp = xp2.reshape(bp, sp, d_model)
        xs = xs2.reshape(bs, ss, d_model)

    return (xp, xs, jnp.stack(lru_conv_p), jnp.stack(lru_h_p), jnp.stack(ccm_conv_p),
            jnp.stack(lru_conv_s), jnp.stack(lru_h_s), jnp.stack(ccm_conv_s))
```

```python
import functools

import jax
import jax.numpy as jnp
from jax import lax
from jax.experimental import pallas as pl
from jax.experimental.pallas import tpu as pltpu
from jax.experimental.pallas import tpu_sc as plsc

LANES = 128
SUBLANES = 8
SC_CORES = 2
SC_SUBCORES = 16
SC_LANES = 16
VMEM_HEADROOM_BYTES = 8 << 20
VMEM_LIMIT_CAP_BYTES = 56 << 20
CONV_STATE_BLOCK_BYTES = 4 << 20

LN_EPS = 1e-5
LRU_C = 8.0
N_GROUPS = 8
TOPK_GROUPS = 4
TOP_K = 8
ROUTED_SCALE = 2.5

MIXER_ROWS = 512
ROUTE_ROWS = 512
MOE_TILE = 256
MOE_BIG_TILE = 512
MOE_UNIT = 32
MOE_EXTEND_UNITS = 2
MOE_CHUNKS = 4
SCATTER_UNROLL = 8
WEIGHT_BUFFERS = 2
ACC_VREGS = 16
SCAN_VREGS = 16

F32 = jnp.float32
BF16 = jnp.bfloat16


def _layer_norm(v, g, b):
    mu = jnp.mean(v, axis=-1, keepdims=True)
    d = v - mu
    var = jnp.mean(d * d, axis=-1, keepdims=True)
    return d * lax.rsqrt(var + LN_EPS) * g + b


def _round_up(n, m):
    return (n + m - 1) // m * m


def _lru_body(x_ref, cs_ref, h0_ref, win_ref, bin_ref, cw_ref, cb_ref, wa_ref, ba_ref, wx_ref, bx_ref,
              lam_ref, wout_ref, bout_ref, g1_ref, b1_ref,
              x1_ref, ncs_ref, hl_ref,
              ubuf, a_s, b_s, h_s, *, seq_start, alpha, scan_panels):
    nb, tc, d_model = x_ref.shape
    d_rnn = h0_ref.shape[1]
    heads = wa_ref.shape[0]
    hb = d_rnn // heads
    kw = cw_ref.shape[0]
    hist0 = SUBLANES - (kw - 1)
    rows = nb * tc
    ti = pl.program_id(1)

    @pl.when(ti == 0)
    def _init_state():
        ubuf[:, hist0:SUBLANES, :] = cs_ref[...]
        h_s[...] = h0_ref[...]

    x = x_ref[...].reshape(rows, d_model)
    proj = jnp.dot(x.astype(BF16), win_ref[...], preferred_element_type=F32) + bin_ref[...]
    gate = jax.nn.gelu(proj[:, :d_rnn])
    u = proj[:, d_rnn:]
    ubuf[:, SUBLANES:SUBLANES + tc, :] = u.reshape(nb, tc, d_rnn)
    xc3 = jnp.broadcast_to(cb_ref[...].reshape(1, 1, d_rnn), (nb, tc, d_rnn))
    for j in range(kw):
        xc3 = xc3 + cw_ref[j:j + 1, :].reshape(1, 1, d_rnn) * ubuf[:, hist0 + j:hist0 + j + tc, :]
    tail = ubuf[:, tc + hist0:tc + SUBLANES, :]
    ncs_ref[...] = tail
    ubuf[:, hist0:SUBLANES, :] = tail
    xc = xc3.reshape(rows, d_rnn)

    xcb = xc.astype(BF16)
    ra = jnp.concatenate([jnp.dot(xcb[:, h * hb:(h + 1) * hb], wa_ref[h], preferred_element_type=F32)
                          for h in range(heads)], axis=1)
    ia = jnp.concatenate([jnp.dot(xcb[:, h * hb:(h + 1) * hb], wx_ref[h], preferred_element_type=F32)
                          for h in range(heads)], axis=1)
    r = jax.nn.sigmoid(ra + ba_ref[...])
    gi = jax.nn.sigmoid(ia + bx_ref[...])
    lam = lam_ref[...]
    softplus_neg_lam = jnp.maximum(-lam, 0.0) + jnp.log1p(jnp.exp(-jnp.abs(lam)))
    log_a = (-LRU_C * softplus_neg_lam) * r
    a = jnp.exp(log_a)
    mult = jnp.sqrt(-jnp.tanh(log_a) * (a * a + 1.0))
    if seq_start:
        t_in_seq = lax.broadcasted_iota(jnp.int32, (rows, 1), 0) % tc
        mult = jnp.where((t_in_seq == 0) & (ti == 0), 1.0, mult)
    bterm = xc * gi * mult
    npan = d_rnn // LANES
    seq_stride = a_s.shape[1] // nb
    for p in range(npan):
        for n in range(nb):
            dst = pl.ds(n * seq_stride, tc)
            a_s[p, dst, :] = a[n * tc:(n + 1) * tc, p * LANES:(p + 1) * LANES]
            b_s[p, dst, :] = bterm[n * tc:(n + 1) * tc, p * LANES:(p + 1) * LANES]

    for p0 in range(0, npan, scan_panels):
        group = range(p0, min(p0 + scan_panels, npan))
        hs = [h_s[:, p * LANES:(p + 1) * LANES] for p in group]
        for t in range(tc):
            rs = pl.ds(t, nb, stride=seq_stride)
            for k, p in enumerate(group):
                hs[k] = a_s[p, rs, :] * hs[k] + b_s[p, rs, :]
                b_s[p, rs, :] = hs[k]
        for k, p in enumerate(group):
            h_s[:, p * LANES:(p + 1) * LANES] = hs[k]
    hl_ref[...] = h_s[...]

    h_all = jnp.concatenate(
        [jnp.concatenate([b_s[p, pl.ds(n * seq_stride, tc), :] for n in range(nb)], axis=0) for p in range(npan)],
        axis=1)
    y = jnp.dot((h_all * gate).astype(BF16), wout_ref[...], preferred_element_type=F32) + bout_ref[...]
    x1 = _layer_norm(alpha * x + y, g1_ref[...], b1_ref[...])
    x1_ref[...] = x1.reshape(nb, tc, d_model)


def _lru_mixer(x, conv_state, h0, win, b_in, conv_w, conv_b, wa, ba, wx, bx, lam, wout, bout, g1, b1, *,
               seq_start, alpha):
    n, s, d_model = x.shape
    d_rnn = h0.shape[1]
    kw = conv_w.shape[0]
    tc = min(s, MIXER_ROWS // SUBLANES)
    nb = min(n, MIXER_ROWS // tc)
    rows = nb * tc
    scan_panels = max(1, SCAN_VREGS * SUBLANES // nb)
    seq_stride = tc + SUBLANES if tc % (2 * SUBLANES) == 0 else tc
    row = lambda v: v.reshape(1, -1)
    const2 = lambda i, t: (0, 0)
    const3 = lambda i, t: (0, 0, 0)
    vmem = (2 * 2 * rows * d_model * 4 + 2 * (win.size + wout.size + wa.size + wx.size) * 2
            + (nb * (tc + SUBLANES) + 2 * rows + nb) * d_rnn * 4 + 4 * rows * 2 * d_rnn * 4 + VMEM_HEADROOM_BYTES)
    return pl.pallas_call(
        functools.partial(_lru_body, seq_start=seq_start, alpha=alpha, scan_panels=scan_panels),
        grid=(n // nb, s // tc),
        in_specs=[
            pl.BlockSpec((nb, tc, d_model), lambda i, t: (i, t, 0)),
            pl.BlockSpec((nb, kw - 1, d_rnn), lambda i, t: (i, 0, 0)),
            pl.BlockSpec((nb, d_rnn), lambda i, t: (i, 0)),
            pl.BlockSpec(win.shape, const2), pl.BlockSpec((1, 2 * d_rnn), const2),
            pl.BlockSpec(conv_w.shape, const2), pl.BlockSpec((1, d_rnn), const2),
            pl.BlockSpec(wa.shape, const3), pl.BlockSpec((1, d_rnn), const2),
            pl.BlockSpec(wx.shape, const3), pl.BlockSpec((1, d_rnn), const2),
            pl.BlockSpec((1, d_rnn), const2),
            pl.BlockSpec(wout.shape, const2), pl.BlockSpec((1, d_model), const2),
            pl.BlockSpec((1, d_model), const2), pl.BlockSpec((1, d_model), const2),
        ],
        out_specs=[
            pl.BlockSpec((nb, tc, d_model), lambda i, t: (i, t, 0)),
            pl.BlockSpec((nb, kw - 1, d_rnn), lambda i, t: (i, 0, 0)),
            pl.BlockSpec((nb, d_rnn), lambda i, t: (i, 0)),
        ],
        out_shape=[jax.ShapeDtypeStruct(x.shape, F32),
                   jax.ShapeDtypeStruct((n, kw - 1, d_rnn), F32),
                   jax.ShapeDtypeStruct((n, d_rnn), F32)],
        scratch_shapes=[pltpu.VMEM((nb, tc + SUBLANES, d_rnn), F32),
                        pltpu.VMEM((d_rnn // LANES, nb * seq_stride, LANES), F32),
                        pltpu.VMEM((d_rnn // LANES, nb * seq_stride, LANES), F32),
                        pltpu.VMEM((nb, d_rnn), F32)],
        compiler_params=pltpu.CompilerParams(dimension_semantics=("arbitrary", "arbitrary"),
                                             vmem_limit_bytes=min(vmem, VMEM_LIMIT_CAP_BYTES)),
        name="lru_mixer",
    )(x, conv_state, h0, win, row(b_in), conv_w, row(conv_b), wa, row(ba), wx, row(bx), row(lam),
      wout, row(bout), row(g1), row(b1))


def _ccm_body(x_ref, cs_ref, win_ref, bin_ref, dw_ref, dwb_ref, lng_ref, lnb_ref, wout_ref, bout_ref,
              g1_ref, b1_ref, x1_ref, ncs_ref, gbuf, cbuf, *, alpha, conv_cols):
    nb, tc, d_model = x_ref.shape
    kw, d_conv = dw_ref.shape
    hist_rows = _round_up(kw - 1, SUBLANES)
    hist0 = hist_rows - (kw - 1)
    rows = nb * tc
    ti = pl.program_id(1)

    @pl.when(ti == 0)
    def _init_state():
        gbuf[:, hist0:hist_rows, :] = cs_ref[...]

    x = x_ref[...].reshape(rows, d_model)
    p = jnp.dot(x.astype(BF16), win_ref[...], preferred_element_type=F32) + bin_ref[...]
    glu = p[:, :d_conv] * jax.nn.sigmoid(p[:, d_conv:])
    gbuf[:, hist_rows:hist_rows + tc, :] = glu.reshape(nb, tc, d_conv)

    def conv_seq(n, carry):
        for c0 in range(0, d_conv, conv_cols):
            cols = slice(c0, c0 + conv_cols)
            acc = jnp.broadcast_to(dwb_ref[:, cols], (tc, conv_cols))
            for shift in range(SUBLANES):
                taps = [j for j in range(kw) if (hist0 + j) % SUBLANES == shift]
                if not taps:
                    continue
                rows = tc if shift == 0 else tc + SUBLANES
                part = jnp.zeros((rows, conv_cols), F32)
                for j in taps:
                    start = hist0 + j - shift
                    part = part + dw_ref[j:j + 1, cols] * gbuf[n, start:start + rows, cols]
                acc = acc + part[shift:shift + tc]
            cbuf[n, :, cols] = acc
        return carry
    lax.fori_loop(0, nb, conv_seq, 0)

    tail = gbuf[:, tc + hist0:tc + hist_rows, :]
    ncs_ref[...] = tail
    gbuf[:, hist0:hist_rows, :] = tail

    c = cbuf[...].reshape(rows, d_conv)
    hn = _layer_norm(c, lng_ref[...], lnb_ref[...])
    hdn = hn * jax.nn.sigmoid(hn)
    y = jnp.dot(hdn.astype(BF16), wout_ref[...], preferred_element_type=F32) + bout_ref[...]
    x1 = _layer_norm(alpha * x + y, g1_ref[...], b1_ref[...])
    x1_ref[...] = x1.reshape(nb, tc, d_model)


def _ccm_mixer(x, conv_state, win, b_in, dw_w, dw_b, ln_g, ln_b, wout, bout, g1, b1, *, alpha):
    n, s, d_model = x.shape
    kw, d_conv = dw_w.shape
    tc = min(s, MIXER_ROWS // SUBLANES)
    nb = min(n, MIXER_ROWS // tc)
    while nb > SUBLANES and nb * (kw - 1) * d_conv * 4 > CONV_STATE_BLOCK_BYTES:
        nb //= 2
    rows = nb * tc
    hist_rows = _round_up(kw - 1, SUBLANES)
    conv_cols = min(d_conv, max(LANES, (ACC_VREGS * SUBLANES * LANES // tc) // LANES * LANES))
    while d_conv % conv_cols:
        conv_cols -= LANES
    row = lambda v: v.reshape(1, -1)
    const2 = lambda i, t: (0, 0)
    vmem = (2 * 2 * rows * d_model * 4 + 2 * (win.size + wout.size) * 2
            + (nb * (tc + hist_rows) + rows) * d_conv * 4 + 4 * nb * (kw - 1) * d_conv * 4
            + 4 * rows * 2 * d_conv * 4 + VMEM_HEADROOM_BYTES)
    return pl.pallas_call(
        functools.partial(_ccm_body, alpha=alpha, conv_cols=conv_cols),
        grid=(n // nb, s // tc),
        in_specs=[
            pl.BlockSpec((nb, tc, d_model), lambda i, t: (i, t, 0)),
            pl.BlockSpec((nb, kw - 1, d_conv), lambda i, t: (i, 0, 0)),
            pl.BlockSpec(win.shape, const2), pl.BlockSpec((1, 2 * d_conv), const2),
            pl.BlockSpec(dw_w.shape, const2), pl.BlockSpec((1, d_conv), const2),
            pl.BlockSpec((1, d_conv), const2), pl.BlockSpec((1, d_conv), const2),
            pl.BlockSpec(wout.shape, const2), pl.BlockSpec((1, d_model), const2),
            pl.BlockSpec((1, d_model), const2), pl.BlockSpec((1, d_model), const2),
        ],
        out_specs=[
            pl.BlockSpec((nb, tc, d_model), lambda i, t: (i, t, 0)),
            pl.BlockSpec((nb, kw - 1, d_conv), lambda i, t: (i, 0, 0)),
        ],
        out_shape=[jax.ShapeDtypeStruct(x.shape, F32),
                   jax.ShapeDtypeStruct((n, kw - 1, d_conv), F32)],
        scratch_shapes=[pltpu.VMEM((nb, tc + hist_rows, d_conv), F32),
                        pltpu.VMEM((nb, tc, d_conv), F32)],
        compiler_params=pltpu.CompilerParams(dimension_semantics=("arbitrary", "arbitrary"),
                                             vmem_limit_bytes=min(vmem, VMEM_LIMIT_CAP_BYTES)),
        name="ccm_mixer",
    )(x, conv_state, win, row(b_in), dw_w, row(dw_b), row(ln_g), row(ln_b), wout, row(bout), row(g1), row(b1))


def _first_argmax(v, axis):
    m = jnp.max(v, axis=axis, keepdims=True)
    iota = lax.broadcasted_iota(jnp.int32, v.shape, axis)
    first = jnp.min(jnp.where(v == m, iota, v.shape[axis]), axis=axis, keepdims=True)
    return m, iota == first, first


def _route_body(x_ref, rwt_ref, rb_ref, sg_ref, su_ref, sd_ref, cnt_in_ref,
                resid_ref, idx_ref, w_ref, rank_ref, cnt_ref, tri_s, run_s, *, alpha, steps_per_chunk):
    x = x_ref[...]
    n_exp = rwt_ref.shape[0]
    tr = x.shape[0]
    step = pl.program_id(0)

    @pl.when(step == 0)
    def _build_prefix_matrix():
        r = lax.broadcasted_iota(jnp.int32, (tr, tr), 0)
        c = lax.broadcasted_iota(jnp.int32, (tr, tr), 1)
        tri_s[...] = jnp.where(r <= c, 1.0, 0.0).astype(BF16)

    @pl.when(step % steps_per_chunk == 0)
    def _start_chunk():
        run_s[...] = cnt_in_ref[0]

    logits = lax.dot_general(rwt_ref[...], x, (((1,), (1,)), ((), ())),
                             precision=lax.Precision.HIGHEST, preferred_element_type=F32)
    scores = jax.nn.sigmoid(logits)
    biased = scores + rb_ref[...]
    per_group = n_exp // N_GROUPS
    grp = biased.reshape(N_GROUPS, per_group, tr)
    m1, hit1, _ = _first_argmax(grp, 1)
    m2 = jnp.max(jnp.where(hit1, -jnp.inf, grp), axis=1, keepdims=True)
    gscore = (m1 + m2).reshape(N_GROUPS, tr)
    gsel = jnp.zeros((N_GROUPS, tr), F32)
    for _ in range(TOPK_GROUPS):
        _, hit, _ = _first_argmax(gscore, 0)
        gsel = jnp.where(hit, 1.0, gsel)
        gscore = jnp.where(hit, -jnp.inf, gscore)
    emask = jnp.broadcast_to(gsel.reshape(N_GROUPS, 1, tr), (N_GROUPS, per_group, tr)).reshape(n_exp, tr)
    masked = jnp.where(emask > 0.0, biased, -jnp.inf)
    idx_rows, w_rows, hits = [], [], []
    for _ in range(TOP_K):
        _, hit, first = _first_argmax(masked, 0)
        idx_rows.append(first)
        hits.append(hit)
        w_rows.append(jnp.sum(jnp.where(hit, scores, 0.0), axis=0, keepdims=True))
        masked = jnp.where(hit, -jnp.inf, masked)
    w = jnp.concatenate(w_rows, axis=0)
    idx_ref[...] = jnp.concatenate(idx_rows, axis=0)
    w_ref[...] = w / jnp.sum(w, axis=0, keepdims=True) * ROUTED_SCALE

    sel = jnp.zeros((n_exp, tr), F32)
    for hit in hits:
        sel = jnp.where(hit, 1.0, sel)
    before = jnp.dot(sel.astype(BF16), tri_s[...], preferred_element_type=F32) - sel + run_s[:, 0:1]
    rank_ref[...] = jnp.concatenate(
        [jnp.sum(jnp.where(hit, before, 0.0), axis=0, keepdims=True) for hit in hits], axis=0).astype(jnp.int32)
    run_s[...] = run_s[...] + jnp.sum(sel, axis=1, keepdims=True)
    cnt_ref[0] = run_s[...]

    xb = x.astype(BF16)
    sgate = jnp.dot(xb, sg_ref[...], preferred_element_type=F32)
    sup = jnp.dot(xb, su_ref[...], preferred_element_type=F32)
    hs = (sgate * jax.nn.sigmoid(sgate) * sup).astype(BF16)
    resid_ref[...] = alpha * x + jnp.dot(hs, sd_ref[...], preferred_element_type=F32)


def _route_shared(x, router_wt, router_bias, sg, su, sd, cnt_in, *, alpha):
    t, d_model = x.shape
    n_exp = router_wt.shape[0]
    rows_per_chunk = t // MOE_CHUNKS
    tr = min(rows_per_chunk, ROUTE_ROWS)
    steps_per_chunk = rows_per_chunk // tr
    const2 = lambda i: (0, 0)
    chunk_map = lambda i: (i // steps_per_chunk, 0, 0)
    vmem = 2 * 2 * tr * d_model * 4 + 2 * (router_wt.size * 4 + 3 * sg.size * 2) + VMEM_HEADROOM_BYTES
    return pl.pallas_call(
        functools.partial(_route_body, alpha=alpha, steps_per_chunk=steps_per_chunk),
        grid=(t // tr,),
        in_specs=[pl.BlockSpec((tr, d_model), lambda i: (i, 0)),
                  pl.BlockSpec(router_wt.shape, const2), pl.BlockSpec((n_exp, 1), const2),
                  pl.BlockSpec(sg.shape, const2), pl.BlockSpec(su.shape, const2), pl.BlockSpec(sd.shape, const2),
                  pl.BlockSpec((1, n_exp, LANES), chunk_map)],
        out_specs=[pl.BlockSpec((tr, d_model), lambda i: (i, 0)),
                   pl.BlockSpec((TOP_K, tr), lambda i: (0, i)),
                   pl.BlockSpec((TOP_K, tr), lambda i: (0, i)),
                   pl.BlockSpec((TOP_K, tr), lambda i: (0, i)),
                   pl.BlockSpec((1, n_exp, LANES), chunk_map)],
        out_shape=[jax.ShapeDtypeStruct((t, d_model), F32),
                   jax.ShapeDtypeStruct((TOP_K, t), jnp.int32),
                   jax.ShapeDtypeStruct((TOP_K, t), F32),
                   jax.ShapeDtypeStruct((TOP_K, t), jnp.int32),
                   jax.ShapeDtypeStruct((MOE_CHUNKS, n_exp, LANES), F32)],
        scratch_shapes=[pltpu.VMEM((tr, tr), BF16), pltpu.VMEM((n_exp, LANES), F32)],
        compiler_params=pltpu.CompilerParams(dimension_semantics=("arbitrary",),
                                             vmem_limit_bytes=min(vmem, VMEM_LIMIT_CAP_BYTES)),
        name="route_shared",
    )(x, router_wt, router_bias.reshape(n_exp, 1), sg, su, sd, cnt_in)


def _moe_body(seg_cnt_ref, seg_base_ref,
              wg_ref, wu_ref, wd_ref, g2_ref, b2_ref,
              tok_hbm, wts_hbm, xp_hbm, xs_hbm, rp_hbm, rs_hbm,
              op_hbm, os_hbm,
              x_s, y_s, xt_s, ot_s, tok_sm, wts_sm, sem_in, sem_out, sem_sm,
              *, cp_rows, cs_rows, npan):
    chunk = cp_rows + cs_rows
    chunk_stride = chunk + SUBLANES
    big = MOE_BIG_TILE
    tile_stride = big + MOE_EXTEND_UNITS * MOE_UNIT + SUBLANES
    n_exp = seg_cnt_ref.shape[0] // MOE_CHUNKS
    seg = pl.program_id(0)
    c = seg // n_exp
    chunk_first = seg % n_exp == 0
    chunk_last = seg % n_exp == n_exp - 1
    n_rows = seg_cnt_ref[seg]
    seg_row0 = seg_base_ref[seg]
    units = (n_rows + MOE_UNIT - 1) // MOE_UNIT
    n_big = units // (big // MOE_UNIT)
    units_left = units % (big // MOE_UNIT)

    def chunk_copies(hbm_p, hbm_s, vmem, sem, to_vmem):
        copies = []
        for p in range(npan):
            for hbm, nrows, row0 in ((hbm_p, cp_rows, 0), (hbm_s, cs_rows, cp_rows)):
                h = hbm.at[pl.ds(c * nrows, nrows), pl.ds(p * LANES, LANES)]
                v = vmem.at[pl.ds(p * chunk_stride + row0, nrows), :]
                copies.append(pltpu.make_async_copy(h, v, sem.at[0]) if to_vmem
                              else pltpu.make_async_copy(v, h, sem.at[0]))
        return copies

    @pl.when(chunk_first)
    def _load_chunk():
        copies = (chunk_copies(xp_hbm, xs_hbm, x_s, sem_in, True)
                  + chunk_copies(rp_hbm, rs_hbm, y_s, sem_in, True))
        for cp in copies:
            cp.start()
        zeros = jnp.zeros((SUBLANES, LANES), F32)
        for p in range(npan):
            x_s[pl.ds(p * chunk_stride + chunk, SUBLANES), :] = zeros
            y_s[pl.ds(p * chunk_stride + chunk, SUBLANES), :] = zeros
        list_rows = tok_sm.shape[0]
        lists = (pltpu.make_async_copy(tok_hbm.at[pl.ds(c * list_rows, list_rows)], tok_sm, sem_sm.at[0]),
                 pltpu.make_async_copy(wts_hbm.at[pl.ds(c * list_rows, list_rows)], wts_sm, sem_sm.at[1]))
        for cp in lists:
            cp.start()
        for cp in lists + tuple(copies):
            cp.wait()

    def process_rows(first, n):
        for i in range(n):
            t = tok_sm[first + i]
            xt_s[pl.ds(i, npan, stride=tile_stride), :] = x_s[pl.ds(t, npan, stride=chunk_stride), :]

        lhs = jnp.concatenate([xt_s[pl.ds(p * tile_stride, n), :] for p in range(npan)], axis=1).astype(BF16)
        g = jnp.dot(lhs, wg_ref[0, 0], preferred_element_type=F32)
        u = jnp.dot(lhs, wu_ref[0, 0], preferred_element_type=F32)
        hidden = (g * jax.nn.sigmoid(g) * u).astype(BF16)
        o = jnp.dot(hidden, wd_ref[0, 0], preferred_element_type=F32)
        for p in range(npan):
            ot_s[pl.ds(p * tile_stride, n), :] = o[:, p * LANES:(p + 1) * LANES]

        for i0 in range(0, n, SCATTER_UNROLL):
            toks, rows_new = [], []
            for i in range(i0, i0 + SCATTER_UNROLL):
                t = tok_sm[first + i]
                w = wts_sm[first + i]
                row = ot_s[pl.ds(i, npan, stride=tile_stride), :]
                toks.append(t)
                rows_new.append(y_s[pl.ds(t, npan, stride=chunk_stride), :] + w * row)
            for t, new in zip(toks, rows_new):
                y_s[pl.ds(t, npan, stride=chunk_stride), :] = new

    extend_last = (n_big >= 1) & (units_left >= 1) & (units_left <= MOE_EXTEND_UNITS)
    n_plain = n_big - extend_last.astype(jnp.int32)

    def big_tile(i, carry):
        process_rows(seg_row0 + i * big, big)
        return carry
    lax.fori_loop(0, n_plain, big_tile, 0)
    first = seg_row0 + n_plain * big
    for extra in range(1, MOE_EXTEND_UNITS + 1):
        pl.when(extend_last & (units_left == extra))(
            functools.partial(process_rows, first, big + extra * MOE_UNIT))

    size = big // 2
    while size >= MOE_UNIT:
        needed = ((units_left & (size // MOE_UNIT)) != 0) & jnp.logical_not(extend_last)
        pl.when(needed)(functools.partial(process_rows, first, size))
        first = first + jnp.where(needed, size, 0)
        size //= 2

    @pl.when(chunk_last)
    def _finish_chunk():
        blk = MOE_TILE
        g2 = g2_ref[...]
        b2 = b2_ref[...]

        def ln_block(rb, carry):
            r0 = pl.multiple_of(rb * blk, blk)
            v = jnp.concatenate([y_s[pl.ds(p * chunk_stride + r0, blk), :] for p in range(npan)], axis=1)
            out = _layer_norm(v, g2, b2)
            for p in range(npan):
                y_s[pl.ds(p * chunk_stride + r0, blk), :] = out[:, p * LANES:(p + 1) * LANES]
            return carry
        lax.fori_loop(0, chunk // blk, ln_block, 0)
        copies = chunk_copies(op_hbm, os_hbm, y_s, sem_out, False)
        for cp in copies:
            cp.start()
        for cp in copies:
            cp.wait()


def _list_rows(chunk, top_k, n_exp):
    return (chunk * top_k) // MOE_TILE * MOE_TILE + n_exp * MOE_TILE


def _build_row_lists(dest, wsel, *, chunk, top_k, n_exp):
    n_pairs = top_k * chunk
    list_rows = _list_rows(chunk, top_k, n_exp)
    n_jobs = 2 * MOE_CHUNKS
    assert n_jobs <= SC_CORES * SC_SUBCORES and n_pairs % SC_LANES == 0 and list_rows % SC_LANES == 0
    tokens = jnp.tile(jnp.arange(chunk, dtype=jnp.int32), MOE_CHUNKS * top_k)
    vals = jnp.concatenate([tokens, lax.bitcast_convert_type(wsel, jnp.int32)])
    pad = jnp.concatenate([jnp.full((SC_LANES,), chunk, jnp.int32), jnp.zeros((SC_LANES,), jnp.int32)])
    mesh = plsc.VectorSubcoreMesh(core_axis_name="c", subcore_axis_name="s",
                                  num_cores=SC_CORES, num_subcores=SC_SUBCORES)

    def body(dest_hbm, vals_hbm, pad_hbm, out_hbm, dest_v, vals_v, list_v, pad_v):
        job = lax.axis_index("s") * SC_CORES + lax.axis_index("c")

        @pl.when(job < n_jobs)
        def _scatter_one_list():
            pltpu.sync_copy(dest_hbm.at[pl.ds((job % MOE_CHUNKS) * n_pairs, n_pairs)], dest_v)
            pltpu.sync_copy(vals_hbm.at[pl.ds(job * n_pairs, n_pairs)], vals_v)
            pltpu.sync_copy(pad_hbm.at[pl.ds((job // MOE_CHUNKS) * SC_LANES, SC_LANES)], pad_v)
            pad_vec = pad_v[...]

            @pl.loop(0, list_rows // SC_LANES)
            def _fill(i):
                list_v[pl.ds(i * SC_LANES, SC_LANES)] = pad_vec

            @pl.loop(0, n_pairs // SC_LANES)
            def _scatter(i):
                lanes = pl.ds(i * SC_LANES, SC_LANES)
                plsc.store_scatter(list_v, [dest_v[lanes]], vals_v[lanes])

            pltpu.sync_copy(list_v, out_hbm.at[pl.ds(job * list_rows, list_rows)])

    lists = pl.kernel(
        body, mesh=mesh,
        out_type=jax.ShapeDtypeStruct((n_jobs * list_rows,), jnp.int32),
        scratch_types=[pltpu.VMEM((n_pairs,), jnp.int32), pltpu.VMEM((n_pairs,), jnp.int32),
                       pltpu.VMEM((list_rows,), jnp.int32), pltpu.VMEM((SC_LANES,), jnp.int32)],
        compiler_params=pltpu.CompilerParams(needs_layout_passes=False),
        name="build_row_lists",
    )(dest, vals, pad)
    half = MOE_CHUNKS * list_rows
    return lists[:half], lax.bitcast_convert_type(lists[half:], F32)


def _moe_routed(seg_cnt, seg_base, tok_list, wts_list, wg, wu, wd, g2, b2, xp, xs, rp, rs, *, layer):
    _, n_exp, d_model, d_ff = wg.shape
    npan = d_model // LANES
    assert npan == SUBLANES, "one token row must fill exactly one (8, 128) register"
    assert MOE_TILE % MOE_UNIT == 0 and MOE_BIG_TILE % MOE_TILE == 0
    cp_rows, cs_rows = xp.shape[0] // MOE_CHUNKS, xs.shape[0] // MOE_CHUNKS
    chunk = cp_rows + cs_rows
    assert chunk % MOE_TILE == 0 and cs_rows % SUBLANES == 0
    chunk_stride = chunk + SUBLANES
    tile_stride = MOE_BIG_TILE + MOE_EXTEND_UNITS * MOE_UNIT + SUBLANES
    list_rows = tok_list.shape[0] // MOE_CHUNKS
    any_spec = pl.BlockSpec(memory_space=pl.ANY)
    expert_map = lambda s, *_: (layer, s % n_exp, 0, 0)
    const2 = lambda s, *_: (0, 0)
    expert_spec = lambda shape: pl.BlockSpec((1, 1) + shape, expert_map)
    vmem = (2 * npan * chunk_stride * LANES * 4 + 2 * npan * tile_stride * LANES * 4
            + WEIGHT_BUFFERS * 3 * d_model * d_ff * wg.dtype.itemsize + VMEM_HEADROOM_BYTES)
    return pl.pallas_call(
        functools.partial(_moe_body, cp_rows=cp_rows, cs_rows=cs_rows, npan=npan),
        grid_spec=pltpu.PrefetchScalarGridSpec(
            num_scalar_prefetch=2,
            grid=(MOE_CHUNKS * n_exp,),
            in_specs=[
                expert_spec((d_model, d_ff)),
                expert_spec((d_model, d_ff)),
                expert_spec((d_ff, d_model)),
                pl.BlockSpec((1, d_model), const2),
                pl.BlockSpec((1, d_model), const2),
                any_spec, any_spec, any_spec, any_spec, any_spec, any_spec,
            ],
            out_specs=[any_spec, any_spec],
            scratch_shapes=[
                pltpu.VMEM((npan * chunk_stride, LANES), F32),
                pltpu.VMEM((npan * chunk_stride, LANES), F32),
                pltpu.VMEM((npan * tile_stride, LANES), F32),
                pltpu.VMEM((npan * tile_stride, LANES), F32),
                pltpu.SMEM((list_rows,), jnp.int32),
                pltpu.SMEM((list_rows,), F32),
                pltpu.SemaphoreType.DMA((1,)),
                pltpu.SemaphoreType.DMA((1,)),
                pltpu.SemaphoreType.DMA((2,)),
            ]),
        out_shape=[jax.ShapeDtypeStruct(xp.shape, F32), jax.ShapeDtypeStruct(xs.shape, F32)],
        compiler_params=pltpu.CompilerParams(dimension_semantics=("arbitrary",),
                                             vmem_limit_bytes=min(vmem, VMEM_LIMIT_CAP_BYTES)),
        name="moe_routed",
    )(seg_cnt, seg_base, wg, wu, wd, g2.reshape(1, d_model), b2.reshape(1, d_model),
      tok_list, wts_list, xp, xs, rp, rs)


def _plan_tiles(cnt, idx_p, w_p, rank_p, idx_s, w_s, rank_s):
    k, tp = idx_p.shape
    ts = idx_s.shape[1]
    n_chunks, n_exp = cnt.shape
    tm = MOE_TILE
    cp, cs = tp // n_chunks, ts // n_chunks
    experts = jnp.arange(n_exp, dtype=jnp.int32)

    seg_rows = (cnt + tm - 1) // tm * tm
    seg_base = jnp.sum(jnp.where(experts[None, :] < experts[:, None], seg_rows[:, None, :], 0), axis=-1)

    def rows_of(idx, rank, per_chunk):
        idx3 = idx.reshape(k, n_chunks, per_chunk)
        base = jnp.sum(jnp.where(idx3[..., None] == experts, seg_base[None, :, None, :], 0), axis=-1)
        return jnp.transpose(base + rank.reshape(k, n_chunks, per_chunk), (1, 0, 2))
    by_chunk = lambda a, per_chunk: jnp.transpose(a.reshape(k, n_chunks, per_chunk), (1, 0, 2))
    dest = jnp.concatenate([rows_of(idx_p, rank_p, cp), rows_of(idx_s, rank_s, cs)], axis=2).reshape(-1)
    wsel = jnp.concatenate([by_chunk(w_p, cp), by_chunk(w_s, cs)], axis=2).reshape(-1)
    return cnt.reshape(-1), seg_base.reshape(-1), dest, wsel


def kernel(x_prompt, x_sample, state_lru_conv, state_lru_h, state_ccm_conv, lru_w_in, lru_b_in, lru_conv_w, lru_conv_b, lru_w_a, lru_b_a, lru_w_x, lru_b_x, lru_lambda, lru_w_out, lru_b_out, ccm_w_in, ccm_b_in, ccm_dw_w, ccm_dw_b, ccm_ln_g, ccm_ln_b, ccm_w_out, ccm_b_out, ln1_g, ln1_b, ln2_g, ln2_b, router_w, router_bias, exp_w_gate, exp_w_up, exp_w_down, sh_w_gate, sh_w_up, sh_w_down):
    depth = ln1_g.shape[0]
    alpha = (2 * depth) ** 0.25
    n_exp = router_w.shape[2]
    bp, sp, d_model = x_prompt.shape
    bs, ss, _ = x_sample.shape
    kw_lru = lru_conv_w.shape[1]
    kw_ccm = ccm_dw_w.shape[1]
    d_rnn = lru_conv_w.shape[2]
    d_conv = ccm_dw_w.shape[2]
    bf = lambda a: a.astype(BF16)

    zero_lru_conv = jnp.zeros((bp, kw_lru - 1, d_rnn), F32)
    zero_lru_h = jnp.zeros((bp, d_rnn), F32)
    zero_ccm_conv = jnp.zeros((bp, kw_ccm - 1, d_conv), F32)

    exp_gate_bf, exp_up_bf, exp_down_bf = bf(exp_w_gate), bf(exp_w_up), bf(exp_w_down)

    xp, xs = x_prompt, x_sample
    lru_conv_p, lru_h_p, ccm_conv_p = [], [], []
    lru_conv_s, lru_h_s, ccm_conv_s = [], [], []
    for layer in range(depth):
        j = layer // 2
        if layer % 2 == 0:
            weights = (bf(lru_w_in[j]), lru_b_in[j], lru_conv_w[j], lru_conv_b[j], bf(lru_w_a[j]), lru_b_a[j],
                       bf(lru_w_x[j]), lru_b_x[j], lru_lambda[j], bf(lru_w_out[j]), lru_b_out[j],
                       ln1_g[layer], ln1_b[layer])
            xp, cb, hl = _lru_mixer(xp, zero_lru_conv, zero_lru_h, *weights, seq_start=True, alpha=alpha)
            lru_conv_p.append(cb)
            lru_h_p.append(hl)
            xs, cb, hl = _lru_mixer(xs, state_lru_conv[j], state_lru_h[j], *weights, seq_start=False, alpha=alpha)
            lru_conv_s.append(cb)
            lru_h_s.append(hl)
        else:
            weights = (bf(ccm_w_in[j]), ccm_b_in[j], ccm_dw_w[j], ccm_dw_b[j], ccm_ln_g[j], ccm_ln_b[j],
                       bf(ccm_w_out[j]), ccm_b_out[j], ln1_g[layer], ln1_b[layer])
            xp, cb = _ccm_mixer(xp, zero_ccm_conv, *weights, alpha=alpha)
            ccm_conv_p.append(cb)
            xs, cb = _ccm_mixer(xs, state_ccm_conv[j], *weights, alpha=alpha)
            ccm_conv_s.append(cb)

        shared = (router_w[layer].T, router_bias[layer], bf(sh_w_gate[layer]), bf(sh_w_up[layer]),
                  bf(sh_w_down[layer]))
        xp2 = xp.reshape(bp * sp, d_model)
        xs2 = xs.reshape(bs * ss, d_model)
        no_tokens = jnp.zeros((MOE_CHUNKS, n_exp, LANES), F32)
        resid_p, idx_p, w_p, rank_p, cnt_p = _route_shared(xp2, *shared, no_tokens, alpha=alpha)
        resid_s, idx_s, w_s, rank_s, cnt = _route_shared(xs2, *shared, cnt_p, alpha=alpha)
        seg_cnt, seg_base, dest, wsel = _plan_tiles(cnt[:, :, 0].astype(jnp.int32), idx_p, w_p, rank_p,
                                                    idx_s, w_s, rank_s)
        tok_list, wts_list = _build_row_lists(dest, wsel, chunk=(xp2.shape[0] + xs2.shape[0]) // MOE_CHUNKS,
                                              top_k=idx_p.shape[0], n_exp=n_exp)
        xp2, xs2 = _moe_routed(seg_cnt, seg_base, tok_list, wts_list, exp_gate_bf, exp_up_bf, exp_down_bf,
                               ln2_g[layer], ln2_b[layer], xp2, xs2, resid_p, resid_s, layer=layer)
        xp = xp2.reshape(bp, sp, d_model)
        xs = xs2.reshape(bs, ss, d_model)

    return (xp, xs, jnp.stack(lru_conv_p), jnp.stack(lru_h_p), jnp.stack(ccm_conv_p),
            jnp.stack(lru_conv_s), jnp.stack(lru_h_s), jnp.stack(ccm_conv_s))
```

```python
import functools

import jax
import jax.numpy as jnp
from jax import lax
from jax.experimental import pallas as pl
from jax.experimental.pallas import tpu as pltpu
from jax.experimental.pallas import tpu_sc as plsc

LANES = 128
SUBLANES = 8
SC_CORES = 2
SC_SUBCORES = 16
SC_LANES = 16
VMEM_HEADROOM_BYTES = 8 << 20
VMEM_LIMIT_CAP_BYTES = 56 << 20
CONV_STATE_BLOCK_BYTES = 4 << 20

LN_EPS = 1e-5
LRU_C = 8.0
N_GROUPS = 8
TOPK_GROUPS = 4
TOP_K = 8
ROUTED_SCALE = 2.5

MIXER_ROWS = 512
ROUTE_ROWS = 512
MOE_TILE = 256
MOE_BIG_TILE = 512
MOE_UNIT = 64
MOE_CHUNKS = 4
SCATTER_UNROLL = 8
WEIGHT_BUFFERS = 2
ACC_VREGS = 16
SCAN_VREGS = 16

F32 = jnp.float32
BF16 = jnp.bfloat16


def _layer_norm(v, g, b):
    mu = jnp.mean(v, axis=-1, keepdims=True)
    d = v - mu
    var = jnp.mean(d * d, axis=-1, keepdims=True)
    return d * lax.rsqrt(var + LN_EPS) * g + b


def _round_up(n, m):
    return (n + m - 1) // m * m


def _lru_body(x_ref, cs_ref, h0_ref, win_ref, bin_ref, cw_ref, cb_ref, wa_ref, ba_ref, wx_ref, bx_ref,
              lam_ref, wout_ref, bout_ref, g1_ref, b1_ref,
              x1_ref, ncs_ref, hl_ref,
              ubuf, a_s, b_s, h_s, *, seq_start, alpha, scan_panels):
    nb, tc, d_model = x_ref.shape
    d_rnn = h0_ref.shape[1]
    heads = wa_ref.shape[0]
    hb = d_rnn // heads
    kw = cw_ref.shape[0]
    hist0 = SUBLANES - (kw - 1)
    rows = nb * tc
    ti = pl.program_id(1)

    @pl.when(ti == 0)
    def _init_state():
        ubuf[:, hist0:SUBLANES, :] = cs_ref[...]
        h_s[...] = h0_ref[...]

    x = x_ref[...].reshape(rows, d_model)
    proj = jnp.dot(x.astype(BF16), win_ref[...], preferred_element_type=F32) + bin_ref[...]
    gate = jax.nn.gelu(proj[:, :d_rnn])
    u = proj[:, d_rnn:]
    ubuf[:, SUBLANES:SUBLANES + tc, :] = u.reshape(nb, tc, d_rnn)
    xc3 = jnp.broadcast_to(cb_ref[...].reshape(1, 1, d_rnn), (nb, tc, d_rnn))
    for j in range(kw):
        xc3 = xc3 + cw_ref[j:j + 1, :].reshape(1, 1, d_rnn) * ubuf[:, hist0 + j:hist0 + j + tc, :]
    tail = ubuf[:, tc + hist0:tc + SUBLANES, :]
    ncs_ref[...] = tail
    ubuf[:, hist0:SUBLANES, :] = tail
    xc = xc3.reshape(rows, d_rnn)

    xcb = xc.astype(BF16)
    ra = jnp.concatenate([jnp.dot(xcb[:, h * hb:(h + 1) * hb], wa_ref[h], preferred_element_type=F32)
                          for h in range(heads)], axis=1)
    ia = jnp.concatenate([jnp.dot(xcb[:, h * hb:(h + 1) * hb], wx_ref[h], preferred_element_type=F32)
                          for h in range(heads)], axis=1)
    r = jax.nn.sigmoid(ra + ba_ref[...])
    gi = jax.nn.sigmoid(ia + bx_ref[...])
    lam = lam_ref[...]
    softplus_neg_lam = jnp.maximum(-lam, 0.0) + jnp.log1p(jnp.exp(-jnp.abs(lam)))
    log_a = (-LRU_C * softplus_neg_lam) * r
    a = jnp.exp(log_a)
    mult = jnp.sqrt(-jnp.tanh(log_a) * (a * a + 1.0))
    if seq_start:
        t_in_seq = lax.broadcasted_iota(jnp.int32, (rows, 1), 0) % tc
        mult = jnp.where((t_in_seq == 0) & (ti == 0), 1.0, mult)
    bterm = xc * gi * mult
    npan = d_rnn // LANES
    seq_stride = a_s.shape[1] // nb
    for p in range(npan):
        for n in range(nb):
            dst = pl.ds(n * seq_stride, tc)
            a_s[p, dst, :] = a[n * tc:(n + 1) * tc, p * LANES:(p + 1) * LANES]
            b_s[p, dst, :] = bterm[n * tc:(n + 1) * tc, p * LANES:(p + 1) * LANES]

    for p0 in range(0, npan, scan_panels):
        group = range(p0, min(p0 + scan_panels, npan))
        hs = [h_s[:, p * LANES:(p + 1) * LANES] for p in group]
        for t in range(tc):
            rs = pl.ds(t, nb, stride=seq_stride)
            for k, p in enumerate(group):
                hs[k] = a_s[p, rs, :] * hs[k] + b_s[p, rs, :]
                b_s[p, rs, :] = hs[k]
        for k, p in enumerate(group):
            h_s[:, p * LANES:(p + 1) * LANES] = hs[k]
    hl_ref[...] = h_s[...]

    h_all = jnp.concatenate(
        [jnp.concatenate([b_s[p, pl.ds(n * seq_stride, tc), :] for n in range(nb)], axis=0) for p in range(npan)],
        axis=1)
    y = jnp.dot((h_all * gate).astype(BF16), wout_ref[...], preferred_element_type=F32) + bout_ref[...]
    x1 = _layer_norm(alpha * x + y, g1_ref[...], b1_ref[...])
    x1_ref[...] = x1.reshape(nb, tc, d_model)


def _lru_mixer(x, conv_state, h0, win, b_in, conv_w, conv_b, wa, ba, wx, bx, lam, wout, bout, g1, b1, *,
               seq_start, alpha):
    n, s, d_model = x.shape
    d_rnn = h0.shape[1]
    kw = conv_w.shape[0]
    tc = min(s, MIXER_ROWS // SUBLANES)
    nb = min(n, MIXER_ROWS // tc)
    rows = nb * tc
    scan_panels = max(1, SCAN_VREGS * SUBLANES // nb)
    seq_stride = tc + SUBLANES if tc % (2 * SUBLANES) == 0 else tc
    row = lambda v: v.reshape(1, -1)
    const2 = lambda i, t: (0, 0)
    const3 = lambda i, t: (0, 0, 0)
    vmem = (2 * 2 * rows * d_model * 4 + 2 * (win.size + wout.size + wa.size + wx.size) * 2
            + (nb * (tc + SUBLANES) + 2 * rows + nb) * d_rnn * 4 + 4 * rows * 2 * d_rnn * 4 + VMEM_HEADROOM_BYTES)
    return pl.pallas_call(
        functools.partial(_lru_body, seq_start=seq_start, alpha=alpha, scan_panels=scan_panels),
        grid=(n // nb, s // tc),
        in_specs=[
            pl.BlockSpec((nb, tc, d_model), lambda i, t: (i, t, 0)),
            pl.BlockSpec((nb, kw - 1, d_rnn), lambda i, t: (i, 0, 0)),
            pl.BlockSpec((nb, d_rnn), lambda i, t: (i, 0)),
            pl.BlockSpec(win.shape, const2), pl.BlockSpec((1, 2 * d_rnn), const2),
            pl.BlockSpec(conv_w.shape, const2), pl.BlockSpec((1, d_rnn), const2),
            pl.BlockSpec(wa.shape, const3), pl.BlockSpec((1, d_rnn), const2),
            pl.BlockSpec(wx.shape, const3), pl.BlockSpec((1, d_rnn), const2),
            pl.BlockSpec((1, d_rnn), const2),
            pl.BlockSpec(wout.shape, const2), pl.BlockSpec((1, d_model), const2),
            pl.BlockSpec((1, d_model), const2), pl.BlockSpec((1, d_model), const2),
        ],
        out_specs=[
            pl.BlockSpec((nb, tc, d_model), lambda i, t: (i, t, 0)),
            pl.BlockSpec((nb, kw - 1, d_rnn), lambda i, t: (i, 0, 0)),
            pl.BlockSpec((nb, d_rnn), lambda i, t: (i, 0)),
        ],
        out_shape=[jax.ShapeDtypeStruct(x.shape, F32),
                   jax.ShapeDtypeStruct((n, kw - 1, d_rnn), F32),
                   jax.ShapeDtypeStruct((n, d_rnn), F32)],
        scratch_shapes=[pltpu.VMEM((nb, tc + SUBLANES, d_rnn), F32),
                        pltpu.VMEM((d_rnn // LANES, nb * seq_stride, LANES), F32),
                        pltpu.VMEM((d_rnn // LANES, nb * seq_stride, LANES), F32),
                        pltpu.VMEM((nb, d_rnn), F32)],
        compiler_params=pltpu.CompilerParams(dimension_semantics=("arbitrary", "arbitrary"),
                                             vmem_limit_bytes=min(vmem, VMEM_LIMIT_CAP_BYTES)),
        name="lru_mixer",
    )(x, conv_state, h0, win, row(b_in), conv_w, row(conv_b), wa, row(ba), wx, row(bx), row(lam),
      wout, row(bout), row(g1), row(b1))


def _ccm_body(x_ref, cs_ref, win_ref, bin_ref, dw_ref, dwb_ref, lng_ref, lnb_ref, wout_ref, bout_ref,
              g1_ref, b1_ref, x1_ref, ncs_ref, gbuf, cbuf, *, alpha, conv_cols):
    nb, tc, d_model = x_ref.shape
    kw, d_conv = dw_ref.shape
    hist_rows = _round_up(kw - 1, SUBLANES)
    hist0 = hist_rows - (kw - 1)
    rows = nb * tc
    ti = pl.program_id(1)

    @pl.when(ti == 0)
    def _init_state():
        gbuf[:, hist0:hist_rows, :] = cs_ref[...]

    x = x_ref[...].reshape(rows, d_model)
    p = jnp.dot(x.astype(BF16), win_ref[...], preferred_element_type=F32) + bin_ref[...]
    glu = p[:, :d_conv] * jax.nn.sigmoid(p[:, d_conv:])
    gbuf[:, hist_rows:hist_rows + tc, :] = glu.reshape(nb, tc, d_conv)

    def conv_seq(n, carry):
        for c0 in range(0, d_conv, conv_cols):
            cols = slice(c0, c0 + conv_cols)
            acc = jnp.broadcast_to(dwb_ref[:, cols], (tc, conv_cols))
            for shift in range(SUBLANES):
                taps = [j for j in range(kw) if (hist0 + j) % SUBLANES == shift]
                if not taps:
                    continue
                rows = tc if shift == 0 else tc + SUBLANES
                part = jnp.zeros((rows, conv_cols), F32)
                for j in taps:
                    start = hist0 + j - shift
                    part = part + dw_ref[j:j + 1, cols] * gbuf[n, start:start + rows, cols]
                acc = acc + part[shift:shift + tc]
            cbuf[n, :, cols] = acc
        return carry
    lax.fori_loop(0, nb, conv_seq, 0)

    tail = gbuf[:, tc + hist0:tc + hist_rows, :]
    ncs_ref[...] = tail
    gbuf[:, hist0:hist_rows, :] = tail

    c = cbuf[...].reshape(rows, d_conv)
    hn = _layer_norm(c, lng_ref[...], lnb_ref[...])
    hdn = hn * jax.nn.sigmoid(hn)
    y = jnp.dot(hdn.astype(BF16), wout_ref[...], preferred_element_type=F32) + bout_ref[...]
    x1 = _layer_norm(alpha * x + y, g1_ref[...], b1_ref[...])
    x1_ref[...] = x1.reshape(nb, tc, d_model)


def _ccm_mixer(x, conv_state, win, b_in, dw_w, dw_b, ln_g, ln_b, wout, bout, g1, b1, *, alpha):
    n, s, d_model = x.shape
    kw, d_conv = dw_w.shape
    tc = min(s, MIXER_ROWS // SUBLANES)
    nb = min(n, MIXER_ROWS // tc)
    while nb > SUBLANES and nb * (kw - 1) * d_conv * 4 > CONV_STATE_BLOCK_BYTES:
        nb //= 2
    rows = nb * tc
    hist_rows = _round_up(kw - 1, SUBLANES)
    conv_cols = min(d_conv, max(LANES, (ACC_VREGS * SUBLANES * LANES // tc) // LANES * LANES))
    while d_conv % conv_cols:
        conv_cols -= LANES
    row = lambda v: v.reshape(1, -1)
    const2 = lambda i, t: (0, 0)
    vmem = (2 * 2 * rows * d_model * 4 + 2 * (win.size + wout.size) * 2
            + (nb * (tc + hist_rows) + rows) * d_conv * 4 + 4 * nb * (kw - 1) * d_conv * 4
            + 4 * rows * 2 * d_conv * 4 + VMEM_HEADROOM_BYTES)
    return pl.pallas_call(
        functools.partial(_ccm_body, alpha=alpha, conv_cols=conv_cols),
        grid=(n // nb, s // tc),
        in_specs=[
            pl.BlockSpec((nb, tc, d_model), lambda i, t: (i, t, 0)),
            pl.BlockSpec((nb, kw - 1, d_conv), lambda i, t: (i, 0, 0)),
            pl.BlockSpec(win.shape, const2), pl.BlockSpec((1, 2 * d_conv), const2),
            pl.BlockSpec(dw_w.shape, const2), pl.BlockSpec((1, d_conv), const2),
            pl.BlockSpec((1, d_conv), const2), pl.BlockSpec((1, d_conv), const2),
            pl.BlockSpec(wout.shape, const2), pl.BlockSpec((1, d_model), const2),
            pl.BlockSpec((1, d_model), const2), pl.BlockSpec((1, d_model), const2),
        ],
        out_specs=[
            pl.BlockSpec((nb, tc, d_model), lambda i, t: (i, t, 0)),
            pl.BlockSpec((nb, kw - 1, d_conv), lambda i, t: (i, 0, 0)),
        ],
        out_shape=[jax.ShapeDtypeStruct(x.shape, F32),
                   jax.ShapeDtypeStruct((n, kw - 1, d_conv), F32)],
        scratch_shapes=[pltpu.VMEM((nb, tc + hist_rows, d_conv), F32),
                        pltpu.VMEM((nb, tc, d_conv), F32)],
        compiler_params=pltpu.CompilerParams(dimension_semantics=("arbitrary", "arbitrary"),
                                             vmem_limit_bytes=min(vmem, VMEM_LIMIT_CAP_BYTES)),
        name="ccm_mixer",
    )(x, conv_state, win, row(b_in), dw_w, row(dw_b), row(ln_g), row(ln_b), wout, row(bout), row(g1), row(b1))


def _first_argmax(v, axis):
    m = jnp.max(v, axis=axis, keepdims=True)
    iota = lax.broadcasted_iota(jnp.int32, v.shape, axis)
    first = jnp.min(jnp.where(v == m, iota, v.shape[axis]), axis=axis, keepdims=True)
    return m, iota == first, first


def _route_body(x_ref, rwt_ref, rb_ref, sg_ref, su_ref, sd_ref, cnt_in_ref,
                resid_ref, idx_ref, w_ref, rank_ref, cnt_ref, tri_s, run_s, *, alpha, steps_per_chunk):
    x = x_ref[...]
    n_exp = rwt_ref.shape[0]
    tr = x.shape[0]
    step = pl.program_id(0)

    @pl.when(step == 0)
    def _build_prefix_matrix():
        r = lax.broadcasted_iota(jnp.int32, (tr, tr), 0)
        c = lax.broadcasted_iota(jnp.int32, (tr, tr), 1)
        tri_s[...] = jnp.where(r <= c, 1.0, 0.0).astype(BF16)

    @pl.when(step % steps_per_chunk == 0)
    def _start_chunk():
        run_s[...] = cnt_in_ref[0]

    def split(v):
        hi = v.astype(BF16)
        return hi, (v - hi.astype(F32)).astype(BF16)
    nt_dot = lambda a, b: lax.dot_general(a, b, (((1,), (1,)), ((), ())), preferred_element_type=F32)
    (w_hi, w_lo), (x_hi, x_lo) = split(rwt_ref[...]), split(x)
    logits = nt_dot(w_hi, x_hi) + (nt_dot(w_hi, x_lo) + nt_dot(w_lo, x_hi))
    scores = jax.nn.sigmoid(logits)
    biased = scores + rb_ref[...]
    per_group = n_exp // N_GROUPS
    grp = biased.reshape(N_GROUPS, per_group, tr)
    m1, hit1, _ = _first_argmax(grp, 1)
    m2 = jnp.max(jnp.where(hit1, -jnp.inf, grp), axis=1, keepdims=True)
    gscore = (m1 + m2).reshape(N_GROUPS, tr)
    gsel = jnp.zeros((N_GROUPS, tr), F32)
    for _ in range(TOPK_GROUPS):
        _, hit, _ = _first_argmax(gscore, 0)
        gsel = jnp.where(hit, 1.0, gsel)
        gscore = jnp.where(hit, -jnp.inf, gscore)
    emask = jnp.broadcast_to(gsel.reshape(N_GROUPS, 1, tr), (N_GROUPS, per_group, tr)).reshape(n_exp, tr)
    masked = jnp.where(emask > 0.0, biased, -jnp.inf)
    idx_rows, w_rows, hits = [], [], []
    for _ in range(TOP_K):
        _, hit, first = _first_argmax(masked, 0)
        idx_rows.append(first)
        hits.append(hit)
        w_rows.append(jnp.sum(jnp.where(hit, scores, 0.0), axis=0, keepdims=True))
        masked = jnp.where(hit, -jnp.inf, masked)
    w = jnp.concatenate(w_rows, axis=0)
    idx_ref[...] = jnp.concatenate(idx_rows, axis=0)
    w_ref[...] = w / jnp.sum(w, axis=0, keepdims=True) * ROUTED_SCALE

    sel = jnp.zeros((n_exp, tr), F32)
    for hit in hits:
        sel = jnp.where(hit, 1.0, sel)
    before = jnp.dot(sel.astype(BF16), tri_s[...], preferred_element_type=F32) - sel + run_s[:, 0:1]
    rank_ref[...] = jnp.concatenate(
        [jnp.sum(jnp.where(hit, before, 0.0), axis=0, keepdims=True) for hit in hits], axis=0).astype(jnp.int32)
    run_s[...] = run_s[...] + jnp.sum(sel, axis=1, keepdims=True)
    cnt_ref[0] = run_s[...]

    xb = x.astype(BF16)
    sgate = jnp.dot(xb, sg_ref[...], preferred_element_type=F32)
    sup = jnp.dot(xb, su_ref[...], preferred_element_type=F32)
    hs = (sgate * jax.nn.sigmoid(sgate) * sup).astype(BF16)
    resid_ref[...] = alpha * x + jnp.dot(hs, sd_ref[...], preferred_element_type=F32)


def _route_shared(x, router_wt, router_bias, sg, su, sd, cnt_in, *, alpha):
    t, d_model = x.shape
    n_exp = router_wt.shape[0]
    rows_per_chunk = t // MOE_CHUNKS
    tr = min(rows_per_chunk, ROUTE_ROWS)
    steps_per_chunk = rows_per_chunk // tr
    const2 = lambda i: (0, 0)
    chunk_map = lambda i: (i // steps_per_chunk, 0, 0)
    vmem = 2 * 2 * tr * d_model * 4 + 2 * (router_wt.size * 4 + 3 * sg.size * 2) + VMEM_HEADROOM_BYTES
    return pl.pallas_call(
        functools.partial(_route_body, alpha=alpha, steps_per_chunk=steps_per_chunk),
        grid=(t // tr,),
        in_specs=[pl.BlockSpec((tr, d_model), lambda i: (i, 0)),
                  pl.BlockSpec(router_wt.shape, const2), pl.BlockSpec((n_exp, 1), const2),
                  pl.BlockSpec(sg.shape, const2), pl.BlockSpec(su.shape, const2), pl.BlockSpec(sd.shape, const2),
                  pl.BlockSpec((1, n_exp, LANES), chunk_map)],
        out_specs=[pl.BlockSpec((tr, d_model), lambda i: (i, 0)),
                   pl.BlockSpec((TOP_K, tr), lambda i: (0, i)),
                   pl.BlockSpec((TOP_K, tr), lambda i: (0, i)),
                   pl.BlockSpec((TOP_K, tr), lambda i: (0, i)),
                   pl.BlockSpec((1, n_exp, LANES), chunk_map)],
        out_shape=[jax.ShapeDtypeStruct((t, d_model), F32),
                   jax.ShapeDtypeStruct((TOP_K, t), jnp.int32),
                   jax.ShapeDtypeStruct((TOP_K, t), F32),
                   jax.ShapeDtypeStruct((TOP_K, t), jnp.int32),
                   jax.ShapeDtypeStruct((MOE_CHUNKS, n_exp, LANES), F32)],
        scratch_shapes=[pltpu.VMEM((tr, tr), BF16), pltpu.VMEM((n_exp, LANES), F32)],
        compiler_params=pltpu.CompilerParams(dimension_semantics=("arbitrary",),
                                             vmem_limit_bytes=min(vmem, VMEM_LIMIT_CAP_BYTES)),
        name="route_shared",
    )(x, router_wt, router_bias.reshape(n_exp, 1), sg, su, sd, cnt_in)


def _moe_body(seg_cnt_ref, seg_base_ref,
              wg_ref, wu_ref, wd_ref, g2_ref, b2_ref,
              tok_hbm, wts_hbm, xp_hbm, xs_hbm, rp_hbm, rs_hbm,
              op_hbm, os_hbm,
              x_s, y_s, xt_s, ot_s, tok_sm, wts_sm, sem_in, sem_out, sem_sm,
              *, cp_rows, cs_rows, npan):
    chunk = cp_rows + cs_rows
    chunk_stride = chunk + SUBLANES
    big = MOE_BIG_TILE
    tile_stride = big + MOE_UNIT + SUBLANES
    n_exp = seg_cnt_ref.shape[0] // MOE_CHUNKS
    seg = pl.program_id(0)
    c = seg // n_exp
    chunk_first = seg % n_exp == 0
    chunk_last = seg % n_exp == n_exp - 1
    n_rows = seg_cnt_ref[seg]
    seg_row0 = seg_base_ref[seg]
    units = (n_rows + MOE_UNIT - 1) // MOE_UNIT
    n_big = units // (big // MOE_UNIT)
    units_left = units % (big // MOE_UNIT)

    def chunk_copies(hbm_p, hbm_s, vmem, sem, to_vmem):
        copies = []
        for p in range(npan):
            for hbm, nrows, row0 in ((hbm_p, cp_rows, 0), (hbm_s, cs_rows, cp_rows)):
                h = hbm.at[pl.ds(c * nrows, nrows), pl.ds(p * LANES, LANES)]
                v = vmem.at[pl.ds(p * chunk_stride + row0, nrows), :]
                copies.append(pltpu.make_async_copy(h, v, sem.at[0]) if to_vmem
                              else pltpu.make_async_copy(v, h, sem.at[0]))
        return copies

    @pl.when(chunk_first)
    def _load_chunk():
        copies = (chunk_copies(xp_hbm, xs_hbm, x_s, sem_in, True)
                  + chunk_copies(rp_hbm, rs_hbm, y_s, sem_in, True))
        for cp in copies:
            cp.start()
        zeros = jnp.zeros((SUBLANES, LANES), F32)
        for p in range(npan):
            x_s[pl.ds(p * chunk_stride + chunk, SUBLANES), :] = zeros
            y_s[pl.ds(p * chunk_stride + chunk, SUBLANES), :] = zeros
        list_rows = tok_sm.shape[0]
        lists = (pltpu.make_async_copy(tok_hbm.at[pl.ds(c * list_rows, list_rows)], tok_sm, sem_sm.at[0]),
                 pltpu.make_async_copy(wts_hbm.at[pl.ds(c * list_rows, list_rows)], wts_sm, sem_sm.at[1]))
        for cp in lists:
            cp.start()
        for cp in lists + tuple(copies):
            cp.wait()

    def process_rows(first, n):
        for i in range(n):
            t = tok_sm[first + i]
            xt_s[pl.ds(i, npan, stride=tile_stride), :] = x_s[pl.ds(t, npan, stride=chunk_stride), :]

        lhs = jnp.concatenate([xt_s[pl.ds(p * tile_stride, n), :] for p in range(npan)], axis=1).astype(BF16)
        g = jnp.dot(lhs, wg_ref[0, 0], preferred_element_type=F32)
        u = jnp.dot(lhs, wu_ref[0, 0], preferred_element_type=F32)
        hidden = (g * jax.nn.sigmoid(g) * u).astype(BF16)
        o = jnp.dot(hidden, wd_ref[0, 0], preferred_element_type=F32)
        for p in range(npan):
            ot_s[pl.ds(p * tile_stride, n), :] = o[:, p * LANES:(p + 1) * LANES]

        for i0 in range(0, n, SCATTER_UNROLL):
            toks, rows_new = [], []
            for i in range(i0, i0 + SCATTER_UNROLL):
                t = tok_sm[first + i]
                w = wts_sm[first + i]
                row = ot_s[pl.ds(i, npan, stride=tile_stride), :]
                toks.append(t)
                rows_new.append(y_s[pl.ds(t, npan, stride=chunk_stride), :] + w * row)
            for t, new in zip(toks, rows_new):
                y_s[pl.ds(t, npan, stride=chunk_stride), :] = new

    extend_last = (n_big >= 1) & (units_left == 1)
    n_plain = n_big - extend_last.astype(jnp.int32)

    def big_tile(i, carry):
        process_rows(seg_row0 + i * big, big)
        return carry
    lax.fori_loop(0, n_plain, big_tile, 0)
    first = seg_row0 + n_plain * big
    pl.when(extend_last)(functools.partial(process_rows, first, big + MOE_UNIT))

    size = big // 2
    while size >= MOE_UNIT:
        needed = ((units_left & (size // MOE_UNIT)) != 0) & jnp.logical_not(extend_last)
        pl.when(needed)(functools.partial(process_rows, first, size))
        first = first + jnp.where(needed, size, 0)
        size //= 2

    @pl.when(chunk_last)
    def _finish_chunk():
        blk = MOE_TILE
        g2 = g2_ref[...]
        b2 = b2_ref[...]

        def ln_block(rb, carry):
            r0 = pl.multiple_of(rb * blk, blk)
            v = jnp.concatenate([y_s[pl.ds(p * chunk_stride + r0, blk), :] for p in range(npan)], axis=1)
            out = _layer_norm(v, g2, b2)
            for p in range(npan):
                y_s[pl.ds(p * chunk_stride + r0, blk), :] = out[:, p * LANES:(p + 1) * LANES]
            return carry
        lax.fori_loop(0, chunk // blk, ln_block, 0)
        copies = chunk_copies(op_hbm, os_hbm, y_s, sem_out, False)
        for cp in copies:
            cp.start()
        for cp in copies:
            cp.wait()


def _list_rows(chunk, top_k, n_exp):
    return (chunk * top_k) // MOE_TILE * MOE_TILE + n_exp * MOE_TILE


def _build_row_lists(dest, wsel, *, chunk, top_k, n_exp):
    n_pairs = top_k * chunk
    list_rows = _list_rows(chunk, top_k, n_exp)
    n_jobs = 2 * MOE_CHUNKS
    assert n_jobs <= SC_CORES * SC_SUBCORES and n_pairs % SC_LANES == 0 and list_rows % SC_LANES == 0
    tokens = jnp.tile(jnp.arange(chunk, dtype=jnp.int32), MOE_CHUNKS * top_k)
    vals = jnp.concatenate([tokens, lax.bitcast_convert_type(wsel, jnp.int32)])
    pad = jnp.concatenate([jnp.full((SC_LANES,), chunk, jnp.int32), jnp.zeros((SC_LANES,), jnp.int32)])
    mesh = plsc.VectorSubcoreMesh(core_axis_name="c", subcore_axis_name="s",
                                  num_cores=SC_CORES, num_subcores=SC_SUBCORES)

    def body(dest_hbm, vals_hbm, pad_hbm, out_hbm, dest_v, vals_v, list_v, pad_v):
        job = lax.axis_index("s") * SC_CORES + lax.axis_index("c")

        @pl.when(job < n_jobs)
        def _scatter_one_list():
            pltpu.sync_copy(dest_hbm.at[pl.ds((job % MOE_CHUNKS) * n_pairs, n_pairs)], dest_v)
            pltpu.sync_copy(vals_hbm.at[pl.ds(job * n_pairs, n_pairs)], vals_v)
            pltpu.sync_copy(pad_hbm.at[pl.ds((job // MOE_CHUNKS) * SC_LANES, SC_LANES)], pad_v)
            pad_vec = pad_v[...]

            @pl.loop(0, list_rows // SC_LANES)
            def _fill(i):
                list_v[pl.ds(i * SC_LANES, SC_LANES)] = pad_vec

            @pl.loop(0, n_pairs // SC_LANES)
            def _scatter(i):
                lanes = pl.ds(i * SC_LANES, SC_LANES)
                plsc.store_scatter(list_v, [dest_v[lanes]], vals_v[lanes])

            pltpu.sync_copy(list_v, out_hbm.at[pl.ds(job * list_rows, list_rows)])

    lists = pl.kernel(
        body, mesh=mesh,
        out_type=jax.ShapeDtypeStruct((n_jobs * list_rows,), jnp.int32),
        scratch_types=[pltpu.VMEM((n_pairs,), jnp.int32), pltpu.VMEM((n_pairs,), jnp.int32),
                       pltpu.VMEM((list_rows,), jnp.int32), pltpu.VMEM((SC_LANES,), jnp.int32)],
        compiler_params=pltpu.CompilerParams(needs_layout_passes=False),
        name="build_row_lists",
    )(dest, vals, pad)
    half = MOE_CHUNKS * list_rows
    return lists[:half], lax.bitcast_convert_type(lists[half:], F32)


def _moe_routed(seg_cnt, seg_base, tok_list, wts_list, wg, wu, wd, g2, b2, xp, xs, rp, rs, *, layer):
    _, n_exp, d_model, d_ff = wg.shape
    npan = d_model // LANES
    assert npan == SUBLANES, "one token row must fill exactly one (8, 128) register"
    assert MOE_TILE % MOE_UNIT == 0 and MOE_BIG_TILE % MOE_TILE == 0
    cp_rows, cs_rows = xp.shape[0] // MOE_CHUNKS, xs.shape[0] // MOE_CHUNKS
    chunk = cp_rows + cs_rows
    assert chunk % MOE_TILE == 0 and cs_rows % SUBLANES == 0
    chunk_stride = chunk + SUBLANES
    tile_stride = MOE_BIG_TILE + MOE_UNIT + SUBLANES
    list_rows = tok_list.shape[0] // MOE_CHUNKS
    any_spec = pl.BlockSpec(memory_space=pl.ANY)
    expert_map = lambda s, *_: (layer, s % n_exp, 0, 0)
    const2 = lambda s, *_: (0, 0)
    expert_spec = lambda shape: pl.BlockSpec((1, 1) + shape, expert_map)
    vmem = (2 * npan * chunk_stride * LANES * 4 + 2 * npan * tile_stride * LANES * 4
            + WEIGHT_BUFFERS * 3 * d_model * d_ff * wg.dtype.itemsize + VMEM_HEADROOM_BYTES)
    return pl.pallas_call(
        functools.partial(_moe_body, cp_rows=cp_rows, cs_rows=cs_rows, npan=npan),
        grid_spec=pltpu.PrefetchScalarGridSpec(
            num_scalar_prefetch=2,
            grid=(MOE_CHUNKS * n_exp,),
            in_specs=[
                expert_spec((d_model, d_ff)),
                expert_spec((d_model, d_ff)),
                expert_spec((d_ff, d_model)),
                pl.BlockSpec((1, d_model), const2),
                pl.BlockSpec((1, d_model), const2),
                any_spec, any_spec, any_spec, any_spec, any_spec, any_spec,
            ],
            out_specs=[any_spec, any_spec],
            scratch_shapes=[
                pltpu.VMEM((npan * chunk_stride, LANES), F32),
                pltpu.VMEM((npan * chunk_stride, LANES), F32),
                pltpu.VMEM((npan * tile_stride, LANES), F32),
                pltpu.VMEM((npan * tile_stride, LANES), F32),
                pltpu.SMEM((list_rows,), jnp.int32),
                pltpu.SMEM((list_rows,), F32),
                pltpu.SemaphoreType.DMA((1,)),
                pltpu.SemaphoreType.DMA((1,)),
                pltpu.SemaphoreType.DMA((2,)),
            ]),
        out_shape=[jax.ShapeDtypeStruct(xp.shape, F32), jax.ShapeDtypeStruct(xs.shape, F32)],
        compiler_params=pltpu.CompilerParams(dimension_semantics=("arbitrary",),
                                             vmem_limit_bytes=min(vmem, VMEM_LIMIT_CAP_BYTES)),
        name="moe_routed",
    )(seg_cnt, seg_base, wg, wu, wd, g2.reshape(1, d_model), b2.reshape(1, d_model),
      tok_list, wts_list, xp, xs, rp, rs)


def _plan_tiles(cnt, idx_p, w_p, rank_p, idx_s, w_s, rank_s):
    k, tp = idx_p.shape
    ts = idx_s.shape[1]
    n_chunks, n_exp = cnt.shape
    tm = MOE_TILE
    cp, cs = tp // n_chunks, ts // n_chunks
    experts = jnp.arange(n_exp, dtype=jnp.int32)

    seg_rows = (cnt + tm - 1) // tm * tm
    seg_base = jnp.sum(jnp.where(experts[None, :] < experts[:, None], seg_rows[:, None, :], 0), axis=-1)

    def rows_of(idx, rank, per_chunk):
        idx3 = idx.reshape(k, n_chunks, per_chunk)
        base = jnp.sum(jnp.where(idx3[..., None] == experts, seg_base[None, :, None, :], 0), axis=-1)
        return jnp.transpose(base + rank.reshape(k, n_chunks, per_chunk), (1, 0, 2))
    by_chunk = lambda a, per_chunk: jnp.transpose(a.reshape(k, n_chunks, per_chunk), (1, 0, 2))
    dest = jnp.concatenate([rows_of(idx_p, rank_p, cp), rows_of(idx_s, rank_s, cs)], axis=2).reshape(-1)
    wsel = jnp.concatenate([by_chunk(w_p, cp), by_chunk(w_s, cs)], axis=2).reshape(-1)
    return cnt.reshape(-1), seg_base.reshape(-1), dest, wsel


def kernel(x_prompt, x_sample, state_lru_conv, state_lru_h, state_ccm_conv, lru_w_in, lru_b_in, lru_conv_w, lru_conv_b, lru_w_a, lru_b_a, lru_w_x, lru_b_x, lru_lambda, lru_w_out, lru_b_out, ccm_w_in, ccm_b_in, ccm_dw_w, ccm_dw_b, ccm_ln_g, ccm_ln_b, ccm_w_out, ccm_b_out, ln1_g, ln1_b, ln2_g, ln2_b, router_w, router_bias, exp_w_gate, exp_w_up, exp_w_down, sh_w_gate, sh_w_up, sh_w_down):
    depth = ln1_g.shape[0]
    alpha = (2 * depth) ** 0.25
    n_exp = router_w.shape[2]
    bp, sp, d_model = x_prompt.shape
    bs, ss, _ = x_sample.shape
    kw_lru = lru_conv_w.shape[1]
    kw_ccm = ccm_dw_w.shape[1]
    d_rnn = lru_conv_w.shape[2]
    d_conv = ccm_dw_w.shape[2]
    bf = lambda a: a.astype(BF16)

    zero_lru_conv = jnp.zeros((bp, kw_lru - 1, d_rnn), F32)
    zero_lru_h = jnp.zeros((bp, d_rnn), F32)
    zero_ccm_conv = jnp.zeros((bp, kw_ccm - 1, d_conv), F32)

    exp_gate_bf, exp_up_bf, exp_down_bf = bf(exp_w_gate), bf(exp_w_up), bf(exp_w_down)

    xp, xs = x_prompt, x_sample
    lru_conv_p, lru_h_p, ccm_conv_p = [], [], []
    lru_conv_s, lru_h_s, ccm_conv_s = [], [], []
    for layer in range(depth):
        j = layer // 2
        if layer % 2 == 0:
            weights = (bf(lru_w_in[j]), lru_b_in[j], lru_conv_w[j], lru_conv_b[j], bf(lru_w_a[j]), lru_b_a[j],
                       bf(lru_w_x[j]), lru_b_x[j], lru_lambda[j], bf(lru_w_out[j]), lru_b_out[j],
                       ln1_g[layer], ln1_b[layer])
            xp, cb, hl = _lru_mixer(xp, zero_lru_conv, zero_lru_h, *weights, seq_start=True, alpha=alpha)
            lru_conv_p.append(cb)
            lru_h_p.append(hl)
            xs, cb, hl = _lru_mixer(xs, state_lru_conv[j], state_lru_h[j], *weights, seq_start=False, alpha=alpha)
            lru_conv_s.append(cb)
            lru_h_s.append(hl)
        else:
            weights = (bf(ccm_w_in[j]), ccm_b_in[j], ccm_dw_w[j], ccm_dw_b[j], ccm_ln_g[j], ccm_ln_b[j],
                       bf(ccm_w_out[j]), ccm_b_out[j], ln1_g[layer], ln1_b[layer])
            xp, cb = _ccm_mixer(xp, zero_ccm_conv, *weights, alpha=alpha)
            ccm_conv_p.append(cb)
            xs, cb = _ccm_mixer(xs, state_ccm_conv[j], *weights, alpha=alpha)
            ccm_conv_s.append(cb)

        shared = (router_w[layer].T, router_bias[layer], bf(sh_w_gate[layer]), bf(sh_w_up[layer]),
                  bf(sh_w_down[layer]))
        xp2 = xp.reshape(bp * sp, d_model)
        xs2 = xs.reshape(bs * ss, d_model)
        no_tokens = jnp.zeros((MOE_CHUNKS, n_exp, LANES), F32)
        resid_p, idx_p, w_p, rank_p, cnt_p = _route_shared(xp2, *shared, no_tokens, alpha=alpha)
        resid_s, idx_s, w_s, rank_s, cnt = _route_shared(xs2, *shared, cnt_p, alpha=alpha)
        seg_cnt, seg_base, dest, wsel = _plan_tiles(cnt[:, :, 0].astype(jnp.int32), idx_p, w_p, rank_p,
                                                    idx_s, w_s, rank_s)
        tok_list, wts_list = _build_row_lists(dest, wsel, chunk=(xp2.shape[0] + xs2.shape[0]) // MOE_CHUNKS,
                                              top_k=idx_p.shape[0], n_exp=n_exp)
        xp2, xs2 = _moe_routed(seg_cnt, seg_base, tok_list, wts_list, exp_gate_bf, exp_up_bf, exp_down_bf,
                               ln2_g[layer], ln2_b[layer], xp2, xs2, resid_p, resid_s, layer=layer)
        xp = xp2.reshape(bp, sp, d_model)
        xs = xs2.reshape(bs, ss, d_model)

    return (xp, xs, jnp.stack(lru_conv_p), jnp.stack(lru_h_p), jnp.stack(ccm_conv_p),
            jnp.stack(lru_conv_s), jnp.stack(lru_h_s), jnp.stack(ccm_conv_s))
```
